```python
import math, functools
import jax, jax.numpy as jnp
from jax import lax
import numpy as np

D_MODEL = 1024
BATCH = 16
SEQ = 4096
DEPTH = 1
DEC_BATCH = 32
DEC_SEQ = 2048
PAST_LEN = 128

D_MIX = D_MODEL
D_ATT = D_MIX // 2
D_CONV = D_MIX - D_ATT
ATT_HEAD_DIM = 64
N_ATT_HEADS = D_ATT // (2 * ATT_HEAD_DIM)
N_MAPS = 2 * N_ATT_HEADS
V_HEAD_DIM = 2 * ATT_HEAD_DIM
CONV_WIDTH = 3
D_IN = 3 * D_ATT + 3 * D_CONV
Q_BLOCK = 128
NUM_BUCKETS = 32
MAX_DISTANCE = 128
N_GROUPS = 4
EXPERTS_PER_GROUP = 4
N_EXPERTS = N_GROUPS * EXPERTS_PER_GROUP
TOP_K_IN_GROUP = 2
D_EXPERT = 256
PLE_DIM = 256
EPS = 1e-6

kernel_name = "hymba_diffattn_shortconv_hmoe_encoder"


def rmsnorm(x, g):
    xf = x.astype(jnp.float32)
    y = xf * lax.rsqrt(jnp.mean(xf * xf, axis=-1, keepdims=True) + EPS)
    return (y * g.astype(jnp.float32)).astype(x.dtype)


def t5_bucket(rel):
    half = NUM_BUCKETS // 2
    max_exact = half // 2
    ret = (rel > 0).astype(jnp.int32) * half
    n = jnp.abs(rel)
    nf = jnp.maximum(n, 1).astype(jnp.float32)
    large = max_exact + (jnp.log(nf / max_exact) / math.log(MAX_DISTANCE / max_exact)
                         * (half - max_exact)).astype(jnp.int32)
    large = jnp.minimum(large, half - 1)
    return ret + jnp.where(n < max_exact, n, large)


def diff_attention(q, k, v, lam, rel_bias):
    B, _, S, d = q.shape
    nb = S // Q_BLOCK
    scale = 1.0 / math.sqrt(ATT_HEAD_DIM)
    q_blocks = q.reshape(B, N_MAPS, nb, Q_BLOCK, d).transpose(2, 0, 1, 3, 4)
    starts = jnp.arange(nb, dtype=jnp.int32) * Q_BLOCK
    k_pos = jnp.arange(S, dtype=jnp.int32)

    def one_block(args):
        qb, start = args
        q_pos = start + jnp.arange(Q_BLOCK, dtype=jnp.int32)
        bucket = t5_bucket(k_pos[None, :] - q_pos[:, None])
        bias = rel_bias[bucket].astype(jnp.float32).transpose(2, 0, 1)
        s = jnp.einsum('bhqd,bhkd->bhqk', qb, k).astype(jnp.float32) * scale + bias[None]
        p = jax.nn.softmax(s, axis=-1).reshape(B, N_ATT_HEADS, 2, Q_BLOCK, S)
        a = (p[:, :, 0] - lam * p[:, :, 1]).astype(v.dtype)
        return jnp.einsum('bhqk,bhkv->bhqv', a, v)

    o = lax.map(one_block, (q_blocks, starts))
    return o.transpose(1, 0, 3, 2, 4).reshape(B, S, N_ATT_HEADS, V_HEAD_DIM)


def short_conv3(h, w):
    hp = jnp.pad(h, ((0, 0), (1, 1), (0, 0)))
    return w[0] * hp[:, :-2] + w[1] * hp[:, 1:-1] + w[2] * hp[:, 2:]


def hier_moe(h, w_group, b_group, w_erouter, b_erouter, w_gate_up, w_down):
    B, S, D = h.shape
    t = h.reshape(B * S, D)
    g_logits = (t @ w_group + b_group).astype(jnp.float32)
    p_group = jax.nn.softmax(g_logits, axis=-1)
    g_idx = jnp.argmax(p_group, axis=-1)
    g_w = jnp.max(p_group, axis=-1)
    e_logits = (t @ w_erouter + b_erouter).astype(jnp.float32).reshape(-1, N_GROUPS, EXPERTS_PER_GROUP)
    e_sel = jnp.take_along_axis(e_logits, g_idx[:, None, None], axis=1)[:, 0]
    p_exp = jax.nn.softmax(e_sel, axis=-1)
    top_v, top_i = lax.top_k(p_exp, TOP_K_IN_GROUP)
    top_v = top_v / jnp.sum(top_v, axis=-1, keepdims=True)
    expert_id = g_idx[:, None] * EXPERTS_PER_GROUP + top_i
    combine = jnp.sum(jax.nn.one_hot(expert_id, N_EXPERTS, dtype=jnp.float32)
                      * (g_w[:, None] * top_v)[..., None], axis=1).astype(h.dtype)
    out = jnp.zeros_like(t)
    for e in range(N_EXPERTS):
        gu = t @ w_gate_up[e]
        y = (jax.nn.silu(gu[:, :D_EXPERT]) * gu[:, D_EXPERT:]) @ w_down[e]
        out = out + combine[:, e:e + 1] * y
    return out.reshape(B, S, D)


def encode(x, p, norm_mix, w_in, q_norm, k_norm, lambda_q1, lambda_k1, lambda_q2, lambda_k2,
           attn_sub_norm, conv_w, w_out, rel_bias, norm_ffn, w_group, b_group, w_erouter, b_erouter,
           w_gate_up, w_down, w_ple_proj, w_ple_gate, ple_norm):
    B, S, _ = x.shape
    for i in range(DEPTH):
        h = rmsnorm(x, norm_mix[i])
        z = h @ w_in[i]
        q, k, v, b_g, c_g, xc = jnp.split(
            z, [D_ATT, 2 * D_ATT, 3 * D_ATT, 3 * D_ATT + D_CONV, 3 * D_ATT + 2 * D_CONV], axis=-1)
        q = rmsnorm(q.reshape(B, S, N_MAPS, ATT_HEAD_DIM), q_norm[i]).transpose(0, 2, 1, 3)
        k = rmsnorm(k.reshape(B, S, N_MAPS, ATT_HEAD_DIM), k_norm[i]).transpose(0, 2, 1, 3)
        v = v.reshape(B, S, N_ATT_HEADS, V_HEAD_DIM).transpose(0, 2, 1, 3)
        lambda_init = 0.8 - 0.6 * math.exp(-0.3 * i)
        lam = (jnp.exp(jnp.sum(lambda_q1[i].astype(jnp.float32) * lambda_k1[i].astype(jnp.float32)))
               - jnp.exp(jnp.sum(lambda_q2[i].astype(jnp.float32) * lambda_k2[i].astype(jnp.float32)))
               + lambda_init)
        o_att = diff_attention(q, k, v, lam, rel_bias)
        o_att = (rmsnorm(o_att, attn_sub_norm[i]) * (1.0 - lambda_init)).reshape(B, S, D_ATT)
        o_conv = b_g * short_conv3(c_g * xc, conv_w[i])
        x = x + jnp.concatenate([o_att, o_conv], axis=-1) @ w_out[i]
        h = rmsnorm(x, norm_ffn[i])
        x = x + hier_moe(h, w_group[i], b_group[i], w_erouter[i], b_erouter[i], w_gate_up[i], w_down[i])
        e = rmsnorm(p[i] @ w_ple_proj[i], ple_norm[i])
        x = x + jax.nn.sigmoid(x @ w_ple_gate[i]) * e
    return x


def setup_inputs(seed: int = 0) -> dict:
    key = jax.random.key(seed)
    ks = jax.random.split(key, 26)
    f32 = jnp.float32
    nrm = lambda k, shape, s: jax.random.normal(k, shape, f32) * s
    gain = lambda k, shape: 1.0 + 0.02 * jax.random.normal(k, shape, f32)
    return {
        "x_prompt": nrm(ks[0], (BATCH, SEQ, D_MODEL), 1.0),
        "x_sample": nrm(ks[1], (DEC_BATCH, DEC_SEQ, D_MODEL), 1.0),
        "p_prompt": nrm(ks[2], (DEPTH, BATCH, SEQ, PLE_DIM), 1.0),
        "p_sample": nrm(ks[3], (DEPTH, DEC_BATCH, DEC_SEQ, PLE_DIM), 1.0),
        "norm_mix": gain(ks[4], (DEPTH, D_MODEL)),
        "w_in": nrm(ks[5], (DEPTH, D_MODEL, D_IN), D_MODEL ** -0.5),
        "q_norm": gain(ks[6], (DEPTH, ATT_HEAD_DIM)),
        "k_norm": gain(ks[7], (DEPTH, ATT_HEAD_DIM)),
        "lambda_q1": nrm(ks[8], (DEPTH, ATT_HEAD_DIM), 0.1),
        "lambda_k1": nrm(ks[9], (DEPTH, ATT_HEAD_DIM), 0.1),
        "lambda_q2": nrm(ks[10], (DEPTH, ATT_HEAD_DIM), 0.1),
        "lambda_k2": nrm(ks[11], (DEPTH, ATT_HEAD_DIM), 0.1),
        "attn_sub_norm": gain(ks[12], (DEPTH, V_HEAD_DIM)),
        "conv_w": nrm(ks[13], (DEPTH, CONV_WIDTH, D_CONV), CONV_WIDTH ** -0.5),
        "w_out": nrm(ks[14], (DEPTH, D_MIX, D_MODEL), D_MIX ** -0.5),
        "rel_bias": nrm(ks[15], (NUM_BUCKETS, N_MAPS), 0.1),
        "norm_ffn": gain(ks[16], (DEPTH, D_MODEL)),
        "w_group": nrm(ks[17], (DEPTH, D_MODEL, N_GROUPS), D_MODEL ** -0.5),
        "b_group": nrm(ks[18], (DEPTH, N_GROUPS), 0.01),
        "w_erouter": nrm(ks[19], (DEPTH, D_MODEL, N_EXPERTS), D_MODEL ** -0.5),
        "b_erouter": nrm(ks[20], (DEPTH, N_EXPERTS), 0.01),
        "w_gate_up": nrm(ks[21], (DEPTH, N_EXPERTS, D_MODEL, 2 * D_EXPERT), D_MODEL ** -0.5),
        "w_down": nrm(ks[22], (DEPTH, N_EXPERTS, D_EXPERT, D_MODEL), D_EXPERT ** -0.5),
        "w_ple_proj": nrm(ks[23], (DEPTH, PLE_DIM, D_MODEL), PLE_DIM ** -0.5),
        "w_ple_gate": nrm(ks[24], (DEPTH, D_MODEL, D_MODEL), D_MODEL ** -0.5),
        "ple_norm": gain(ks[25], (DEPTH, D_MODEL)),
    }


def reference(x_prompt, x_sample, p_prompt, p_sample, norm_mix, w_in, q_norm, k_norm,
              lambda_q1, lambda_k1, lambda_q2, lambda_k2, attn_sub_norm, conv_w, w_out, rel_bias,
              norm_ffn, w_group, b_group, w_erouter, b_erouter, w_gate_up, w_down,
              w_ple_proj, w_ple_gate, ple_norm):
    layer = functools.partial(
        encode, norm_mix=norm_mix, w_in=w_in, q_norm=q_norm, k_norm=k_norm,
        lambda_q1=lambda_q1, lambda_k1=lambda_k1, lambda_q2=lambda_q2, lambda_k2=lambda_k2,
        attn_sub_norm=attn_sub_norm, conv_w=conv_w, w_out=w_out, rel_bias=rel_bias,
        norm_ffn=norm_ffn, w_group=w_group, b_group=b_group, w_erouter=w_erouter,
        b_erouter=b_erouter, w_gate_up=w_gate_up, w_down=w_down,
        w_ple_proj=w_ple_proj, w_ple_gate=w_ple_gate, ple_norm=ple_norm)
    y_prompt = layer(x_prompt, p_prompt)
    y_sample = layer(x_sample, p_sample)
    return (y_prompt, y_sample)
```

```python
import functools
import math

import jax
import jax.numpy as jnp
from jax import lax
from jax.experimental import pallas as pl
from jax.experimental.pallas import tpu as pltpu

_F32 = jnp.float32
_BF16 = jnp.bfloat16

_EPS = 1e-6
_ATT_HEAD_DIM = 64
_MAX_DISTANCE = 128
_EXPERTS_PER_GROUP = 4
_LANES = 128
_BF16_SUBLANES = 16
_ATTN_TILE = 512
_TOKEN_TILE = 512
_VMEM_LIMIT_BYTES = 56 * 1024 * 1024
_NEG_BIG = -1e30


def _cparams(*sem):
    return pltpu.CompilerParams(dimension_semantics=sem, vmem_limit_bytes=_VMEM_LIMIT_BYTES)


def _dot(a, b):
    return jnp.dot(a, b, preferred_element_type=_F32)


def _bias_tile_kernel(tbl_ref, out_ref, *, tile, num_buckets):
    di = pl.program_id(0)
    m = pl.program_id(1)
    delta = (di - 2) * tile
    kk = lax.broadcasted_iota(jnp.int32, (tile, tile), 0)
    qq = lax.broadcasted_iota(jnp.int32, (tile, tile), 1)
    rel = kk - qq + delta
    half = num_buckets // 2
    max_exact = half // 2
    ret = jnp.where(rel > 0, half, 0)
    n = jnp.abs(rel)
    nf = jnp.maximum(n, 1).astype(_F32)
    large = max_exact + (jnp.log(nf / max_exact) / math.log(_MAX_DISTANCE / max_exact)
                         * (half - max_exact)).astype(jnp.int32)
    large = jnp.minimum(large, half - 1)
    bucket = ret + jnp.where(n < max_exact, n, large)
    acc = jnp.zeros((tile, tile), _F32)
    for b in range(num_buckets):
        acc = jnp.where(bucket == b, tbl_ref[b, m], acc)
    out_ref[0, 0] = acc


def _bias_tiles(rel_bias, tile):
    num_buckets, n_maps = rel_bias.shape
    assert tile >= _MAX_DISTANCE
    return pl.pallas_call(
        functools.partial(_bias_tile_kernel, tile=tile, num_buckets=num_buckets),
        grid=(5, n_maps),
        in_specs=[pl.BlockSpec(memory_space=pltpu.SMEM)],
        out_specs=pl.BlockSpec((1, 1, tile, tile), lambda di, m: (di, m // 2, 0, m % 2)),
        out_shape=jax.ShapeDtypeStruct((5, n_maps // 2, tile, 2 * tile), _F32),
        compiler_params=_cparams("arbitrary", "arbitrary"),
        name="bias_tiles",
    )(rel_bias.astype(_F32))


def _inproj_kernel(x_ref, gmix_ref, w_ref, gq_ref, gk_ref, bd_ref,
                   qT_ref, k_ref, vT_ref, b_ref, u_ref, *, d_att, d_conv, scale):
    x = x_ref[0]
    ms = jnp.mean(x * x, axis=-1, keepdims=True)
    h = (x * lax.rsqrt(ms + _EPS) * gmix_ref[...]).astype(_BF16)

    def proj(lo, n):
        return _dot(h, w_ref[:, lo:lo + n])

    tm = x.shape[0]
    n_maps = d_att // _ATT_HEAD_DIM
    zqT = proj(0, d_att).T.reshape(n_maps, _ATT_HEAD_DIM, tm)
    qms = jnp.mean(zqT * zqT, axis=1, keepdims=True)
    qn = zqT * lax.rsqrt(qms + _EPS) * gq_ref[...][None]
    qT_ref[0] = (qn * scale).reshape(d_att, tm).astype(_BF16)
    zk = proj(d_att, d_att)
    kms = _dot((zk * zk).astype(_BF16), bd_ref[...])
    k_ref[0] = (zk * lax.rsqrt(kms + _EPS) * gk_ref[...]).astype(_BF16)
    vT_ref[0] = proj(2 * d_att, d_att).T.astype(_BF16)
    b_ref[0] = proj(3 * d_att, d_conv).astype(_BF16)
    c = proj(3 * d_att + d_conv, d_conv)
    xc = proj(3 * d_att + 2 * d_conv, d_conv)
    u_ref[0] = (c * xc).astype(_BF16)


def _inproj(x, gmix, w_in, gq_col, gk_row, bd, *, d_att, d_conv, tm):
    B, S, D = x.shape
    d_in = w_in.shape[1]
    grid = (B, S // tm)
    tok = lambda width: pl.BlockSpec((1, tm, width), lambda b, i: (b, i, 0))
    tr = lambda rows: pl.BlockSpec((1, rows, tm), lambda b, i: (b, 0, i))
    full = lambda shape: pl.BlockSpec(shape, lambda b, i: (0,) * len(shape))
    return pl.pallas_call(
        functools.partial(_inproj_kernel, d_att=d_att, d_conv=d_conv,
                          scale=1.0 / math.sqrt(_ATT_HEAD_DIM)),
        grid=grid,
        in_specs=[tok(D), full((1, D)), full((D, d_in)), full((_ATT_HEAD_DIM, 1)), full((1, d_att)),
                  full((d_att, d_att))],
        out_specs=[tr(d_att), tok(d_att), tr(d_att), tok(d_conv), tok(d_conv)],
        out_shape=[jax.ShapeDtypeStruct((B, d_att, S), _BF16),
                   jax.ShapeDtypeStruct((B, S, d_att), _BF16),
                   jax.ShapeDtypeStruct((B, d_att, S), _BF16),
                   jax.ShapeDtypeStruct((B, S, d_conv), _BF16),
                   jax.ShapeDtypeStruct((B, S, d_conv), _BF16)],
        compiler_params=_cparams("parallel", "parallel"),
        name="inproj",
    )(x, gmix, w_in, gq_col, gk_row, bd)


def _attention_kernel(lam_ref, qT_ref, k_ref, vT_ref, bias_ref, gsub_ref, o_ref,
                      rhs_ref, m_ref, l_ref, acc_ref, *, tile, out_scale):
    qi = pl.program_id(2)
    n_k = k_ref.shape[1] // tile
    hd = _ATT_HEAD_DIM
    qT = qT_ref[0]
    row = lax.broadcasted_iota(jnp.int32, qT.shape, 0)
    zero = jnp.zeros_like(qT)
    rhs_ref[:, :tile] = jnp.where(row < hd, qT, zero)
    rhs_ref[:, tile:] = jnp.where(row >= hd, qT, zero)
    m_ref[...] = jnp.full(m_ref.shape, _NEG_BIG, _F32)
    l_ref[...] = jnp.zeros(l_ref.shape, _F32)
    acc_ref[...] = jnp.zeros(acc_ref.shape, _F32)

    def step(kj, carry):
        off = pl.multiple_of(kj * tile, tile)
        bias = bias_ref[jnp.clip(kj - qi, -2, 2) + 2, 0]
        s = _dot(k_ref[0, pl.ds(off, tile), :], rhs_ref[...]) + bias
        m_old = m_ref[...]
        m_new = jnp.maximum(m_old, jnp.max(s, axis=0, keepdims=True))
        alpha = jnp.exp(m_old - m_new)
        p = jnp.exp(s - m_new)
        l_ref[...] = alpha * l_ref[...] + jnp.sum(p, axis=0, keepdims=True)
        pv = _dot(vT_ref[0, :, pl.ds(off, tile)], p.astype(_BF16))
        acc_ref[...] = alpha * acc_ref[...] + pv
        m_ref[...] = m_new
        return carry

    lax.fori_loop(0, n_k, step, 0)

    lam = lam_ref[0]
    inv_l = 1.0 / l_ref[...]
    o = acc_ref[...] * inv_l
    oT = o[:, :tile] - lam * o[:, tile:]
    ms = jnp.mean(oT * oT, axis=0, keepdims=True)
    oT = oT * lax.rsqrt(ms + _EPS) * gsub_ref[...] * out_scale
    o_ref[0] = oT.T.astype(_BF16)


def _lambda_kernel(q1_ref, k1_ref, q2_ref, k2_ref, out_ref, *, lambda_init):
    a = jnp.sum(q1_ref[...] * k1_ref[...], axis=-1, keepdims=True)
    b = jnp.sum(q2_ref[...] * k2_ref[...], axis=-1, keepdims=True)
    out_ref[...] = jnp.broadcast_to(jnp.exp(a) - jnp.exp(b) + lambda_init, out_ref.shape)


def _lambda(q1, k1, q2, k2, lambda_init):
    out = pl.pallas_call(
        functools.partial(_lambda_kernel, lambda_init=lambda_init),
        out_shape=jax.ShapeDtypeStruct((1, _LANES), _F32),
        name="lambda_scalar",
    )(q1, k1, q2, k2)
    return out[0, :1]


def _attention(lam, qT, k, vT, bias_tiles, gsub_col, *, tile, out_scale):
    B, d_att, S = qT.shape
    v_dim = 2 * _ATT_HEAD_DIM
    n_heads = d_att // v_dim
    grid = (n_heads, B, S // tile)
    return pl.pallas_call(
        functools.partial(_attention_kernel, tile=tile, out_scale=out_scale),
        grid=grid,
        in_specs=[pl.BlockSpec(memory_space=pltpu.SMEM),
                  pl.BlockSpec((1, v_dim, tile), lambda h, b, i: (b, h, i)),
                  pl.BlockSpec((1, S, v_dim), lambda h, b, i: (b, 0, h)),
                  pl.BlockSpec((1, v_dim, S), lambda h, b, i: (b, h, 0)),
                  pl.BlockSpec((5, 1, tile, 2 * tile), lambda h, b, i: (0, h, 0, 0)),
                  pl.BlockSpec((v_dim, 1), lambda h, b, i: (0, 0))],
        out_specs=pl.BlockSpec((1, tile, v_dim), lambda h, b, i: (b, i, h)),
        out_shape=jax.ShapeDtypeStruct((B, S, d_att), _BF16),
        scratch_shapes=[pltpu.VMEM((v_dim, 2 * tile), _BF16),
                        pltpu.VMEM((1, 2 * tile), _F32),
                        pltpu.VMEM((1, 2 * tile), _F32),
                        pltpu.VMEM((v_dim, 2 * tile), _F32)],
        compiler_params=_cparams("parallel", "parallel", "parallel"),
        name="diff_attention",
    )(lam, qT, k, vT, bias_tiles, gsub_col)


def _outproj_kernel(oatt_ref, b_ref, u_ref, uprev_ref, unext_ref, x_ref, cw_ref, wout_ref, gffn_ref,
                    wr_ref, br_ref, x1_ref, h2_ref, comb_ref, *, d_att, n_experts):
    i = pl.program_id(1)
    n_i = pl.num_programs(1)
    u = u_ref[0].astype(_F32)
    tm = u.shape[0]
    prev_row = jnp.where(i > 0, uprev_ref[0, _BF16_SUBLANES - 1:_BF16_SUBLANES, :].astype(_F32), 0.0)
    next_row = jnp.where(i < n_i - 1, unext_ref[0, 0:1, :].astype(_F32), 0.0)
    rows = lax.broadcasted_iota(jnp.int32, (tm, 1), 0)
    u_prev = jnp.where(rows == 0, prev_row, pltpu.roll(u, 1, axis=0))
    u_next = jnp.where(rows == tm - 1, next_row, pltpu.roll(u, tm - 1, axis=0))
    conv = cw_ref[0:1, :] * u_prev + cw_ref[1:2, :] * u + cw_ref[2:3, :] * u_next
    o_conv = (b_ref[0].astype(_F32) * conv).astype(_BF16)
    mix = _dot(oatt_ref[0], wout_ref[:d_att, :]) + _dot(o_conv, wout_ref[d_att:, :])
    x1 = x_ref[0] + mix
    x1_ref[0] = x1
    ms = jnp.mean(x1 * x1, axis=-1, keepdims=True)
    t = x1 * lax.rsqrt(ms + _EPS) * gffn_ref[...]
    t_hi = t.astype(_BF16)
    h2_ref[0] = t_hi
    t_lo = (t - t_hi.astype(_F32)).astype(_BF16)
    a_hi = _dot(t_hi, wr_ref[...])
    a_lo = _dot(t_lo, wr_ref[...])
    logits = a_hi[:, :_LANES] + a_hi[:, _LANES:] + a_lo[:, :_LANES] + br_ref[...]
    n_groups = n_experts // _EXPERTS_PER_GROUP
    lane = lax.broadcasted_iota(jnp.int32, (1, _LANES), 1)
    lane_f = lane.astype(_F32)
    big = float(_LANES)
    gmask = (lane >= n_experts) & (lane < n_experts + n_groups)
    gl = jnp.where(gmask, logits, _NEG_BIG)
    gmax = jnp.max(gl, axis=-1, keepdims=True)
    gsum = jnp.sum(jnp.where(gmask, jnp.exp(gl - gmax), 0.0), axis=-1, keepdims=True)
    g_w = 1.0 / gsum
    g_idx = jnp.min(jnp.where(gmask & (gl == gmax), lane_f - n_experts, big), axis=-1, keepdims=True)
    lo = g_idx * _EXPERTS_PER_GROUP
    emask = (lane_f >= lo) & (lane_f < lo + _EXPERTS_PER_GROUP)
    el = jnp.where(emask, logits, _NEG_BIG)
    emax = jnp.max(el, axis=-1, keepdims=True)
    ep = jnp.where(emask, jnp.exp(el - emax), 0.0)
    p_exp = ep / jnp.sum(ep, axis=-1, keepdims=True)
    top1 = jnp.max(p_exp, axis=-1, keepdims=True)
    i1 = jnp.min(jnp.where(emask & (p_exp == top1), lane_f, big), axis=-1, keepdims=True)
    rest = jnp.where(emask & (lane_f != i1), p_exp, -1.0)
    top2 = jnp.max(rest, axis=-1, keepdims=True)
    i2 = jnp.min(jnp.where(rest == top2, lane_f, big), axis=-1, keepdims=True)
    denom = top1 + top2
    comb_ref[0] = jnp.where(lane_f == i1, g_w * (top1 / denom),
                            jnp.where(lane_f == i2, g_w * (top2 / denom), 0.0))


def _outproj(oatt, b, u, x, conv_w, w_out, gffn, wr, br, *, n_experts, tm):
    B, S, D = x.shape
    d_att = oatt.shape[-1]
    d_conv = b.shape[-1]
    hb = _BF16_SUBLANES
    per_tile = tm // hb
    n_halo = S // hb
    tok = lambda width: pl.BlockSpec((1, tm, width), lambda bb, i: (bb, i, 0))
    full = lambda shape: pl.BlockSpec(shape, lambda bb, i: (0,) * len(shape))
    return pl.pallas_call(
        functools.partial(_outproj_kernel, d_att=d_att, n_experts=n_experts),
        grid=(B, S // tm),
        in_specs=[tok(d_att), tok(d_conv), tok(d_conv),
                  pl.BlockSpec((1, hb, d_conv), lambda bb, i: (bb, jnp.maximum(i * per_tile - 1, 0), 0)),
                  pl.BlockSpec((1, hb, d_conv),
                               lambda bb, i: (bb, jnp.minimum((i + 1) * per_tile, n_halo - 1), 0)),
                  tok(D), full(conv_w.shape), full(w_out.shape), full((1, D)), full(wr.shape),
                  full((1, _LANES))],
        out_specs=[tok(D), tok(D), tok(_LANES)],
        out_shape=[jax.ShapeDtypeStruct((B, S, D), _F32),
                   jax.ShapeDtypeStruct((B, S, D), _BF16),
                   jax.ShapeDtypeStruct((B, S, _LANES), _F32)],
        compiler_params=_cparams("parallel", "parallel"),
        name="outproj_router",
    )(oatt, b, u, u, u, x, conv_w, w_out, gffn, wr, br)


def _moe_ple_kernel(h2_ref, comb_ref, x1_ref, p_ref, wgu_ref, wdn_ref, wproj_ref, wgate_ref, gple_ref,
                    y_ref, acc_ref, *, d_expert):
    g = pl.program_id(1)
    n_g = pl.num_programs(1)
    width = _EXPERTS_PER_GROUP * d_expert

    @pl.when(g == 0)
    def _():
        acc_ref[...] = jnp.zeros(acc_ref.shape, _F32)

    gu = _dot(h2_ref[...], wgu_ref[0])
    gate = gu[:, :width]
    act = gate * jax.nn.sigmoid(gate) * gu[:, width:]
    comb = comb_ref[...]
    lane = lax.broadcasted_iota(jnp.int32, (1, _LANES), 1)
    parts = []
    for e in range(_EXPERTS_PER_GROUP):
        w_e = jnp.sum(jnp.where(lane == g * _EXPERTS_PER_GROUP + e, comb, 0.0), axis=-1, keepdims=True)
        parts.append((act[:, e * d_expert:(e + 1) * d_expert] * w_e).astype(_BF16))
    acc_ref[...] += _dot(jnp.concatenate(parts, axis=1), wdn_ref[0])

    @pl.when(g == n_g - 1)
    def _():
        x2 = x1_ref[...] + acc_ref[...]
        e_raw = _dot(p_ref[...].astype(_BF16), wproj_ref[...])
        ms = jnp.mean(e_raw * e_raw, axis=-1, keepdims=True)
        emb = e_raw * lax.rsqrt(ms + _EPS) * gple_ref[...]
        gate_p = jax.nn.sigmoid(_dot(x2.astype(_BF16), wgate_ref[...]))
        y_ref[...] = x2 + gate_p * emb


def _moe_ple(h2, comb, x1, p, wgu, wdn, wproj, wgate, gple, *, d_expert, tm):
    T, D = x1.shape
    n_groups = wgu.shape[0]
    tok = lambda width: pl.BlockSpec((tm, width), lambda i, g: (i, 0))
    full = lambda shape: pl.BlockSpec(shape, lambda i, g: (0,) * len(shape))
    grp = lambda shape: pl.BlockSpec((1,) + shape, lambda i, g: (g, 0, 0))
    return pl.pallas_call(
        functools.partial(_moe_ple_kernel, d_expert=d_expert),
        grid=(T // tm, n_groups),
        in_specs=[tok(D), tok(_LANES), tok(D), tok(p.shape[-1]),
                  grp(wgu.shape[1:]), grp(wdn.shape[1:]), full(wproj.shape), full(wgate.shape),
                  full((1, D))],
        out_specs=tok(D),
        out_shape=jax.ShapeDtypeStruct((T, D), _F32),
        scratch_shapes=[pltpu.VMEM((tm, D), _F32)],
        compiler_params=_cparams("parallel", "arbitrary"),
        name="moe_ple",
    )(h2, comb, x1, p, wgu, wdn, wproj, wgate, gple)


def _prepare_layer(i, norm_mix, w_in, q_norm, k_norm, lambda_q1, lambda_k1, lambda_q2, lambda_k2,
                   attn_sub_norm, conv_w, w_out, norm_ffn, w_group, b_group, w_erouter, b_erouter,
                   w_gate_up, w_down, w_ple_proj, w_ple_gate, ple_norm):
    D = w_in.shape[1]
    d_mix = w_out.shape[1]
    d_att = d_mix // 2
    d_conv = d_mix - d_att
    n_maps = d_att // _ATT_HEAD_DIM
    n_experts = w_gate_up.shape[1]
    n_groups = w_group.shape[-1]
    d_expert = w_down.shape[2]
    assert n_experts == n_groups * _EXPERTS_PER_GROUP and n_experts + n_groups <= _LANES
    lambda_init = 0.8 - 0.6 * math.exp(-0.3 * i)
    head_of = jnp.arange(d_att) // _ATT_HEAD_DIM
    bd = jnp.where(head_of[:, None] == head_of[None, :], 1.0 / _ATT_HEAD_DIM, 0.0).astype(_BF16)
    wr = jnp.zeros((D, _LANES), _F32)
    wr = wr.at[:, :n_experts].set(w_erouter[i]).at[:, n_experts:n_experts + n_groups].set(w_group[i])
    wr_hi = wr.astype(_BF16)
    wr_lo = (wr - wr_hi.astype(_F32)).astype(_BF16)
    br = jnp.zeros((1, _LANES), _F32)
    br = br.at[0, :n_experts].set(b_erouter[i]).at[0, n_experts:n_experts + n_groups].set(b_group[i])
    wg = w_gate_up[i].reshape(n_groups, _EXPERTS_PER_GROUP, D, 2, d_expert)
    wgu = jnp.transpose(wg, (0, 2, 3, 1, 4)).reshape(n_groups, D, 2 * _EXPERTS_PER_GROUP * d_expert)
    wdn = w_down[i].reshape(n_groups, _EXPERTS_PER_GROUP * d_expert, D)
    return dict(
        d_att=d_att, d_conv=d_conv, n_experts=n_experts, d_expert=d_expert, lambda_init=lambda_init,
        gmix=norm_mix[i][None, :], w_in=w_in[i].astype(_BF16),
        gq_col=q_norm[i][:, None], gk_row=jnp.tile(k_norm[i], n_maps)[None, :], bd=bd,
        lam_vecs=tuple(v[i][None, :] for v in (lambda_q1, lambda_k1, lambda_q2, lambda_k2)),
        gsub_col=attn_sub_norm[i][:, None], conv_w=conv_w[i], w_out=w_out[i].astype(_BF16),
        gffn=norm_ffn[i][None, :], wr=jnp.concatenate([wr_hi, wr_lo], axis=1), br=br,
        wgu=wgu.astype(_BF16), wdn=wdn.astype(_BF16), wproj=w_ple_proj[i].astype(_BF16),
        wgate=w_ple_gate[i].astype(_BF16), gple=ple_norm[i][None, :])


def _layer(x, p_i, L, bias_tiles, lam):
    B, S, D = x.shape
    tm = _TOKEN_TILE
    tile = _ATTN_TILE
    assert S % tm == 0 and S % tile == 0 and tm % _BF16_SUBLANES == 0
    qT, k, vT, b, u = _inproj(x, L["gmix"], L["w_in"], L["gq_col"], L["gk_row"], L["bd"],
                              d_att=L["d_att"], d_conv=L["d_conv"], tm=tm)
    oatt = _attention(lam, qT, k, vT, bias_tiles, L["gsub_col"], tile=tile,
                      out_scale=1.0 - L["lambda_init"])
    x1, h2, comb = _outproj(oatt, b, u, x, L["conv_w"], L["w_out"], L["gffn"], L["wr"], L["br"],
                            n_experts=L["n_experts"], tm=tm)
    y = _moe_ple(h2.reshape(B * S, D), comb.reshape(B * S, _LANES), x1.reshape(B * S, D),
                 p_i.reshape(B * S, -1), L["wgu"], L["wdn"], L["wproj"], L["wgate"], L["gple"],
                 d_expert=L["d_expert"], tm=tm)
    return y.reshape(B, S, D)


def kernel(x_prompt, x_sample, p_prompt, p_sample, norm_mix, w_in, q_norm, k_norm, lambda_q1, lambda_k1, lambda_q2, lambda_k2, attn_sub_norm, conv_w, w_out, rel_bias, norm_ffn, w_group, b_group, w_erouter, b_erouter, w_gate_up, w_down, w_ple_proj, w_ple_gate, ple_norm):
    depth = w_in.shape[0]
    bias_tiles = _bias_tiles(rel_bias, _ATTN_TILE)
    layers = []
    for i in range(depth):
        L = _prepare_layer(i, norm_mix, w_in, q_norm, k_norm, lambda_q1, lambda_k1, lambda_q2,
                           lambda_k2, attn_sub_norm, conv_w, w_out, norm_ffn, w_group, b_group,
                           w_erouter, b_erouter, w_gate_up, w_down, w_ple_proj, w_ple_gate, ple_norm)
        layers.append((L, _lambda(*L["lam_vecs"], L["lambda_init"])))

    def encode(x, p):
        for i, (L, lam) in enumerate(layers):
            x = _layer(x, p[i], L, bias_tiles, lam)
        return x

    return (encode(x_prompt, p_prompt), encode(x_sample, p_sample))
```

```python
import functools
import math

import jax
import jax.numpy as jnp
from jax import lax
from jax.experimental import pallas as pl
from jax.experimental.pallas import tpu as pltpu

_F32 = jnp.float32
_BF16 = jnp.bfloat16

_EPS = 1e-6
_ATT_HEAD_DIM = 64
_MAX_DISTANCE = 128
_EXPERTS_PER_GROUP = 4
_LANES = 128
_BF16_SUBLANES = 16
_ATTN_TILE = 512
_ATTN_ROWS = 128
_ATTN_LANES = 256
_TOKEN_TILE = 512
_VMEM_LIMIT_BYTES = 56 * 1024 * 1024
_NEG_BIG = -1e30
_LOG2E = math.log2(math.e)


def _cparams(*sem):
    return pltpu.CompilerParams(dimension_semantics=sem, vmem_limit_bytes=_VMEM_LIMIT_BYTES)


def _dot(a, b):
    return jnp.dot(a, b, preferred_element_type=_F32)


def _bias_tile_kernel(tbl_ref, out_ref, *, tile, num_buckets):
    di = pl.program_id(0)
    m = pl.program_id(1)
    delta = (di - 2) * tile
    kk = lax.broadcasted_iota(jnp.int32, (tile, tile), 0)
    qq = lax.broadcasted_iota(jnp.int32, (tile, tile), 1)
    rel = kk - qq + delta
    half = num_buckets // 2
    max_exact = half // 2
    ret = jnp.where(rel > 0, half, 0)
    n = jnp.abs(rel)
    nf = jnp.maximum(n, 1).astype(_F32)
    large = max_exact + (jnp.log(nf / max_exact) / math.log(_MAX_DISTANCE / max_exact)
                         * (half - max_exact)).astype(jnp.int32)
    large = jnp.minimum(large, half - 1)
    bucket = ret + jnp.where(n < max_exact, n, large)
    acc = jnp.zeros((tile, tile), _F32)
    for b in range(num_buckets):
        acc = jnp.where(bucket == b, tbl_ref[b, m], acc)
    out_ref[0, 0] = acc * _LOG2E


def _bias_tiles(rel_bias, tile):
    num_buckets, n_maps = rel_bias.shape
    assert tile >= _MAX_DISTANCE
    return pl.pallas_call(
        functools.partial(_bias_tile_kernel, tile=tile, num_buckets=num_buckets),
        grid=(5, n_maps),
        in_specs=[pl.BlockSpec(memory_space=pltpu.SMEM)],
        out_specs=pl.BlockSpec((1, 1, tile, tile), lambda di, m: (di, m // 2, 0, m % 2)),
        out_shape=jax.ShapeDtypeStruct((5, n_maps // 2, tile, 2 * tile), _F32),
        compiler_params=_cparams("arbitrary", "arbitrary"),
        name="bias_tiles",
    )(rel_bias.astype(_F32))


def _inproj_kernel(x_ref, gmix_ref, w_ref, gq_ref, gk_ref, bd_ref,
                   qT_ref, k_ref, vT_ref, b_ref, u_ref, *, d_att, d_conv, scale):
    x = x_ref[0]
    ms = jnp.mean(x * x, axis=-1, keepdims=True)
    h = (x * lax.rsqrt(ms + _EPS) * gmix_ref[...]).astype(_BF16)

    def proj(lo, n):
        return _dot(h, w_ref[:, lo:lo + n])

    tm = x.shape[0]
    n_maps = d_att // _ATT_HEAD_DIM
    zqT = proj(0, d_att).T.reshape(n_maps, _ATT_HEAD_DIM, tm)
    qms = jnp.mean(zqT * zqT, axis=1, keepdims=True)
    qn = zqT * lax.rsqrt(qms + _EPS) * gq_ref[...][None]
    qT_ref[0] = (qn * scale).reshape(d_att, tm).astype(_BF16)
    zk = proj(d_att, d_att)
    kms = _dot((zk * zk).astype(_BF16), bd_ref[...])
    k_ref[0] = (zk * lax.rsqrt(kms + _EPS) * gk_ref[...]).astype(_BF16)
    vT_ref[0] = proj(2 * d_att, d_att).T.astype(_BF16)
    b_ref[0] = proj(3 * d_att, d_conv).astype(_BF16)
    c = proj(3 * d_att + d_conv, d_conv)
    xc = proj(3 * d_att + 2 * d_conv, d_conv)
    u_ref[0] = (c * xc).astype(_BF16)


def _inproj(x, gmix, w_in, gq_col, gk_row, bd, *, d_att, d_conv, tm):
    B, S, D = x.shape
    d_in = w_in.shape[1]
    grid = (B, S // tm)
    tok = lambda width: pl.BlockSpec((1, tm, width), lambda b, i: (b, i, 0))
    tr = lambda rows: pl.BlockSpec((1, rows, tm), lambda b, i: (b, 0, i))
    full = lambda shape: pl.BlockSpec(shape, lambda b, i: (0,) * len(shape))
    return pl.pallas_call(
        functools.partial(_inproj_kernel, d_att=d_att, d_conv=d_conv,
                          scale=_LOG2E / math.sqrt(_ATT_HEAD_DIM)),
        grid=grid,
        in_specs=[tok(D), full((1, D)), full((D, d_in)), full((_ATT_HEAD_DIM, 1)), full((1, d_att)),
                  full((d_att, d_att))],
        out_specs=[tr(d_att), tok(d_att), tr(d_att), tok(d_conv), tok(d_conv)],
        out_shape=[jax.ShapeDtypeStruct((B, d_att, S), _BF16),
                   jax.ShapeDtypeStruct((B, S, d_att), _BF16),
                   jax.ShapeDtypeStruct((B, d_att, S), _BF16),
                   jax.ShapeDtypeStruct((B, S, d_conv), _BF16),
                   jax.ShapeDtypeStruct((B, S, d_conv), _BF16)],
        compiler_params=_cparams("parallel", "parallel"),
        name="inproj",
    )(x, gmix, w_in, gq_col, gk_row, bd)


def _attention_kernel(lam_ref, qT_ref, k_ref, vT_ref, bias_ref, gsub_ref, o_ref,
                      rhs_ref, s0_ref, s1_ref, p0_ref, p1_ref, mx0_ref, mx1_ref, al0_ref, al1_ref,
                      m_ref, l_ref, acc_ref, *, tile, out_scale):
    qi = pl.program_id(2)
    n_k = k_ref.shape[1] // tile
    hd = _ATT_HEAD_DIM
    s_refs, p_refs = (s0_ref, s1_ref), (p0_ref, p1_ref)
    mx_refs, al_refs = (mx0_ref, mx1_ref), (al0_ref, al1_ref)
    qT = qT_ref[0]
    row = lax.broadcasted_iota(jnp.int32, qT.shape, 0)
    zero = jnp.zeros_like(qT)
    rhs_ref[:, :tile] = jnp.where(row < hd, qT, zero)
    rhs_ref[:, tile:] = jnp.where(row >= hd, qT, zero)
    m_ref[...] = jnp.full(m_ref.shape, _NEG_BIG, _F32)
    l_ref[...] = jnp.zeros(l_ref.shape, _F32)
    acc_ref[...] = jnp.zeros(acc_ref.shape, _F32)

    row_blocks = [slice(r, r + _ATTN_ROWS) for r in range(0, tile, _ATTN_ROWS)]
    lane_blocks = [slice(c, c + _ATTN_LANES) for c in range(0, 2 * tile, _ATTN_LANES)]

    def fold8(x, op):
        return op(x.reshape(x.shape[0] // 8, 8, x.shape[1]), axis=0)

    def stage_a(kj, par):
        off = pl.multiple_of(kj * tile, tile)
        bidx = jnp.clip(kj - qi, -2, 2) + 2
        for cs in lane_blocks:
            rhs_c = rhs_ref[:, cs]
            mx = None
            for rs in row_blocks:
                k_blk = k_ref[0, pl.ds(off + rs.start, _ATTN_ROWS), :]
                s = _dot(k_blk, rhs_c) + bias_ref[bidx, 0, rs, cs]
                s_refs[par][rs, cs] = s
                part = fold8(s, jnp.max)
                mx = part if mx is None else jnp.maximum(mx, part)
            mx_refs[par][:, cs] = jnp.max(mx, axis=0, keepdims=True)

    def stage_b(par):
        m_old = m_ref[...]
        m_new = jnp.maximum(m_old, mx_refs[par][...])
        alpha = jnp.exp2(m_old - m_new)
        al_refs[par][...] = alpha
        m_ref[...] = m_new
        for cs in lane_blocks:
            m_c = m_new[:, cs]
            lsum = None
            for rs in row_blocks:
                p = jnp.exp2(s_refs[par][rs, cs] - m_c)
                p_refs[par][rs, cs] = p.astype(_BF16)
                part = fold8(p, jnp.sum)
                lsum = part if lsum is None else lsum + part
            l_ref[:, cs] = alpha[:, cs] * l_ref[:, cs] + jnp.sum(lsum, axis=0, keepdims=True)

    def stage_c(kj, par):
        off = pl.multiple_of(kj * tile, tile)
        for cs in lane_blocks:
            pv = _dot(vT_ref[0, :, pl.ds(off, tile)], p_refs[par][:, cs])
            acc_ref[:, cs] = al_refs[par][:, cs] * acc_ref[:, cs] + pv

    stage_a(0, 0)
    stage_a(1, 1)
    stage_b(0)

    def pair(t, carry):
        j = 2 * t + 1
        stage_a(j + 1, 0)
        stage_b(1)
        stage_c(j - 1, 0)
        stage_a(j + 2, 1)
        stage_b(0)
        stage_c(j, 1)
        return carry

    lax.fori_loop(0, (n_k - 2) // 2, pair, 0)
    stage_b(1)
    stage_c(n_k - 2, 0)
    stage_c(n_k - 1, 1)

    lam = lam_ref[0]
    inv_l = 1.0 / l_ref[...]
    o = acc_ref[...] * inv_l
    oT = o[:, :tile] - lam * o[:, tile:]
    ms = jnp.mean(oT * oT, axis=0, keepdims=True)
    oT = oT * lax.rsqrt(ms + _EPS) * gsub_ref[...] * out_scale
    o_ref[0] = oT.T.astype(_BF16)


def _lambda_kernel(q1_ref, k1_ref, q2_ref, k2_ref, out_ref, *, lambda_init):
    a = jnp.sum(q1_ref[...] * k1_ref[...], axis=-1, keepdims=True)
    b = jnp.sum(q2_ref[...] * k2_ref[...], axis=-1, keepdims=True)
    out_ref[...] = jnp.broadcast_to(jnp.exp(a) - jnp.exp(b) + lambda_init, out_ref.shape)


def _lambda(q1, k1, q2, k2, lambda_init):
    out = pl.pallas_call(
        functools.partial(_lambda_kernel, lambda_init=lambda_init),
        out_shape=jax.ShapeDtypeStruct((1, _LANES), _F32),
        name="lambda_scalar",
    )(q1, k1, q2, k2)
    return out[0, :1]


def _attention(lam, qT, k, vT, bias_tiles, gsub_col, *, tile, out_scale):
    B, d_att, S = qT.shape
    v_dim = 2 * _ATT_HEAD_DIM
    n_heads = d_att // v_dim
    grid = (n_heads, B, S // tile)
    return pl.pallas_call(
        functools.partial(_attention_kernel, tile=tile, out_scale=out_scale),
        grid=grid,
        in_specs=[pl.BlockSpec(memory_space=pltpu.SMEM),
                  pl.BlockSpec((1, v_dim, tile), lambda h, b, i: (b, h, i)),
                  pl.BlockSpec((1, S, v_dim), lambda h, b, i: (b, 0, h)),
                  pl.BlockSpec((1, v_dim, S), lambda h, b, i: (b, h, 0)),
                  pl.BlockSpec((5, 1, tile, 2 * tile), lambda h, b, i: (0, h, 0, 0)),
                  pl.BlockSpec((v_dim, 1), lambda h, b, i: (0, 0))],
        out_specs=pl.BlockSpec((1, tile, v_dim), lambda h, b, i: (b, i, h)),
        out_shape=jax.ShapeDtypeStruct((B, S, d_att), _BF16),
        scratch_shapes=[pltpu.VMEM((v_dim, 2 * tile), _BF16),
                        pltpu.VMEM((tile, 2 * tile), _F32), pltpu.VMEM((tile, 2 * tile), _F32),
                        pltpu.VMEM((tile, 2 * tile), _BF16), pltpu.VMEM((tile, 2 * tile), _BF16),
                        pltpu.VMEM((1, 2 * tile), _F32), pltpu.VMEM((1, 2 * tile), _F32),
                        pltpu.VMEM((1, 2 * tile), _F32), pltpu.VMEM((1, 2 * tile), _F32),
                        pltpu.VMEM((1, 2 * tile), _F32), pltpu.VMEM((1, 2 * tile), _F32),
                        pltpu.VMEM((v_dim, 2 * tile), _F32)],
        compiler_params=_cparams("parallel", "parallel", "parallel"),
        name="diff_attention",
    )(lam, qT, k, vT, bias_tiles, gsub_col)


def _outproj_kernel(oatt_ref, b_ref, u_ref, uprev_ref, unext_ref, x_ref, cw_ref, wout_ref, gffn_ref,
                    wr_ref, br_ref, x1_ref, h2_ref, comb_ref, *, d_att, n_experts):
    i = pl.program_id(1)
    n_i = pl.num_programs(1)
    u = u_ref[0].astype(_F32)
    tm = u.shape[0]
    prev_row = jnp.where(i > 0, uprev_ref[0, _BF16_SUBLANES - 1:_BF16_SUBLANES, :].astype(_F32), 0.0)
    next_row = jnp.where(i < n_i - 1, unext_ref[0, 0:1, :].astype(_F32), 0.0)
    rows = lax.broadcasted_iota(jnp.int32, (tm, 1), 0)
    u_prev = jnp.where(rows == 0, prev_row, pltpu.roll(u, 1, axis=0))
    u_next = jnp.where(rows == tm - 1, next_row, pltpu.roll(u, tm - 1, axis=0))
    conv = cw_ref[0:1, :] * u_prev + cw_ref[1:2, :] * u + cw_ref[2:3, :] * u_next
    o_conv = (b_ref[0].astype(_F32) * conv).astype(_BF16)
    mix = _dot(oatt_ref[0], wout_ref[:d_att, :]) + _dot(o_conv, wout_ref[d_att:, :])
    x1 = x_ref[0] + mix
    x1_ref[0] = x1
    ms = jnp.mean(x1 * x1, axis=-1, keepdims=True)
    t = x1 * lax.rsqrt(ms + _EPS) * gffn_ref[...]
    t_hi = t.astype(_BF16)
    h2_ref[0] = t_hi
    t_lo = (t - t_hi.astype(_F32)).astype(_BF16)
    a_hi = _dot(t_hi, wr_ref[...])
    a_lo = _dot(t_lo, wr_ref[...])
    logits = a_hi[:, :_LANES] + a_hi[:, _LANES:] + a_lo[:, :_LANES] + br_ref[...]
    n_groups = n_experts // _EXPERTS_PER_GROUP
    lane = lax.broadcasted_iota(jnp.int32, (1, _LANES), 1)
    lane_f = lane.astype(_F32)
    big = float(_LANES)
    gmask = (lane >= n_experts) & (lane < n_experts + n_groups)
    gl = jnp.where(gmask, logits, _NEG_BIG)
    gmax = jnp.max(gl, axis=-1, keepdims=True)
    gsum = jnp.sum(jnp.where(gmask, jnp.exp(gl - gmax), 0.0), axis=-1, keepdims=True)
    g_w = 1.0 / gsum
    g_idx = jnp.min(jnp.where(gmask & (gl == gmax), lane_f - n_experts, big), axis=-1, keepdims=True)
    lo = g_idx * _EXPERTS_PER_GROUP
    emask = (lane_f >= lo) & (lane_f < lo + _EXPERTS_PER_GROUP)
    el = jnp.where(emask, logits, _NEG_BIG)
    emax = jnp.max(el, axis=-1, keepdims=True)
    ep = jnp.where(emask, jnp.exp(el - emax), 0.0)
    p_exp = ep / jnp.sum(ep, axis=-1, keepdims=True)
    top1 = jnp.max(p_exp, axis=-1, keepdims=True)
    i1 = jnp.min(jnp.where(emask & (p_exp == top1), lane_f, big), axis=-1, keepdims=True)
    rest = jnp.where(emask & (lane_f != i1), p_exp, -1.0)
    top2 = jnp.max(rest, axis=-1, keepdims=True)
    i2 = jnp.min(jnp.where(rest == top2, lane_f, big), axis=-1, keepdims=True)
    denom = top1 + top2
    comb_ref[0] = jnp.where(lane_f == i1, g_w * (top1 / denom),
                            jnp.where(lane_f == i2, g_w * (top2 / denom), 0.0))


def _outproj(oatt, b, u, x, conv_w, w_out, gffn, wr, br, *, n_experts, tm):
    B, S, D = x.shape
    d_att = oatt.shape[-1]
    d_conv = b.shape[-1]
    hb = _BF16_SUBLANES
    per_tile = tm // hb
    n_halo = S // hb
    tok = lambda width: pl.BlockSpec((1, tm, width), lambda bb, i: (bb, i, 0))
    full = lambda shape: pl.BlockSpec(shape, lambda bb, i: (0,) * len(shape))
    return pl.pallas_call(
        functools.partial(_outproj_kernel, d_att=d_att, n_experts=n_experts),
        grid=(B, S // tm),
        in_specs=[tok(d_att), tok(d_conv), tok(d_conv),
                  pl.BlockSpec((1, hb, d_conv), lambda bb, i: (bb, jnp.maximum(i * per_tile - 1, 0), 0)),
                  pl.BlockSpec((1, hb, d_conv),
                               lambda bb, i: (bb, jnp.minimum((i + 1) * per_tile, n_halo - 1), 0)),
                  tok(D), full(conv_w.shape), full(w_out.shape), full((1, D)), full(wr.shape),
                  full((1, _LANES))],
        out_specs=[tok(D), tok(D), tok(_LANES)],
        out_shape=[jax.ShapeDtypeStruct((B, S, D), _F32),
                   jax.ShapeDtypeStruct((B, S, D), _BF16),
                   jax.ShapeDtypeStruct((B, S, _LANES), _F32)],
        compiler_params=_cparams("parallel", "parallel"),
        name="outproj_router",
    )(oatt, b, u, u, u, x, conv_w, w_out, gffn, wr, br)


def _moe_ple_kernel(h2_ref, comb_ref, x1_ref, p_ref, wgu_ref, wdn_ref, wproj_ref, wgate_ref, gple_ref,
                    y_ref, acc_ref, *, d_expert):
    g = pl.program_id(1)
    n_g = pl.num_programs(1)
    width = _EXPERTS_PER_GROUP * d_expert

    @pl.when(g == 0)
    def _():
        acc_ref[...] = jnp.zeros(acc_ref.shape, _F32)

    gu = _dot(h2_ref[...], wgu_ref[0])
    gate = gu[:, :width]
    act = gate * jax.nn.sigmoid(gate) * gu[:, width:]
    comb = comb_ref[...]
    lane = lax.broadcasted_iota(jnp.int32, (1, _LANES), 1)
    parts = []
    for e in range(_EXPERTS_PER_GROUP):
        w_e = jnp.sum(jnp.where(lane == g * _EXPERTS_PER_GROUP + e, comb, 0.0), axis=-1, keepdims=True)
        parts.append((act[:, e * d_expert:(e + 1) * d_expert] * w_e).astype(_BF16))
    acc_ref[...] += _dot(jnp.concatenate(parts, axis=1), wdn_ref[0])

    @pl.when(g == n_g - 1)
    def _():
        x2 = x1_ref[...] + acc_ref[...]
        e_raw = _dot(p_ref[...].astype(_BF16), wproj_ref[...])
        ms = jnp.mean(e_raw * e_raw, axis=-1, keepdims=True)
        emb = e_raw * lax.rsqrt(ms + _EPS) * gple_ref[...]
        gate_p = jax.nn.sigmoid(_dot(x2.astype(_BF16), wgate_ref[...]))
        y_ref[...] = x2 + gate_p * emb


def _moe_ple(h2, comb, x1, p, wgu, wdn, wproj, wgate, gple, *, d_expert, tm):
    T, D = x1.shape
    n_groups = wgu.shape[0]
    tok = lambda width: pl.BlockSpec((tm, width), lambda i, g: (i, 0))
    full = lambda shape: pl.BlockSpec(shape, lambda i, g: (0,) * len(shape))
    grp = lambda shape: pl.BlockSpec((1,) + shape, lambda i, g: (g, 0, 0))
    return pl.pallas_call(
        functools.partial(_moe_ple_kernel, d_expert=d_expert),
        grid=(T // tm, n_groups),
        in_specs=[tok(D), tok(_LANES), tok(D), tok(p.shape[-1]),
                  grp(wgu.shape[1:]), grp(wdn.shape[1:]), full(wproj.shape), full(wgate.shape),
                  full((1, D))],
        out_specs=tok(D),
        out_shape=jax.ShapeDtypeStruct((T, D), _F32),
        scratch_shapes=[pltpu.VMEM((tm, D), _F32)],
        compiler_params=_cparams("parallel", "arbitrary"),
        name="moe_ple",
    )(h2, comb, x1, p, wgu, wdn, wproj, wgate, gple)


def _prepare_layer(i, norm_mix, w_in, q_norm, k_norm, lambda_q1, lambda_k1, lambda_q2, lambda_k2,
                   attn_sub_norm, conv_w, w_out, norm_ffn, w_group, b_group, w_erouter, b_erouter,
                   w_gate_up, w_down, w_ple_proj, w_ple_gate, ple_norm):
    D = w_in.shape[1]
    d_mix = w_out.shape[1]
    d_att = d_mix // 2
    d_conv = d_mix - d_att
    n_maps = d_att // _ATT_HEAD_DIM
    n_experts = w_gate_up.shape[1]
    n_groups = w_group.shape[-1]
    d_expert = w_down.shape[2]
    assert n_experts == n_groups * _EXPERTS_PER_GROUP and n_experts + n_groups <= _LANES
    lambda_init = 0.8 - 0.6 * math.exp(-0.3 * i)
    head_of = jnp.arange(d_att) // _ATT_HEAD_DIM
    bd = jnp.where(head_of[:, None] == head_of[None, :], 1.0 / _ATT_HEAD_DIM, 0.0).astype(_BF16)
    wr = jnp.zeros((D, _LANES), _F32)
    wr = wr.at[:, :n_experts].set(w_erouter[i]).at[:, n_experts:n_experts + n_groups].set(w_group[i])
    wr_hi = wr.astype(_BF16)
    wr_lo = (wr - wr_hi.astype(_F32)).astype(_BF16)
    br = jnp.zeros((1, _LANES), _F32)
    br = br.at[0, :n_experts].set(b_erouter[i]).at[0, n_experts:n_experts + n_groups].set(b_group[i])
    wg = w_gate_up[i].reshape(n_groups, _EXPERTS_PER_GROUP, D, 2, d_expert)
    wgu = jnp.transpose(wg, (0, 2, 3, 1, 4)).reshape(n_groups, D, 2 * _EXPERTS_PER_GROUP * d_expert)
    wdn = w_down[i].reshape(n_groups, _EXPERTS_PER_GROUP * d_expert, D)
    return dict(
        d_att=d_att, d_conv=d_conv, n_experts=n_experts, d_expert=d_expert, lambda_init=lambda_init,
        gmix=norm_mix[i][None, :], w_in=w_in[i].astype(_BF16),
        gq_col=q_norm[i][:, None], gk_row=jnp.tile(k_norm[i], n_maps)[None, :], bd=bd,
        lam_vecs=tuple(v[i][None, :] for v in (lambda_q1, lambda_k1, lambda_q2, lambda_k2)),
        gsub_col=attn_sub_norm[i][:, None], conv_w=conv_w[i], w_out=w_out[i].astype(_BF16),
        gffn=norm_ffn[i][None, :], wr=jnp.concatenate([wr_hi, wr_lo], axis=1), br=br,
        wgu=wgu.astype(_BF16), wdn=wdn.astype(_BF16), wproj=w_ple_proj[i].astype(_BF16),
        wgate=w_ple_gate[i].astype(_BF16), gple=ple_norm[i][None, :])


def _layer(x, p_i, L, bias_tiles, lam):
    B, S, D = x.shape
    tm = _TOKEN_TILE
    tile = _ATTN_TILE
    assert S % tm == 0 and S % tile == 0 and tm % _BF16_SUBLANES == 0
    qT, k, vT, b, u = _inproj(x, L["gmix"], L["w_in"], L["gq_col"], L["gk_row"], L["bd"],
                              d_att=L["d_att"], d_conv=L["d_conv"], tm=tm)
    oatt = _attention(lam, qT, k, vT, bias_tiles, L["gsub_col"], tile=tile,
                      out_scale=1.0 - L["lambda_init"])
    x1, h2, comb = _outproj(oatt, b, u, x, L["conv_w"], L["w_out"], L["gffn"], L["wr"], L["br"],
                            n_experts=L["n_experts"], tm=tm)
    y = _moe_ple(h2.reshape(B * S, D), comb.reshape(B * S, _LANES), x1.reshape(B * S, D),
                 p_i.reshape(B * S, -1), L["wgu"], L["wdn"], L["wproj"], L["wgate"], L["gple"],
                 d_expert=L["d_expert"], tm=tm)
    return y.reshape(B, S, D)


def kernel(x_prompt, x_sample, p_prompt, p_sample, norm_mix, w_in, q_norm, k_norm, lambda_q1, lambda_k1, lambda_q2, lambda_k2, attn_sub_norm, conv_w, w_out, rel_bias, norm_ffn, w_group, b_group, w_erouter, b_erouter, w_gate_up, w_down, w_ple_proj, w_ple_gate, ple_norm):
    depth = w_in.shape[0]
    bias_tiles = _bias_tiles(rel_bias, _ATTN_TILE)
    layers = []
    for i in range(depth):
        L = _prepare_layer(i, norm_mix, w_in, q_norm, k_norm, lambda_q1, lambda_k1, lambda_q2,
                           lambda_k2, attn_sub_norm, conv_w, w_out, norm_ffn, w_group, b_group,
                           w_erouter, b_erouter, w_gate_up, w_down, w_ple_proj, w_ple_gate, ple_norm)
        layers.append((L, _lambda(*L["lam_vecs"], L["lambda_init"])))

    def encode(x, p):
        for i, (L, lam) in enumerate(layers):
            x = _layer(x, p[i], L, bias_tiles, lam)
        return x

    return (encode(x_prompt, p_prompt), encode(x_sample, p_sample))
```

```python
import functools
import math

import jax
import jax.numpy as jnp
from jax import lax
from jax.experimental import pallas as pl
from jax.experimental.pallas import tpu as pltpu

_F32 = jnp.float32
_BF16 = jnp.bfloat16

_EPS = 1e-6
_ATT_HEAD_DIM = 64
_MAX_DISTANCE = 128
_EXPERTS_PER_GROUP = 4
_LANES = 128
_F32_SUBLANES = 8
_BF16_SUBLANES = 16
_ATTN_TILE = 512
_ATTN_ROWS = 128
_ATTN_LANES = 256
_TOKEN_TILE = 512
_VMEM_LIMIT_BYTES = 56 * 1024 * 1024
_SAFE_EXPONENT_SPAN = 100.0
_NEG_BIG = -1e30
_LOG2E = math.log2(math.e)


def _cparams(*sem):
    return pltpu.CompilerParams(dimension_semantics=sem, vmem_limit_bytes=_VMEM_LIMIT_BYTES)


def _dot(a, b):
    return jnp.dot(a, b, preferred_element_type=_F32)


def _bias_tile_kernel(tbl_ref, shift_ref, out_ref, *, tile, num_buckets):
    di = pl.program_id(0)
    m = pl.program_id(1)
    delta = (di - 2) * tile
    kk = lax.broadcasted_iota(jnp.int32, (tile, tile), 0)
    qq = lax.broadcasted_iota(jnp.int32, (tile, tile), 1)
    rel = kk - qq + delta
    half = num_buckets // 2
    max_exact = half // 2
    ret = jnp.where(rel > 0, half, 0)
    n = jnp.abs(rel)
    nf = jnp.maximum(n, 1).astype(_F32)
    large = max_exact + (jnp.log(nf / max_exact) / math.log(_MAX_DISTANCE / max_exact)
                         * (half - max_exact)).astype(jnp.int32)
    large = jnp.minimum(large, half - 1)
    bucket = ret + jnp.where(n < max_exact, n, large)
    acc = jnp.zeros((tile, tile), _F32)
    for b in range(num_buckets):
        acc = jnp.where(bucket == b, tbl_ref[b, m], acc)
    out_ref[0, 0] = acc * _LOG2E - shift_ref[m]


def _bias_tiles(rel_bias, shift, tile):
    num_buckets, n_maps = rel_bias.shape
    assert tile >= _MAX_DISTANCE
    return pl.pallas_call(
        functools.partial(_bias_tile_kernel, tile=tile, num_buckets=num_buckets),
        grid=(5, n_maps),
        in_specs=[pl.BlockSpec(memory_space=pltpu.SMEM), pl.BlockSpec(memory_space=pltpu.SMEM)],
        out_specs=pl.BlockSpec((1, 1, tile, tile), lambda di, m: (di, m // 2, 0, m % 2)),
        out_shape=jax.ShapeDtypeStruct((5, n_maps // 2, tile, 2 * tile), _F32),
        compiler_params=_cparams("arbitrary", "arbitrary"),
        name="bias_tiles",
    )(rel_bias.astype(_F32), shift.astype(_F32))


def _inproj_kernel(x_ref, gmix_ref, w_ref, gq_ref, gk_ref, bd_ref,
                   qT_ref, k_ref, vT_ref, b_ref, u_ref, *, d_att, d_conv, scale):
    x = x_ref[0]
    ms = jnp.mean(x * x, axis=-1, keepdims=True)
    h = (x * lax.rsqrt(ms + _EPS) * gmix_ref[...]).astype(_BF16)

    def proj(lo, n):
        return _dot(h, w_ref[:, lo:lo + n])

    tm = x.shape[0]
    n_maps = d_att // _ATT_HEAD_DIM
    zqT = proj(0, d_att).T.reshape(n_maps, _ATT_HEAD_DIM, tm)
    qms = jnp.mean(zqT * zqT, axis=1, keepdims=True)
    qn = zqT * lax.rsqrt(qms + _EPS) * gq_ref[...][None]
    qT_ref[0] = (qn * scale).reshape(d_att, tm).astype(_BF16)
    zk = proj(d_att, d_att)
    kms = _dot((zk * zk).astype(_BF16), bd_ref[...])
    k_ref[0] = (zk * lax.rsqrt(kms + _EPS) * gk_ref[...]).astype(_BF16)
    vT_ref[0] = proj(2 * d_att, d_att).T.astype(_BF16)
    b_ref[0] = proj(3 * d_att, d_conv).astype(_BF16)
    c = proj(3 * d_att + d_conv, d_conv)
    xc = proj(3 * d_att + 2 * d_conv, d_conv)
    u_ref[0] = (c * xc).astype(_BF16)


def _inproj(x, gmix, w_in, gq_col, gk_row, bd, *, d_att, d_conv, tm):
    B, S, D = x.shape
    d_in = w_in.shape[1]
    grid = (B, S // tm)
    tok = lambda width: pl.BlockSpec((1, tm, width), lambda b, i: (b, i, 0))
    tr = lambda rows: pl.BlockSpec((1, rows, tm), lambda b, i: (b, 0, i))
    full = lambda shape: pl.BlockSpec(shape, lambda b, i: (0,) * len(shape))
    return pl.pallas_call(
        functools.partial(_inproj_kernel, d_att=d_att, d_conv=d_conv,
                          scale=_LOG2E / math.sqrt(_ATT_HEAD_DIM)),
        grid=grid,
        in_specs=[tok(D), full((1, D)), full((D, d_in)), full((_ATT_HEAD_DIM, 1)), full((1, d_att)),
                  full((d_att, d_att))],
        out_specs=[tr(d_att), tok(d_att), tr(d_att), tok(d_conv), tok(d_conv)],
        out_shape=[jax.ShapeDtypeStruct((B, d_att, S), _BF16),
                   jax.ShapeDtypeStruct((B, S, d_att), _BF16),
                   jax.ShapeDtypeStruct((B, d_att, S), _BF16),
                   jax.ShapeDtypeStruct((B, S, d_conv), _BF16),
                   jax.ShapeDtypeStruct((B, S, d_conv), _BF16)],
        compiler_params=_cparams("parallel", "parallel"),
        name="inproj",
    )(x, gmix, w_in, gq_col, gk_row, bd)


def _attention_kernel(lam_ref, qT_ref, k_ref, vT_ref, bias_ref, gsub_ref, o_ref,
                      rhs_ref, s0_ref, s1_ref, p0_ref, p1_ref, mx0_ref, mx1_ref, al0_ref, al1_ref,
                      m_ref, l_ref, acc_ref, *, tile, out_scale):
    qi = pl.program_id(2)
    n_k = k_ref.shape[1] // tile
    hd = _ATT_HEAD_DIM
    s_refs, p_refs = (s0_ref, s1_ref), (p0_ref, p1_ref)
    mx_refs, al_refs = (mx0_ref, mx1_ref), (al0_ref, al1_ref)
    qT = qT_ref[0]
    row = lax.broadcasted_iota(jnp.int32, qT.shape, 0)
    zero = jnp.zeros_like(qT)
    rhs_ref[:, :tile] = jnp.where(row < hd, qT, zero)
    rhs_ref[:, tile:] = jnp.where(row >= hd, qT, zero)
    m_ref[...] = jnp.full(m_ref.shape, _NEG_BIG, _F32)
    l_ref[...] = jnp.zeros(l_ref.shape, _F32)
    acc_ref[...] = jnp.zeros(acc_ref.shape, _F32)

    row_blocks = [slice(r, r + _ATTN_ROWS) for r in range(0, tile, _ATTN_ROWS)]
    lane_blocks = [slice(c, c + _ATTN_LANES) for c in range(0, 2 * tile, _ATTN_LANES)]

    def fold8(x, op):
        return op(x.reshape(x.shape[0] // 8, 8, x.shape[1]), axis=0)

    def stage_a(kj, par):
        off = pl.multiple_of(kj * tile, tile)
        bidx = jnp.clip(kj - qi, -2, 2) + 2
        for cs in lane_blocks:
            rhs_c = rhs_ref[:, cs]
            mx = None
            for rs in row_blocks:
                k_blk = k_ref[0, pl.ds(off + rs.start, _ATTN_ROWS), :]
                s = _dot(k_blk, rhs_c) + bias_ref[bidx, 0, rs, cs]
                s_refs[par][rs, cs] = s
                part = fold8(s, jnp.max)
                mx = part if mx is None else jnp.maximum(mx, part)
            mx_refs[par][:, cs] = jnp.max(mx, axis=0, keepdims=True)

    def stage_b(par):
        m_old = m_ref[...]
        m_new = jnp.maximum(m_old, mx_refs[par][...])
        alpha = jnp.exp2(m_old - m_new)
        al_refs[par][...] = alpha
        m_ref[...] = m_new
        for cs in lane_blocks:
            m_c = m_new[:, cs]
            lsum = None
            for rs in row_blocks:
                p = jnp.exp2(s_refs[par][rs, cs] - m_c)
                p_refs[par][rs, cs] = p.astype(_BF16)
                part = fold8(p, jnp.sum)
                lsum = part if lsum is None else lsum + part
            l_ref[:, cs] = alpha[:, cs] * l_ref[:, cs] + jnp.sum(lsum, axis=0, keepdims=True)

    def stage_c(kj, par):
        off = pl.multiple_of(kj * tile, tile)
        for cs in lane_blocks:
            pv = _dot(vT_ref[0, :, pl.ds(off, tile)], p_refs[par][:, cs])
            acc_ref[:, cs] = al_refs[par][:, cs] * acc_ref[:, cs] + pv

    stage_a(0, 0)
    stage_a(1, 1)
    stage_b(0)

    def pair(t, carry):
        j = 2 * t + 1
        stage_a(j + 1, 0)
        stage_b(1)
        stage_c(j - 1, 0)
        stage_a(j + 2, 1)
        stage_b(0)
        stage_c(j, 1)
        return carry

    lax.fori_loop(0, (n_k - 2) // 2, pair, 0)
    stage_b(1)
    stage_c(n_k - 2, 0)
    stage_c(n_k - 1, 1)

    _attention_finish(lam_ref, l_ref, acc_ref, gsub_ref, o_ref, tile=tile, out_scale=out_scale)


def _attention_finish(lam_ref, l_ref, acc_ref, gsub_ref, o_ref, *, tile, out_scale):
    o = acc_ref[...] * (1.0 / l_ref[...])
    oT = o[:, :tile] - lam_ref[0] * o[:, tile:]
    ms = jnp.mean(oT * oT, axis=0, keepdims=True)
    oT = oT * lax.rsqrt(ms + _EPS) * gsub_ref[...] * out_scale
    o_ref[0] = oT.T.astype(_BF16)


def _attention_bounded_kernel(lam_ref, qT_ref, k_ref, vT_ref, bias_ref, gsub_ref, o_ref,
                              rhs_ref, p0_ref, p1_ref, l_ref, acc_ref, *, tile, out_scale):
    qi = pl.program_id(2)
    n_k = k_ref.shape[1] // tile
    hd = _ATT_HEAD_DIM
    p_refs = (p0_ref, p1_ref)
    qT = qT_ref[0]
    row = lax.broadcasted_iota(jnp.int32, qT.shape, 0)
    zero = jnp.zeros_like(qT)
    rhs_ref[:, :tile] = jnp.where(row < hd, qT, zero)
    rhs_ref[:, tile:] = jnp.where(row >= hd, qT, zero)
    l_ref[...] = jnp.zeros(l_ref.shape, _F32)
    acc_ref[...] = jnp.zeros(acc_ref.shape, _F32)
    row_blocks = [slice(r, r + _ATTN_ROWS) for r in range(0, tile, _ATTN_ROWS)]
    lane_blocks = [slice(c, c + _ATTN_LANES) for c in range(0, 2 * tile, _ATTN_LANES)]

    def stage_ab(kj, par):
        off = pl.multiple_of(kj * tile, tile)
        bidx = jnp.clip(kj - qi, -2, 2) + 2
        for cs in lane_blocks:
            rhs_c = rhs_ref[:, cs]
            lsum = None
            for rs in row_blocks:
                k_blk = k_ref[0, pl.ds(off + rs.start, _ATTN_ROWS), :]
                p = jnp.exp2(_dot(k_blk, rhs_c) + bias_ref[bidx, 0, rs, cs])
                p_refs[par][rs, cs] = p.astype(_BF16)
                part = jnp.sum(p.reshape(_ATTN_ROWS // 8, 8, _ATTN_LANES), axis=0)
                lsum = part if lsum is None else lsum + part
            l_ref[:, cs] += jnp.sum(lsum, axis=0, keepdims=True)

    def stage_c(kj, par):
        off = pl.multiple_of(kj * tile, tile)
        for cs in lane_blocks:
            acc_ref[:, cs] += _dot(vT_ref[0, :, pl.ds(off, tile)], p_refs[par][:, cs])

    stage_ab(0, 0)

    def pair(t, carry):
        j = 2 * t
        stage_ab(j + 1, 1)
        stage_c(j, 0)
        stage_ab(j + 2, 0)
        stage_c(j + 1, 1)
        return carry

    lax.fori_loop(0, n_k // 2 - 1, pair, 0)
    stage_ab(n_k - 1, 1)
    stage_c(n_k - 2, 0)
    stage_c(n_k - 1, 1)
    _attention_finish(lam_ref, l_ref, acc_ref, gsub_ref, o_ref, tile=tile, out_scale=out_scale)


def _lambda_kernel(q1_ref, k1_ref, q2_ref, k2_ref, out_ref, *, lambda_init):
    a = jnp.sum(q1_ref[...] * k1_ref[...], axis=-1, keepdims=True)
    b = jnp.sum(q2_ref[...] * k2_ref[...], axis=-1, keepdims=True)
    out_ref[...] = jnp.broadcast_to(jnp.exp(a) - jnp.exp(b) + lambda_init, out_ref.shape)


def _lambda(q1, k1, q2, k2, lambda_init):
    out = pl.pallas_call(
        functools.partial(_lambda_kernel, lambda_init=lambda_init),
        out_shape=jax.ShapeDtypeStruct((1, _LANES), _F32),
        name="lambda_scalar",
    )(q1, k1, q2, k2)
    return out[0, :1]


def _attention(bound_ok, lam, qT, k, vT, bias_tiles, gsub_col, *, tile, out_scale):
    B, d_att, S = qT.shape
    v_dim = 2 * _ATT_HEAD_DIM
    n_heads = d_att // v_dim
    assert S % (2 * tile) == 0
    row = lambda: pltpu.VMEM((1, 2 * tile), _F32)
    s_buf = lambda: pltpu.VMEM((tile, 2 * tile), _F32)
    p_buf = lambda: pltpu.VMEM((tile, 2 * tile), _BF16)
    rhs = pltpu.VMEM((v_dim, 2 * tile), _BF16)
    acc = pltpu.VMEM((v_dim, 2 * tile), _F32)

    def call(body, scratch, name):
        return pl.pallas_call(
            functools.partial(body, tile=tile, out_scale=out_scale),
            grid=(n_heads, B, S // tile),
            in_specs=[pl.BlockSpec(memory_space=pltpu.SMEM),
                      pl.BlockSpec((1, v_dim, tile), lambda h, b, i: (b, h, i)),
                      pl.BlockSpec((1, S, v_dim), lambda h, b, i: (b, 0, h)),
                      pl.BlockSpec((1, v_dim, S), lambda h, b, i: (b, h, 0)),
                      pl.BlockSpec((5, 1, tile, 2 * tile), lambda h, b, i: (0, h, 0, 0)),
                      pl.BlockSpec((v_dim, 1), lambda h, b, i: (0, 0))],
            out_specs=pl.BlockSpec((1, tile, v_dim), lambda h, b, i: (b, i, h)),
            out_shape=jax.ShapeDtypeStruct((B, S, d_att), _BF16),
            scratch_shapes=scratch,
            compiler_params=_cparams("parallel", "parallel", "parallel"),
            name=name,
        )(lam, qT, k, vT, bias_tiles, gsub_col)

    return lax.cond(
        bound_ok,
        lambda: call(_attention_bounded_kernel, [rhs, p_buf(), p_buf(), row(), acc],
                     "diff_attention_bounded"),
        lambda: call(_attention_kernel, [rhs, s_buf(), s_buf(), p_buf(), p_buf(), row(), row(), row(),
                                         row(), row(), row(), acc], "diff_attention"))


def _outproj_kernel(oatt_ref, b_ref, u_ref, uprev_ref, unext_ref, x_ref, cw_ref, wout_ref, gffn_ref,
                    wr_ref, br_ref, x1w_ref, route_ref, *, d_att, n_experts):
    i = pl.program_id(1)
    n_i = pl.num_programs(1)
    u = u_ref[0].astype(_F32)
    tm = u.shape[0]
    prev_row = jnp.where(i > 0, uprev_ref[0, _BF16_SUBLANES - 1:_BF16_SUBLANES, :].astype(_F32), 0.0)
    next_row = jnp.where(i < n_i - 1, unext_ref[0, 0:1, :].astype(_F32), 0.0)
    rows = lax.broadcasted_iota(jnp.int32, (tm, 1), 0)
    u_prev = jnp.where(rows == 0, prev_row, pltpu.roll(u, 1, axis=0))
    u_next = jnp.where(rows == tm - 1, next_row, pltpu.roll(u, tm - 1, axis=0))
    conv = cw_ref[0:1, :] * u_prev + cw_ref[1:2, :] * u + cw_ref[2:3, :] * u_next
    o_conv = (b_ref[0].astype(_F32) * conv).astype(_BF16)
    mix = _dot(oatt_ref[0], wout_ref[:d_att, :]) + _dot(o_conv, wout_ref[d_att:, :])
    x1 = x_ref[0] + mix
    d_model = x1.shape[1]
    x1w_ref[0, :, :d_model] = x1
    ms = jnp.mean(x1 * x1, axis=-1, keepdims=True)
    t = x1 * lax.rsqrt(ms + _EPS) * gffn_ref[...]
    t_hi = t.astype(_BF16)
    t_lo = (t - t_hi.astype(_F32)).astype(_BF16)
    a_hi = _dot(t_hi, wr_ref[...])
    a_lo = _dot(t_lo, wr_ref[...])
    logits = a_hi[:, :_LANES] + a_hi[:, _LANES:] + a_lo[:, :_LANES] + br_ref[...]
    n_groups = n_experts // _EXPERTS_PER_GROUP
    lane = lax.broadcasted_iota(jnp.int32, (1, _LANES), 1)
    lane_f = lane.astype(_F32)
    big = float(_LANES)
    gmask = (lane >= n_experts) & (lane < n_experts + n_groups)
    gl = jnp.where(gmask, logits, _NEG_BIG)
    gmax = jnp.max(gl, axis=-1, keepdims=True)
    gsum = jnp.sum(jnp.where(gmask, jnp.exp(gl - gmax), 0.0), axis=-1, keepdims=True)
    g_w = 1.0 / gsum
    g_idx = jnp.min(jnp.where(gmask & (gl == gmax), lane_f - n_experts, big), axis=-1, keepdims=True)
    lo = g_idx * _EXPERTS_PER_GROUP
    emask = (lane_f >= lo) & (lane_f < lo + _EXPERTS_PER_GROUP)
    el = jnp.where(emask, logits, _NEG_BIG)
    emax = jnp.max(el, axis=-1, keepdims=True)
    ep = jnp.where(emask, jnp.exp(el - emax), 0.0)
    p_exp = ep / jnp.sum(ep, axis=-1, keepdims=True)
    top1 = jnp.max(p_exp, axis=-1, keepdims=True)
    i1 = jnp.min(jnp.where(emask & (p_exp == top1), lane_f, big), axis=-1, keepdims=True)
    rest = jnp.where(emask & (lane_f != i1), p_exp, -1.0)
    top2 = jnp.max(rest, axis=-1, keepdims=True)
    i2 = jnp.min(jnp.where(rest == top2, lane_f, big), axis=-1, keepdims=True)
    denom = top1 + top2
    x1w_ref[0, :, d_model:] = jnp.where(lane_f == i1 - lo, g_w * (top1 / denom),
                                        jnp.where(lane_f == i2 - lo, g_w * (top2 / denom), 0.0))
    route_ref[0] = jnp.broadcast_to(g_idx, (tm, _LANES)).T[:_F32_SUBLANES, :]


def _outproj(oatt, b, u, x, conv_w, w_out, gffn, wr, br, *, n_experts, tm):
    B, S, D = x.shape
    d_att = oatt.shape[-1]
    d_conv = b.shape[-1]
    hb = _BF16_SUBLANES
    per_tile = tm // hb
    n_halo = S // hb
    tok = lambda width: pl.BlockSpec((1, tm, width), lambda bb, i: (bb, i, 0))
    full = lambda shape: pl.BlockSpec(shape, lambda bb, i: (0,) * len(shape))
    return pl.pallas_call(
        functools.partial(_outproj_kernel, d_att=d_att, n_experts=n_experts),
        grid=(B, S // tm),
        in_specs=[tok(d_att), tok(d_conv), tok(d_conv),
                  pl.BlockSpec((1, hb, d_conv), lambda bb, i: (bb, jnp.maximum(i * per_tile - 1, 0), 0)),
                  pl.BlockSpec((1, hb, d_conv),
                               lambda bb, i: (bb, jnp.minimum((i + 1) * per_tile, n_halo - 1), 0)),
                  tok(D), full(conv_w.shape), full(w_out.shape), full((1, D)), full(wr.shape),
                  full((1, _LANES))],
        out_specs=[tok(D + _LANES), pl.BlockSpec((1, _F32_SUBLANES, tm), lambda bb, i: (bb, 0, i))],
        out_shape=[jax.ShapeDtypeStruct((B, S, D + _LANES), _F32),
                   jax.ShapeDtypeStruct((B, _F32_SUBLANES, S), _F32)],
        compiler_params=_cparams("parallel", "parallel"),
        name="outproj_router",
    )(oatt, b, u, u, u, x, conv_w, w_out, gffn, wr, br)


def _row_gather_start(idx_ref, idx_slot, src_hbm, buf, slot, sem, n_rows):
    for r in range(n_rows):
        pltpu.make_async_copy(src_hbm.at[pl.ds(idx_ref[idx_slot, r], 1), :],
                              buf.at[slot, pl.ds(r, 1), :], sem.at[slot]).start()


def _row_gather_wait(src_hbm, buf, slot, sem, n_rows):
    pltpu.make_async_copy(src_hbm.at[pl.ds(0, n_rows), :], buf.at[slot], sem.at[slot]).wait()


def _idx_copy(idx_ref, idx_smem, idx_sem, s):
    return pltpu.make_async_copy(idx_ref.at[0, 0], idx_smem.at[s], idx_sem.at[s])


def _gather_begin(idx0_ref, idx1_ref, idx2_ref, idx_smem, idx_sem, src_hbm, buf, sem, n_rows):
    i = pl.program_id(0)
    slot = i % 2
    nxt = 1 - slot

    @pl.when(i == 0)
    def _():
        for ref, s in ((idx0_ref, 0), (idx1_ref, 1)):
            c = _idx_copy(ref, idx_smem, idx_sem, s)
            c.start()
            c.wait()
        _row_gather_start(idx_smem, 0, src_hbm, buf, 0, sem, n_rows)

    _row_gather_start(idx_smem, nxt, src_hbm, buf, nxt, sem, n_rows)
    _idx_copy(idx2_ref, idx_smem, idx_sem, slot).start()
    _row_gather_wait(src_hbm, buf, slot, sem, n_rows)
    return slot


def _gather_end(idx2_ref, idx_smem, idx_sem, src_hbm, buf, sem, n_rows):
    i = pl.program_id(0)
    slot = i % 2
    _idx_copy(idx2_ref, idx_smem, idx_sem, slot).wait()

    @pl.when(i == pl.num_programs(0) - 1)
    def _():
        _row_gather_wait(src_hbm, buf, 1 - slot, sem, n_rows)


def _gather_specs(n_steps, tm, index_map_args):
    last = n_steps - 1
    blk = (1, 1, tm)
    if index_map_args == 1:
        return [pl.BlockSpec(blk, lambda i: (0, 0, 0)),
                pl.BlockSpec(blk, lambda i: (min(1, last), 0, 0)),
                pl.BlockSpec(blk, lambda i: (jnp.minimum(i + 2, last), 0, 0))]
    return [pl.BlockSpec(blk, lambda i, *_: (0, 0, 0)),
            pl.BlockSpec(blk, lambda i, *_: (min(1, last), 0, 0)),
            pl.BlockSpec(blk, lambda i, *_: (jnp.minimum(i + 2, last), 0, 0))]


def _moe_sorted_kernel(tile_group_ref, idx0_ref, idx1_ref, idx2_ref, x1w_hbm, gffn_ref, wgu_ref, wdn_ref,
                       out_ref, idx_smem, idx_sem, buf, sem, *, d_expert):
    del tile_group_ref
    tm, d_model = out_ref.shape
    slot = _gather_begin(idx0_ref, idx1_ref, idx2_ref, idx_smem, idx_sem, x1w_hbm, buf, sem, tm)
    x1 = buf[slot, :, :d_model]
    gates = buf[slot, :, d_model:]
    ms = jnp.mean(x1 * x1, axis=-1, keepdims=True)
    t = (x1 * lax.rsqrt(ms + _EPS) * gffn_ref[...]).astype(_BF16)
    width = _EXPERTS_PER_GROUP * d_expert
    gu = _dot(t, wgu_ref[0])
    g_lin = gu[:, :width]
    act = g_lin * jax.nn.sigmoid(g_lin) * gu[:, width:]
    parts = [(act[:, e * d_expert:(e + 1) * d_expert] * gates[:, e:e + 1]).astype(_BF16)
             for e in range(_EXPERTS_PER_GROUP)]
    out_ref[...] = _dot(jnp.concatenate(parts, axis=1), wdn_ref[0])
    _gather_end(idx2_ref, idx_smem, idx_sem, x1w_hbm, buf, sem, tm)


def _moe_sorted(tile_group, src_idx, x1w, gffn, wgu, wdn, *, d_expert, tm):
    n_tiles = src_idx.shape[0]
    T, width = x1w.shape
    D = width - _LANES
    full = lambda shape: pl.BlockSpec(shape, lambda i, tg: (0,) * len(shape))
    grp = lambda shape: pl.BlockSpec((1,) + shape, lambda i, tg: (tg[i], 0, 0))
    return pl.pallas_call(
        functools.partial(_moe_sorted_kernel, d_expert=d_expert),
        grid_spec=pltpu.PrefetchScalarGridSpec(
            num_scalar_prefetch=1,
            grid=(n_tiles,),
            in_specs=_gather_specs(n_tiles, tm, 2) + [
                pl.BlockSpec(memory_space=pl.ANY), full((1, D)), grp(wgu.shape[1:]), grp(wdn.shape[1:])],
            out_specs=pl.BlockSpec((tm, D), lambda i, tg: (i, 0)),
            scratch_shapes=[pltpu.SMEM((2, tm), jnp.int32), pltpu.SemaphoreType.DMA((2,)),
                            pltpu.VMEM((2, tm, width), _F32), pltpu.SemaphoreType.DMA((2,))]),
        out_shape=jax.ShapeDtypeStruct((n_tiles * tm, D), _F32),
        compiler_params=_cparams("arbitrary"),
        name="moe_sorted",
    )(tile_group, src_idx, src_idx, src_idx, x1w, gffn, wgu, wdn)


def _ple_kernel(idx0_ref, idx1_ref, idx2_ref, moe_hbm, x1_ref, p_ref, wproj_ref, wgate_ref, gple_ref,
                y_ref, idx_smem, idx_sem, buf, sem):
    tm = y_ref.shape[0]
    slot = _gather_begin(idx0_ref, idx1_ref, idx2_ref, idx_smem, idx_sem, moe_hbm, buf, sem, tm)
    x2 = x1_ref[...] + buf[slot]
    e_raw = _dot(p_ref[...].astype(_BF16), wproj_ref[...])
    ms = jnp.mean(e_raw * e_raw, axis=-1, keepdims=True)
    emb = e_raw * lax.rsqrt(ms + _EPS) * gple_ref[...]
    gate_p = jax.nn.sigmoid(_dot(x2.astype(_BF16), wgate_ref[...]))
    y_ref[...] = x2 + gate_p * emb
    _gather_end(idx2_ref, idx_smem, idx_sem, moe_hbm, buf, sem, tm)


def _ple(pos, moe_sorted, x1w, p, wproj, wgate, gple, *, tm):
    T = x1w.shape[0]
    D = moe_sorted.shape[1]
    n_tiles = T // tm
    tok = lambda width: pl.BlockSpec((tm, width), lambda i: (i, 0))
    full = lambda shape: pl.BlockSpec(shape, lambda i: (0,) * len(shape))
    return pl.pallas_call(
        _ple_kernel,
        grid=(n_tiles,),
        in_specs=_gather_specs(n_tiles, tm, 1) + [
            pl.BlockSpec(memory_space=pl.ANY), tok(D), tok(p.shape[-1]), full(wproj.shape),
            full(wgate.shape), full((1, D))],
        out_specs=tok(D),
        out_shape=jax.ShapeDtypeStruct((T, D), _F32),
        scratch_shapes=[pltpu.SMEM((2, tm), jnp.int32), pltpu.SemaphoreType.DMA((2,)),
                        pltpu.VMEM((2, tm, D), _F32), pltpu.SemaphoreType.DMA((2,))],
        compiler_params=_cparams("arbitrary"),
        name="ple_gate",
    )(pos, pos, pos, moe_sorted, x1w, p, wproj, wgate, gple)


def _dispatch_plan(group_of_token, n_groups, tm):
    T = group_of_token.shape[0]
    n_tiles = T // tm + n_groups - 1
    onehot = (group_of_token[:, None] == jnp.arange(n_groups, dtype=jnp.int32)[None, :]).astype(jnp.int32)
    rank = jnp.take_along_axis(jnp.cumsum(onehot, axis=0), group_of_token[:, None], axis=1)[:, 0] - 1
    tiles_per_group = (jnp.sum(onehot, axis=0) + tm - 1) // tm
    tile_end = jnp.cumsum(tiles_per_group)
    tile_start = tile_end - tiles_per_group
    pos = tile_start[group_of_token] * tm + rank
    src = jnp.zeros((n_tiles * tm,), jnp.int32).at[pos].set(jnp.arange(T, dtype=jnp.int32))
    tile_group = jnp.searchsorted(tile_end, jnp.arange(n_tiles, dtype=jnp.int32), side="right")
    tile_group = jnp.minimum(tile_group, n_groups - 1).astype(jnp.int32)
    return pos, src.reshape(n_tiles, 1, tm), tile_group


def _prepare_layer(i, norm_mix, w_in, q_norm, k_norm, lambda_q1, lambda_k1, lambda_q2, lambda_k2,
                   attn_sub_norm, conv_w, w_out, norm_ffn, w_group, b_group, w_erouter, b_erouter,
                   w_gate_up, w_down, w_ple_proj, w_ple_gate, ple_norm):
    D = w_in.shape[1]
    d_mix = w_out.shape[1]
    d_att = d_mix // 2
    d_conv = d_mix - d_att
    n_maps = d_att // _ATT_HEAD_DIM
    n_experts = w_gate_up.shape[1]
    n_groups = w_group.shape[-1]
    d_expert = w_down.shape[2]
    assert n_experts == n_groups * _EXPERTS_PER_GROUP and n_experts + n_groups <= _LANES
    lambda_init = 0.8 - 0.6 * math.exp(-0.3 * i)
    head_of = jnp.arange(d_att) // _ATT_HEAD_DIM
    bd = jnp.where(head_of[:, None] == head_of[None, :], 1.0 / _ATT_HEAD_DIM, 0.0).astype(_BF16)
    wr = jnp.zeros((D, _LANES), _F32)
    wr = wr.at[:, :n_experts].set(w_erouter[i]).at[:, n_experts:n_experts + n_groups].set(w_group[i])
    wr_hi = wr.astype(_BF16)
    wr_lo = (wr - wr_hi.astype(_F32)).astype(_BF16)
    br = jnp.zeros((1, _LANES), _F32)
    br = br.at[0, :n_experts].set(b_erouter[i]).at[0, n_experts:n_experts + n_groups].set(b_group[i])
    wg = w_gate_up[i].reshape(n_groups, _EXPERTS_PER_GROUP, D, 2, d_expert)
    wgu = jnp.transpose(wg, (0, 2, 3, 1, 4)).reshape(n_groups, D, 2 * _EXPERTS_PER_GROUP * d_expert)
    wdn = w_down[i].reshape(n_groups, _EXPERTS_PER_GROUP * d_expert, D)
    return dict(
        d_att=d_att, d_conv=d_conv, n_experts=n_experts, d_expert=d_expert, lambda_init=lambda_init,
        gmix=norm_mix[i][None, :], w_in=w_in[i].astype(_BF16),
        gq_col=q_norm[i][:, None], gk_row=jnp.tile(k_norm[i], n_maps)[None, :], bd=bd,
        lam_vecs=tuple(v[i][None, :] for v in (lambda_q1, lambda_k1, lambda_q2, lambda_k2)),
        gsub_col=attn_sub_norm[i][:, None], conv_w=conv_w[i], w_out=w_out[i].astype(_BF16),
        gffn=norm_ffn[i][None, :], wr=jnp.concatenate([wr_hi, wr_lo], axis=1), br=br,
        wgu=wgu.astype(_BF16), wdn=wdn.astype(_BF16), wproj=w_ple_proj[i].astype(_BF16),
        wgate=w_ple_gate[i].astype(_BF16), gple=ple_norm[i][None, :])


def _score_bound(q_gain, k_gain, rel_bias):
    hd = _ATT_HEAD_DIM
    rounding_slack = 1.02
    q_norm_max = math.sqrt(hd) * jnp.max(jnp.abs(q_gain)) * (_LOG2E / math.sqrt(hd))
    k_norm_max = math.sqrt(hd) * jnp.max(jnp.abs(k_gain))
    qk = rounding_slack * q_norm_max * k_norm_max
    b2 = rel_bias.astype(_F32) * _LOG2E
    b_max, b_min = jnp.max(b2, axis=0), jnp.min(b2, axis=0)
    ok = jnp.all(2.0 * qk + (b_max - b_min) <= _SAFE_EXPONENT_SPAN)
    ok = ok & jnp.isfinite(qk) & jnp.all(jnp.isfinite(b2))
    return ok, jnp.where(ok, qk + b_max, jnp.zeros_like(b_max))


def _layer(x, p_i, L, bias_tiles, bound_ok, lam):
    B, S, D = x.shape
    tm = _TOKEN_TILE
    tile = _ATTN_TILE
    assert S % tm == 0 and S % tile == 0 and tm % _BF16_SUBLANES == 0
    qT, k, vT, b, u = _inproj(x, L["gmix"], L["w_in"], L["gq_col"], L["gk_row"], L["bd"],
                              d_att=L["d_att"], d_conv=L["d_conv"], tm=tm)
    oatt = _attention(bound_ok, lam, qT, k, vT, bias_tiles, L["gsub_col"], tile=tile,
                      out_scale=1.0 - L["lambda_init"])
    x1w, route = _outproj(oatt, b, u, x, L["conv_w"], L["w_out"], L["gffn"], L["wr"], L["br"],
                          n_experts=L["n_experts"], tm=tm)
    x1w = x1w.reshape(B * S, D + _LANES)
    group_of_token = route[:, 0, :].reshape(B * S).astype(jnp.int32)
    pos, src, tile_group = _dispatch_plan(group_of_token, L["wgu"].shape[0], tm)
    moe = _moe_sorted(tile_group, src, x1w, L["gffn"], L["wgu"], L["wdn"], d_expert=L["d_expert"], tm=tm)
    y = _ple(pos.reshape(-1, 1, tm), moe, x1w, p_i.reshape(B * S, -1), L["wproj"], L["wgate"],
             L["gple"], tm=tm)
    return y.reshape(B, S, D)


def kernel(x_prompt, x_sample, p_prompt, p_sample, norm_mix, w_in, q_norm, k_norm, lambda_q1, lambda_k1, lambda_q2, lambda_k2, attn_sub_norm, conv_w, w_out, rel_bias, norm_ffn, w_group, b_group, w_erouter, b_erouter, w_gate_up, w_down, w_ple_proj, w_ple_gate, ple_norm):
    depth = w_in.shape[0]
    layers = []
    for i in range(depth):
        L = _prepare_layer(i, norm_mix, w_in, q_norm, k_norm, lambda_q1, lambda_k1, lambda_q2,
                           lambda_k2, attn_sub_norm, conv_w, w_out, norm_ffn, w_group, b_group,
                           w_erouter, b_erouter, w_gate_up, w_down, w_ple_proj, w_ple_gate, ple_norm)
        bound_ok, shift = _score_bound(q_norm[i], k_norm[i], rel_bias)
        layers.append((L, _bias_tiles(rel_bias, shift, _ATTN_TILE), bound_ok,
                       _lambda(*L["lam_vecs"], L["lambda_init"])))

    def encode(x, p):
        for i, (L, bias_tiles, bound_ok, lam) in enumerate(layers):
            x = _layer(x, p[i], L, bias_tiles, bound_ok, lam)
        return x

    return (encode(x_prompt, p_prompt), encode(x_sample, p_sample))
```

```python
import functools
import math

import jax
import jax.numpy as jnp
from jax import lax
from jax.experimental import pallas as pl
from jax.experimental.pallas import tpu as pltpu

_F32 = jnp.float32
_BF16 = jnp.bfloat16

_EPS = 1e-6
_ATT_HEAD_DIM = 64
_MAX_DISTANCE = 128
_EXPERTS_PER_GROUP = 4
_LANES = 128
_F32_SUBLANES = 8
_BF16_SUBLANES = 16
_ATTN_TILE = 512
_ATTN_ROWS = 128
_ATTN_LANES = 256
_TOKEN_TILE = 512
_PLE_CHUNK = 128
_VMEM_LIMIT_BYTES = 56 * 1024 * 1024
_SAFE_EXPONENT_SPAN = 100.0
_NEG_BIG = -1e30
_LOG2E = math.log2(math.e)


def _cparams(*sem):
    return pltpu.CompilerParams(dimension_semantics=sem, vmem_limit_bytes=_VMEM_LIMIT_BYTES)


def _dot(a, b):
    return jnp.dot(a, b, preferred_element_type=_F32)


def _bias_tile_kernel(tbl_ref, shift_ref, out_ref, *, tile, num_buckets):
    di = pl.program_id(0)
    m = pl.program_id(1)
    delta = (di - 2) * tile
    kk = lax.broadcasted_iota(jnp.int32, (tile, tile), 0)
    qq = lax.broadcasted_iota(jnp.int32, (tile, tile), 1)
    rel = kk - qq + delta
    half = num_buckets // 2
    max_exact = half // 2
    ret = jnp.where(rel > 0, half, 0)
    n = jnp.abs(rel)
    nf = jnp.maximum(n, 1).astype(_F32)
    large = max_exact + (jnp.log(nf / max_exact) / math.log(_MAX_DISTANCE / max_exact)
                         * (half - max_exact)).astype(jnp.int32)
    large = jnp.minimum(large, half - 1)
    bucket = ret + jnp.where(n < max_exact, n, large)
    acc = jnp.zeros((tile, tile), _F32)
    for b in range(num_buckets):
        acc = jnp.where(bucket == b, tbl_ref[b, m], acc)
    out_ref[0, 0] = acc * _LOG2E - shift_ref[m]


def _bias_tiles(rel_bias, shift, tile):
    num_buckets, n_maps = rel_bias.shape
    assert tile >= _MAX_DISTANCE
    return pl.pallas_call(
        functools.partial(_bias_tile_kernel, tile=tile, num_buckets=num_buckets),
        grid=(5, n_maps),
        in_specs=[pl.BlockSpec(memory_space=pltpu.SMEM), pl.BlockSpec(memory_space=pltpu.SMEM)],
        out_specs=pl.BlockSpec((1, 1, tile, tile), lambda di, m: (di, m // 2, 0, m % 2)),
        out_shape=jax.ShapeDtypeStruct((5, n_maps // 2, tile, 2 * tile), _F32),
        compiler_params=_cparams("arbitrary", "arbitrary"),
        name="bias_tiles",
    )(rel_bias.astype(_F32), shift.astype(_F32))


def _inproj_kernel(x_ref, gmix_ref, w_ref, gq_ref, gk_ref, bd_ref,
                   qT_ref, k_ref, vT_ref, b_ref, u_ref, *, d_att, d_conv, scale):
    x = x_ref[0]
    ms = jnp.mean(x * x, axis=-1, keepdims=True)
    h = (x * lax.rsqrt(ms + _EPS) * gmix_ref[...]).astype(_BF16)

    def proj(lo, n):
        return _dot(h, w_ref[:, lo:lo + n])

    tm = x.shape[0]
    n_maps = d_att // _ATT_HEAD_DIM
    zqT = proj(0, d_att).T.reshape(n_maps, _ATT_HEAD_DIM, tm)
    qms = jnp.mean(zqT * zqT, axis=1, keepdims=True)
    qn = zqT * lax.rsqrt(qms + _EPS) * gq_ref[...][None]
    qT_ref[0] = (qn * scale).reshape(d_att, tm).astype(_BF16)
    zk = proj(d_att, d_att)
    kms = _dot((zk * zk).astype(_BF16), bd_ref[...])
    k_ref[0] = (zk * lax.rsqrt(kms + _EPS) * gk_ref[...]).astype(_BF16)
    vT_ref[0] = proj(2 * d_att, d_att).T.astype(_BF16)
    b_ref[0] = proj(3 * d_att, d_conv).astype(_BF16)
    c = proj(3 * d_att + d_conv, d_conv)
    xc = proj(3 * d_att + 2 * d_conv, d_conv)
    u_ref[0] = (c * xc).astype(_BF16)


def _inproj(x, gmix, w_in, gq_col, gk_row, bd, *, d_att, d_conv, tm):
    B, S, D = x.shape
    d_in = w_in.shape[1]
    grid = (B, S // tm)
    tok = lambda width: pl.BlockSpec((1, tm, width), lambda b, i: (b, i, 0))
    tr = lambda rows: pl.BlockSpec((1, rows, tm), lambda b, i: (b, 0, i))
    full = lambda shape: pl.BlockSpec(shape, lambda b, i: (0,) * len(shape))
    return pl.pallas_call(
        functools.partial(_inproj_kernel, d_att=d_att, d_conv=d_conv,
                          scale=_LOG2E / math.sqrt(_ATT_HEAD_DIM)),
        grid=grid,
        in_specs=[tok(D), full((1, D)), full((D, d_in)), full((_ATT_HEAD_DIM, 1)), full((1, d_att)),
                  full((d_att, d_att))],
        out_specs=[tr(d_att), tok(d_att), tr(d_att), tok(d_conv), tok(d_conv)],
        out_shape=[jax.ShapeDtypeStruct((B, d_att, S), _BF16),
                   jax.ShapeDtypeStruct((B, S, d_att), _BF16),
                   jax.ShapeDtypeStruct((B, d_att, S), _BF16),
                   jax.ShapeDtypeStruct((B, S, d_conv), _BF16),
                   jax.ShapeDtypeStruct((B, S, d_conv), _BF16)],
        compiler_params=_cparams("parallel", "parallel"),
        name="inproj",
    )(x, gmix, w_in, gq_col, gk_row, bd)


def _attention_kernel(lam_ref, qT_ref, k_ref, vT_ref, bias_ref, gsub_ref, o_ref,
                      rhs_ref, s0_ref, s1_ref, p0_ref, p1_ref, mx0_ref, mx1_ref, al0_ref, al1_ref,
                      m_ref, l_ref, acc_ref, *, tile, out_scale):
    qi = pl.program_id(2)
    n_k = k_ref.shape[1] // tile
    hd = _ATT_HEAD_DIM
    s_refs, p_refs = (s0_ref, s1_ref), (p0_ref, p1_ref)
    mx_refs, al_refs = (mx0_ref, mx1_ref), (al0_ref, al1_ref)
    qT = qT_ref[0]
    row = lax.broadcasted_iota(jnp.int32, qT.shape, 0)
    zero = jnp.zeros_like(qT)
    rhs_ref[:, :tile] = jnp.where(row < hd, qT, zero)
    rhs_ref[:, tile:] = jnp.where(row >= hd, qT, zero)
    m_ref[...] = jnp.full(m_ref.shape, _NEG_BIG, _F32)
    l_ref[...] = jnp.zeros(l_ref.shape, _F32)
    acc_ref[...] = jnp.zeros(acc_ref.shape, _F32)

    row_blocks = [slice(r, r + _ATTN_ROWS) for r in range(0, tile, _ATTN_ROWS)]
    lane_blocks = [slice(c, c + _ATTN_LANES) for c in range(0, 2 * tile, _ATTN_LANES)]

    def fold8(x, op):
        return op(x.reshape(x.shape[0] // 8, 8, x.shape[1]), axis=0)

    def stage_a(kj, par):
        off = pl.multiple_of(kj * tile, tile)
        bidx = jnp.clip(kj - qi, -2, 2) + 2
        for cs in lane_blocks:
            rhs_c = rhs_ref[:, cs]
            mx = None
            for rs in row_blocks:
                k_blk = k_ref[0, pl.ds(off + rs.start, _ATTN_ROWS), :]
                s = _dot(k_blk, rhs_c) + bias_ref[bidx, 0, rs, cs]
                s_refs[par][rs, cs] = s
                part = fold8(s, jnp.max)
                mx = part if mx is None else jnp.maximum(mx, part)
            mx_refs[par][:, cs] = jnp.max(mx, axis=0, keepdims=True)

    def stage_b(par):
        m_old = m_ref[...]
        m_new = jnp.maximum(m_old, mx_refs[par][...])
        alpha = jnp.exp2(m_old - m_new)
        al_refs[par][...] = alpha
        m_ref[...] = m_new
        for cs in lane_blocks:
            m_c = m_new[:, cs]
            lsum = None
            for rs in row_blocks:
                p = jnp.exp2(s_refs[par][rs, cs] - m_c)
                p_refs[par][rs, cs] = p.astype(_BF16)
                part = fold8(p, jnp.sum)
                lsum = part if lsum is None else lsum + part
            l_ref[:, cs] = alpha[:, cs] * l_ref[:, cs] + jnp.sum(lsum, axis=0, keepdims=True)

    def stage_c(kj, par):
        off = pl.multiple_of(kj * tile, tile)
        for cs in lane_blocks:
            pv = _dot(vT_ref[0, :, pl.ds(off, tile)], p_refs[par][:, cs])
            acc_ref[:, cs] = al_refs[par][:, cs] * acc_ref[:, cs] + pv

    stage_a(0, 0)
    stage_a(1, 1)
    stage_b(0)

    def pair(t, carry):
        j = 2 * t + 1
        stage_a(j + 1, 0)
        stage_b(1)
        stage_c(j - 1, 0)
        stage_a(j + 2, 1)
        stage_b(0)
        stage_c(j, 1)
        return carry

    lax.fori_loop(0, (n_k - 2) // 2, pair, 0)
    stage_b(1)
    stage_c(n_k - 2, 0)
    stage_c(n_k - 1, 1)

    _attention_finish(lam_ref, l_ref, acc_ref, gsub_ref, o_ref, tile=tile, out_scale=out_scale)


def _attention_finish(lam_ref, l_ref, acc_ref, gsub_ref, o_ref, *, tile, out_scale):
    o = acc_ref[...] * (1.0 / l_ref[...])
    oT = o[:, :tile] - lam_ref[0] * o[:, tile:]
    ms = jnp.mean(oT * oT, axis=0, keepdims=True)
    oT = oT * lax.rsqrt(ms + _EPS) * gsub_ref[...] * out_scale
    o_ref[0] = oT.T.astype(_BF16)


def _attention_bounded_kernel(lam_ref, qT_ref, k_ref, vT_ref, bias_ref, gsub_ref, o_ref,
                              rhs_ref, p0_ref, p1_ref, l_ref, acc_ref, *, tile, out_scale):
    qi = pl.program_id(2)
    n_k = k_ref.shape[1] // tile
    hd = _ATT_HEAD_DIM
    p_refs = (p0_ref, p1_ref)
    qT = qT_ref[0]
    row = lax.broadcasted_iota(jnp.int32, qT.shape, 0)
    zero = jnp.zeros_like(qT)
    rhs_ref[:, :tile] = jnp.where(row < hd, qT, zero)
    rhs_ref[:, tile:] = jnp.where(row >= hd, qT, zero)
    l_ref[...] = jnp.zeros(l_ref.shape, _F32)
    acc_ref[...] = jnp.zeros(acc_ref.shape, _F32)
    row_blocks = [slice(r, r + _ATTN_ROWS) for r in range(0, tile, _ATTN_ROWS)]
    lane_blocks = [slice(c, c + _ATTN_LANES) for c in range(0, 2 * tile, _ATTN_LANES)]

    def stage_ab(kj, par):
        off = pl.multiple_of(kj * tile, tile)
        bidx = jnp.clip(kj - qi, -2, 2) + 2
        for cs in lane_blocks:
            rhs_c = rhs_ref[:, cs]
            lsum = None
            for rs in row_blocks:
                k_blk = k_ref[0, pl.ds(off + rs.start, _ATTN_ROWS), :]
                p = jnp.exp2(_dot(k_blk, rhs_c) + bias_ref[bidx, 0, rs, cs])
                p_refs[par][rs, cs] = p.astype(_BF16)
                part = jnp.sum(p.reshape(_ATTN_ROWS // 8, 8, _ATTN_LANES), axis=0)
                lsum = part if lsum is None else lsum + part
            l_ref[:, cs] += jnp.sum(lsum, axis=0, keepdims=True)

    def stage_c(kj, par):
        off = pl.multiple_of(kj * tile, tile)
        for cs in lane_blocks:
            acc_ref[:, cs] += _dot(vT_ref[0, :, pl.ds(off, tile)], p_refs[par][:, cs])

    stage_ab(0, 0)

    def pair(t, carry):
        j = 2 * t
        stage_ab(j + 1, 1)
        stage_c(j, 0)
        stage_ab(j + 2, 0)
        stage_c(j + 1, 1)
        return carry

    lax.fori_loop(0, n_k // 2 - 1, pair, 0)
    stage_ab(n_k - 1, 1)
    stage_c(n_k - 2, 0)
    stage_c(n_k - 1, 1)
    _attention_finish(lam_ref, l_ref, acc_ref, gsub_ref, o_ref, tile=tile, out_scale=out_scale)


def _lambda_kernel(q1_ref, k1_ref, q2_ref, k2_ref, out_ref, *, lambda_init):
    a = jnp.sum(q1_ref[...] * k1_ref[...], axis=-1, keepdims=True)
    b = jnp.sum(q2_ref[...] * k2_ref[...], axis=-1, keepdims=True)
    out_ref[...] = jnp.broadcast_to(jnp.exp(a) - jnp.exp(b) + lambda_init, out_ref.shape)


def _lambda(q1, k1, q2, k2, lambda_init):
    out = pl.pallas_call(
        functools.partial(_lambda_kernel, lambda_init=lambda_init),
        out_shape=jax.ShapeDtypeStruct((1, _LANES), _F32),
        name="lambda_scalar",
    )(q1, k1, q2, k2)
    return out[0, :1]


def _attention(bound_ok, lam, qT, k, vT, bias_tiles, gsub_col, *, tile, out_scale):
    B, d_att, S = qT.shape
    v_dim = 2 * _ATT_HEAD_DIM
    n_heads = d_att // v_dim
    assert S % (2 * tile) == 0
    row = lambda: pltpu.VMEM((1, 2 * tile), _F32)
    s_buf = lambda: pltpu.VMEM((tile, 2 * tile), _F32)
    p_buf = lambda: pltpu.VMEM((tile, 2 * tile), _BF16)
    rhs = pltpu.VMEM((v_dim, 2 * tile), _BF16)
    acc = pltpu.VMEM((v_dim, 2 * tile), _F32)

    def call(body, scratch, name):
        return pl.pallas_call(
            functools.partial(body, tile=tile, out_scale=out_scale),
            grid=(n_heads, B, S // tile),
            in_specs=[pl.BlockSpec(memory_space=pltpu.SMEM),
                      pl.BlockSpec((1, v_dim, tile), lambda h, b, i: (b, h, i)),
                      pl.BlockSpec((1, S, v_dim), lambda h, b, i: (b, 0, h)),
                      pl.BlockSpec((1, v_dim, S), lambda h, b, i: (b, h, 0)),
                      pl.BlockSpec((5, 1, tile, 2 * tile), lambda h, b, i: (0, h, 0, 0)),
                      pl.BlockSpec((v_dim, 1), lambda h, b, i: (0, 0))],
            out_specs=pl.BlockSpec((1, tile, v_dim), lambda h, b, i: (b, i, h)),
            out_shape=jax.ShapeDtypeStruct((B, S, d_att), _BF16),
            scratch_shapes=scratch,
            compiler_params=_cparams("parallel", "parallel", "parallel"),
            name=name,
        )(lam, qT, k, vT, bias_tiles, gsub_col)

    return lax.cond(
        bound_ok,
        lambda: call(_attention_bounded_kernel, [rhs, p_buf(), p_buf(), row(), acc],
                     "diff_attention_bounded"),
        lambda: call(_attention_kernel, [rhs, s_buf(), s_buf(), p_buf(), p_buf(), row(), row(), row(),
                                         row(), row(), row(), acc], "diff_attention"))


def _outproj_kernel(oatt_ref, b_ref, u_ref, uprev_ref, unext_ref, x_ref, cw_ref, wout_ref, gffn_ref,
                    wr_ref, br_ref, x1w_ref, route_ref, *, d_att, n_experts):
    i = pl.program_id(1)
    n_i = pl.num_programs(1)
    u = u_ref[0].astype(_F32)
    tm = u.shape[0]
    prev_row = jnp.where(i > 0, uprev_ref[0, _BF16_SUBLANES - 1:_BF16_SUBLANES, :].astype(_F32), 0.0)
    next_row = jnp.where(i < n_i - 1, unext_ref[0, 0:1, :].astype(_F32), 0.0)
    rows = lax.broadcasted_iota(jnp.int32, (tm, 1), 0)
    u_prev = jnp.where(rows == 0, prev_row, pltpu.roll(u, 1, axis=0))
    u_next = jnp.where(rows == tm - 1, next_row, pltpu.roll(u, tm - 1, axis=0))
    conv = cw_ref[0:1, :] * u_prev + cw_ref[1:2, :] * u + cw_ref[2:3, :] * u_next
    o_conv = (b_ref[0].astype(_F32) * conv).astype(_BF16)
    mix = _dot(oatt_ref[0], wout_ref[:d_att, :]) + _dot(o_conv, wout_ref[d_att:, :])
    x1 = x_ref[0] + mix
    d_model = x1.shape[1]
    x1w_ref[0, :, :d_model] = x1
    ms = jnp.mean(x1 * x1, axis=-1, keepdims=True)
    t = x1 * lax.rsqrt(ms + _EPS) * gffn_ref[...]
    t_hi = t.astype(_BF16)
    t_lo = (t - t_hi.astype(_F32)).astype(_BF16)
    a_hi = _dot(t_hi, wr_ref[...])
    a_lo = _dot(t_lo, wr_ref[...])
    logits = a_hi[:, :_LANES] + a_hi[:, _LANES:] + a_lo[:, :_LANES] + br_ref[...]
    n_groups = n_experts // _EXPERTS_PER_GROUP
    lane = lax.broadcasted_iota(jnp.int32, (1, _LANES), 1)
    lane_f = lane.astype(_F32)
    big = float(_LANES)
    gmask = (lane >= n_experts) & (lane < n_experts + n_groups)
    gl = jnp.where(gmask, logits, _NEG_BIG)
    gmax = jnp.max(gl, axis=-1, keepdims=True)
    gsum = jnp.sum(jnp.where(gmask, jnp.exp(gl - gmax), 0.0), axis=-1, keepdims=True)
    g_w = 1.0 / gsum
    g_idx = jnp.min(jnp.where(gmask & (gl == gmax), lane_f - n_experts, big), axis=-1, keepdims=True)
    lo = g_idx * _EXPERTS_PER_GROUP
    emask = (lane_f >= lo) & (lane_f < lo + _EXPERTS_PER_GROUP)
    el = jnp.where(emask, logits, _NEG_BIG)
    emax = jnp.max(el, axis=-1, keepdims=True)
    ep = jnp.where(emask, jnp.exp(el - emax), 0.0)
    p_exp = ep / jnp.sum(ep, axis=-1, keepdims=True)
    top1 = jnp.max(p_exp, axis=-1, keepdims=True)
    i1 = jnp.min(jnp.where(emask & (p_exp == top1), lane_f, big), axis=-1, keepdims=True)
    rest = jnp.where(emask & (lane_f != i1), p_exp, -1.0)
    top2 = jnp.max(rest, axis=-1, keepdims=True)
    i2 = jnp.min(jnp.where(rest == top2, lane_f, big), axis=-1, keepdims=True)
    denom = top1 + top2
    x1w_ref[0, :, d_model:] = jnp.where(lane_f == i1 - lo, g_w * (top1 / denom),
                                        jnp.where(lane_f == i2 - lo, g_w * (top2 / denom), 0.0))
    route_ref[0] = jnp.broadcast_to(g_idx, (tm, _LANES)).T[:_F32_SUBLANES, :]


def _outproj(oatt, b, u, x, conv_w, w_out, gffn, wr, br, *, n_experts, tm):
    B, S, D = x.shape
    d_att = oatt.shape[-1]
    d_conv = b.shape[-1]
    hb = _BF16_SUBLANES
    per_tile = tm // hb
    n_halo = S // hb
    tok = lambda width: pl.BlockSpec((1, tm, width), lambda bb, i: (bb, i, 0))
    full = lambda shape: pl.BlockSpec(shape, lambda bb, i: (0,) * len(shape))
    return pl.pallas_call(
        functools.partial(_outproj_kernel, d_att=d_att, n_experts=n_experts),
        grid=(B, S // tm),
        in_specs=[tok(d_att), tok(d_conv), tok(d_conv),
                  pl.BlockSpec((1, hb, d_conv), lambda bb, i: (bb, jnp.maximum(i * per_tile - 1, 0), 0)),
                  pl.BlockSpec((1, hb, d_conv),
                               lambda bb, i: (bb, jnp.minimum((i + 1) * per_tile, n_halo - 1), 0)),
                  tok(D), full(conv_w.shape), full(w_out.shape), full((1, D)), full(wr.shape),
                  full((1, _LANES))],
        out_specs=[tok(D + _LANES), pl.BlockSpec((1, _F32_SUBLANES, tm), lambda bb, i: (bb, 0, i))],
        out_shape=[jax.ShapeDtypeStruct((B, S, D + _LANES), _F32),
                   jax.ShapeDtypeStruct((B, _F32_SUBLANES, S), _F32)],
        compiler_params=_cparams("parallel", "parallel"),
        name="outproj_router",
    )(oatt, b, u, u, u, x, conv_w, w_out, gffn, wr, br)


def _row_copies_wait(src_rows, dst_rows, sem):
    pltpu.make_async_copy(src_rows, dst_rows, sem).wait()


def _stage_indices(idx_ref, idx_smem, slot, idx_sem):
    c = pltpu.make_async_copy(idx_ref.at[0, 0], idx_smem.at[slot], idx_sem)
    c.start()
    c.wait()


def _scatter_rows_kernel(pads_ref, pos_ref, x_ref, out_hbm, idx_smem, idx_sem, sem, *, n_groups):
    i = pl.program_id(0)
    tm = x_ref.shape[0]
    _stage_indices(pos_ref, idx_smem, 0, idx_sem)
    for r in range(tm):
        pltpu.make_async_copy(x_ref.at[pl.ds(r, 1), :], out_hbm.at[pl.ds(idx_smem[0, r], 1), :],
                              sem).start()
    _row_copies_wait(x_ref, out_hbm.at[pl.ds(0, tm), :], sem)

    @pl.when(i == pl.num_programs(0) - 1)
    def _():
        for g in range(n_groups):
            lo = pads_ref[g]
            count = pads_ref[n_groups + g] - lo

            def fill(k, carry, lo=lo):
                pltpu.make_async_copy(x_ref.at[pl.ds(0, 1), :], out_hbm.at[pl.ds(lo + k, 1), :], sem).start()
                return carry

            def drain(k, carry):
                _row_copies_wait(x_ref.at[pl.ds(0, 1), :], out_hbm.at[pl.ds(0, 1), :], sem)
                return carry

            lax.fori_loop(0, count, fill, 0)
            lax.fori_loop(0, count, drain, 0)


def _scatter_rows(pads, pos, x, n_out_rows, *, tm):
    T, width = x.shape
    n_steps = T // tm
    return pl.pallas_call(
        functools.partial(_scatter_rows_kernel, n_groups=pads.shape[0] // 2),
        grid_spec=pltpu.PrefetchScalarGridSpec(
            num_scalar_prefetch=1,
            grid=(n_steps,),
            in_specs=[pl.BlockSpec((1, 1, tm), lambda i, pads: (i, 0, 0)),
                      pl.BlockSpec((tm, width), lambda i, pads: (i, 0))],
            out_specs=pl.BlockSpec(memory_space=pl.ANY),
            scratch_shapes=[pltpu.SMEM((1, tm), jnp.int32), pltpu.SemaphoreType.DMA,
                            pltpu.SemaphoreType.DMA]),
        out_shape=jax.ShapeDtypeStruct((n_out_rows, width), x.dtype),
        compiler_params=_cparams("arbitrary"),
        name="dispatch_rows",
    )(pads, pos.reshape(n_steps, 1, tm), x)


def _moe_sorted_kernel(tile_group_ref, xw_ref, gffn_ref, wgu_ref, wdn_ref, out_ref, *, d_expert):
    del tile_group_ref
    d_model = out_ref.shape[1]
    x1 = xw_ref[:, :d_model]
    gates = xw_ref[:, d_model:]
    ms = jnp.mean(x1 * x1, axis=-1, keepdims=True)
    t = (x1 * lax.rsqrt(ms + _EPS) * gffn_ref[...]).astype(_BF16)
    width = _EXPERTS_PER_GROUP * d_expert
    gu = _dot(t, wgu_ref[0])
    g_lin = gu[:, :width]
    act = g_lin * jax.nn.sigmoid(g_lin) * gu[:, width:]
    parts = [(act[:, e * d_expert:(e + 1) * d_expert] * gates[:, e:e + 1]).astype(_BF16)
             for e in range(_EXPERTS_PER_GROUP)]
    out_ref[...] = _dot(jnp.concatenate(parts, axis=1), wdn_ref[0])


def _moe_sorted(tile_group, xw_sorted, gffn, wgu, wdn, *, d_expert, tm):
    rows, width = xw_sorted.shape
    D = width - _LANES
    full = lambda shape: pl.BlockSpec(shape, lambda i, tg: (0,) * len(shape))
    grp = lambda shape: pl.BlockSpec((1,) + shape, lambda i, tg: (tg[i], 0, 0))
    return pl.pallas_call(
        functools.partial(_moe_sorted_kernel, d_expert=d_expert),
        grid_spec=pltpu.PrefetchScalarGridSpec(
            num_scalar_prefetch=1,
            grid=(rows // tm,),
            in_specs=[pl.BlockSpec((tm, width), lambda i, tg: (i, 0)), full((1, D)),
                      grp(wgu.shape[1:]), grp(wdn.shape[1:])],
            out_specs=pl.BlockSpec((tm, D), lambda i, tg: (i, 0))),
        out_shape=jax.ShapeDtypeStruct((rows, D), _F32),
        compiler_params=_cparams("parallel"),
        name="moe_sorted",
    )(tile_group, xw_sorted, gffn, wgu, wdn)


def _ple_kernel(idx_first_ref, idx_b_ref, idx_a_ref, moe_hbm, x1_ref, p_ref, wproj_ref, wgate_ref,
                gple_ref, y_ref, idx_smem, idx_sem, buf, sem):
    j = pl.program_id(0)
    tm = buf.shape[1]

    def rows_start(idx_slot, slot, lo=0, hi=None):
        for r in range(lo, tm if hi is None else hi):
            pltpu.make_async_copy(moe_hbm.at[pl.ds(idx_smem[idx_slot, r], 1), :],
                                  buf.at[slot, pl.ds(r, 1), :], sem.at[slot]).start()

    def rows_wait(slot):
        _row_copies_wait(moe_hbm.at[pl.ds(0, tm), :], buf.at[slot], sem.at[slot])

    def compute_rows(base, slot, lo, hi):
        rows = slice(base + lo, base + hi)
        x2 = x1_ref[rows, :] + buf[slot, lo:hi, :]
        e_raw = _dot(p_ref[rows, :].astype(_BF16), wproj_ref[...])
        ms = jnp.mean(e_raw * e_raw, axis=-1, keepdims=True)
        emb = e_raw * lax.rsqrt(ms + _EPS) * gple_ref[...]
        gate_p = jax.nn.sigmoid(_dot(x2.astype(_BF16), wgate_ref[...]))
        y_ref[rows, :] = x2 + gate_p * emb

    def compute_and_fetch(base, slot, idx_slot, other):
        for lo in range(0, tm, _PLE_CHUNK):
            rows_start(idx_slot, other, lo, lo + _PLE_CHUNK)
            compute_rows(base, slot, lo, lo + _PLE_CHUNK)

    @pl.when(j == 0)
    def _():
        _stage_indices(idx_first_ref, idx_smem, 0, idx_sem)
        rows_start(0, 0)

    _stage_indices(idx_b_ref, idx_smem, 0, idx_sem)
    _stage_indices(idx_a_ref, idx_smem, 1, idx_sem)
    rows_wait(0)
    compute_and_fetch(0, 0, 0, 1)
    rows_wait(1)
    compute_and_fetch(tm, 1, 1, 0)

    @pl.when(j == pl.num_programs(0) - 1)
    def _():
        rows_wait(0)


def _ple(pos, moe_sorted, x1w, p, wproj, wgate, gple, *, tm):
    T = x1w.shape[0]
    D = moe_sorted.shape[1]
    n_tiles = T // tm
    assert n_tiles % 2 == 0
    idx = pos.reshape(n_tiles, 1, tm)
    tok = lambda width: pl.BlockSpec((2 * tm, width), lambda j: (j, 0))
    full = lambda shape: pl.BlockSpec(shape, lambda j: (0,) * len(shape))
    idx_spec = lambda tile_of_step: pl.BlockSpec((1, 1, tm), lambda j: (tile_of_step(j), 0, 0))
    return pl.pallas_call(
        _ple_kernel,
        grid=(n_tiles // 2,),
        in_specs=[idx_spec(lambda j: 0), idx_spec(lambda j: 2 * j + 1),
                  idx_spec(lambda j: jnp.minimum(2 * j + 2, n_tiles - 1)),
                  pl.BlockSpec(memory_space=pl.ANY), tok(D), tok(p.shape[-1]), full(wproj.shape),
                  full(wgate.shape), full((1, D))],
        out_specs=tok(D),
        out_shape=jax.ShapeDtypeStruct((T, D), _F32),
        scratch_shapes=[pltpu.SMEM((2, tm), jnp.int32), pltpu.SemaphoreType.DMA,
                        pltpu.VMEM((2, tm, D), _F32), pltpu.SemaphoreType.DMA((2,))],
        compiler_params=_cparams("arbitrary"),
        name="ple_gate",
    )(idx, idx, idx, moe_sorted, x1w, p, wproj, wgate, gple)


def _dispatch_plan(group_of_token, n_groups, tm):
    T = group_of_token.shape[0]
    n_tiles = T // tm + n_groups - 1
    onehot = (group_of_token[:, None] == jnp.arange(n_groups, dtype=jnp.int32)[None, :]).astype(jnp.int32)
    rank = jnp.take_along_axis(jnp.cumsum(onehot, axis=0), group_of_token[:, None], axis=1)[:, 0] - 1
    count = jnp.sum(onehot, axis=0)
    tiles_per_group = (count + tm - 1) // tm
    tile_end = jnp.cumsum(tiles_per_group)
    tile_start = tile_end - tiles_per_group
    pos = tile_start[group_of_token] * tm + rank
    tile_group = jnp.searchsorted(tile_end, jnp.arange(n_tiles, dtype=jnp.int32), side="right")
    tile_group = jnp.minimum(tile_group, n_groups - 1).astype(jnp.int32)
    pad_lo = tile_start * tm + count
    pad_hi = (tile_end * tm).at[n_groups - 1].set(n_tiles * tm)
    pads = jnp.concatenate([pad_lo, pad_hi]).astype(jnp.int32)
    return pos.astype(jnp.int32), tile_group, pads, n_tiles * tm


def _prepare_layer(i, norm_mix, w_in, q_norm, k_norm, lambda_q1, lambda_k1, lambda_q2, lambda_k2,
                   attn_sub_norm, conv_w, w_out, norm_ffn, w_group, b_group, w_erouter, b_erouter,
                   w_gate_up, w_down, w_ple_proj, w_ple_gate, ple_norm):
    D = w_in.shape[1]
    d_mix = w_out.shape[1]
    d_att = d_mix // 2
    d_conv = d_mix - d_att
    n_maps = d_att // _ATT_HEAD_DIM
    n_experts = w_gate_up.shape[1]
    n_groups = w_group.shape[-1]
    d_expert = w_down.shape[2]
    assert n_experts == n_groups * _EXPERTS_PER_GROUP and n_experts + n_groups <= _LANES
    lambda_init = 0.8 - 0.6 * math.exp(-0.3 * i)
    head_of = jnp.arange(d_att) // _ATT_HEAD_DIM
    bd = jnp.where(head_of[:, None] == head_of[None, :], 1.0 / _ATT_HEAD_DIM, 0.0).astype(_BF16)
    wr = jnp.zeros((D, _LANES), _F32)
    wr = wr.at[:, :n_experts].set(w_erouter[i]).at[:, n_experts:n_experts + n_groups].set(w_group[i])
    wr_hi = wr.astype(_BF16)
    wr_lo = (wr - wr_hi.astype(_F32)).astype(_BF16)
    br = jnp.zeros((1, _LANES), _F32)
    br = br.at[0, :n_experts].set(b_erouter[i]).at[0, n_experts:n_experts + n_groups].set(b_group[i])
    wg = w_gate_up[i].reshape(n_groups, _EXPERTS_PER_GROUP, D, 2, d_expert)
    wgu = jnp.transpose(wg, (0, 2, 3, 1, 4)).reshape(n_groups, D, 2 * _EXPERTS_PER_GROUP * d_expert)
    wdn = w_down[i].reshape(n_groups, _EXPERTS_PER_GROUP * d_expert, D)
    return dict(
        d_att=d_att, d_conv=d_conv, n_experts=n_experts, d_expert=d_expert, lambda_init=lambda_init,
        gmix=norm_mix[i][None, :], w_in=w_in[i].astype(_BF16),
        gq_col=q_norm[i][:, None], gk_row=jnp.tile(k_norm[i], n_maps)[None, :], bd=bd,
        lam_vecs=tuple(v[i][None, :] for v in (lambda_q1, lambda_k1, lambda_q2, lambda_k2)),
        gsub_col=attn_sub_norm[i][:, None], conv_w=conv_w[i], w_out=w_out[i].astype(_BF16),
        gffn=norm_ffn[i][None, :], wr=jnp.concatenate([wr_hi, wr_lo], axis=1), br=br,
        wgu=wgu.astype(_BF16), wdn=wdn.astype(_BF16), wproj=w_ple_proj[i].astype(_BF16),
        wgate=w_ple_gate[i].astype(_BF16), gple=ple_norm[i][None, :])


def _score_bound(q_gain, k_gain, rel_bias):
    hd = _ATT_HEAD_DIM
    rounding_slack = 1.02
    q_norm_max = math.sqrt(hd) * jnp.max(jnp.abs(q_gain)) * (_LOG2E / math.sqrt(hd))
    k_norm_max = math.sqrt(hd) * jnp.max(jnp.abs(k_gain))
    qk = rounding_slack * q_norm_max * k_norm_max
    b2 = rel_bias.astype(_F32) * _LOG2E
    b_max, b_min = jnp.max(b2, axis=0), jnp.min(b2, axis=0)
    ok = jnp.all(2.0 * qk + (b_max - b_min) <= _SAFE_EXPONENT_SPAN)
    ok = ok & jnp.isfinite(qk) & jnp.all(jnp.isfinite(b2))
    return ok, jnp.where(ok, qk + b_max, jnp.zeros_like(b_max))


def _layer(x, p_i, L, bias_tiles, bound_ok, lam):
    B, S, D = x.shape
    tm = _TOKEN_TILE
    tile = _ATTN_TILE
    assert S % tm == 0 and S % tile == 0 and tm % _BF16_SUBLANES == 0
    qT, k, vT, b, u = _inproj(x, L["gmix"], L["w_in"], L["gq_col"], L["gk_row"], L["bd"],
                              d_att=L["d_att"], d_conv=L["d_conv"], tm=tm)
    oatt = _attention(bound_ok, lam, qT, k, vT, bias_tiles, L["gsub_col"], tile=tile,
                      out_scale=1.0 - L["lambda_init"])
    x1w, route = _outproj(oatt, b, u, x, L["conv_w"], L["w_out"], L["gffn"], L["wr"], L["br"],
                          n_experts=L["n_experts"], tm=tm)
    x1w = x1w.reshape(B * S, D + _LANES)
    group_of_token = route[:, 0, :].reshape(B * S).astype(jnp.int32)
    pos, tile_group, pads, sorted_rows = _dispatch_plan(group_of_token, L["wgu"].shape[0], tm)
    xw_sorted = _scatter_rows(pads, pos, x1w, sorted_rows, tm=tm)
    moe = _moe_sorted(tile_group, xw_sorted, L["gffn"], L["wgu"], L["wdn"], d_expert=L["d_expert"], tm=tm)
    y = _ple(pos, moe, x1w, p_i.reshape(B * S, -1), L["wproj"], L["wgate"], L["gple"], tm=tm)
    return y.reshape(B, S, D)


def kernel(x_prompt, x_sample, p_prompt, p_sample, norm_mix, w_in, q_norm, k_norm, lambda_q1, lambda_k1, lambda_q2, lambda_k2, attn_sub_norm, conv_w, w_out, rel_bias, norm_ffn, w_group, b_group, w_erouter, b_erouter, w_gate_up, w_down, w_ple_proj, w_ple_gate, ple_norm):
    depth = w_in.shape[0]
    layers = []
    for i in range(depth):
        L = _prepare_layer(i, norm_mix, w_in, q_norm, k_norm, lambda_q1, lambda_k1, lambda_q2,
                           lambda_k2, attn_sub_norm, conv_w, w_out, norm_ffn, w_group, b_group,
                           w_erouter, b_erouter, w_gate_up, w_down, w_ple_proj, w_ple_gate, ple_norm)
        bound_ok, shift = _score_bound(q_norm[i], k_norm[i], rel_bias)
        layers.append((L, _bias_tiles(rel_bias, shift, _ATTN_TILE), bound_ok,
                       _lambda(*L["lam_vecs"], L["lambda_init"])))

    def encode(x, p):
        for i, (L, bias_tiles, bound_ok, lam) in enumerate(layers):
            x = _layer(x, p[i], L, bias_tiles, bound_ok, lam)
        return x

    return (encode(x_prompt, p_prompt), encode(x_sample, p_sample))
```

```python
import functools
import math

import jax
import jax.numpy as jnp
from jax import lax
from jax.experimental import pallas as pl
from jax.experimental.pallas import tpu as pltpu

_F32 = jnp.float32
_BF16 = jnp.bfloat16

_EPS = 1e-6
_ATT_HEAD_DIM = 64
_MAX_DISTANCE = 128
_EXPERTS_PER_GROUP = 4
_LANES = 128
_F32_SUBLANES = 8
_BF16_SUBLANES = 16
_ATTN_TILE = 512
_ATTN_ROWS = 128
_ATTN_LANES = 256
_TOKEN_TILE = 512
_PLE_CHUNK = 128
_VMEM_LIMIT_BYTES = 56 * 1024 * 1024
_SAFE_EXPONENT_SPAN = 100.0
_NEG_BIG = -1e30
_LOG2E = math.log2(math.e)


def _cparams(*sem):
    return pltpu.CompilerParams(dimension_semantics=sem, vmem_limit_bytes=_VMEM_LIMIT_BYTES)


def _dot(a, b):
    return jnp.dot(a, b, preferred_element_type=_F32)


def _bias_tile_kernel(tbl_ref, shift_ref, out_ref, *, tile, num_buckets):
    di = pl.program_id(0)
    m = pl.program_id(1)
    delta = (di - 2) * tile
    kk = lax.broadcasted_iota(jnp.int32, (tile, tile), 0)
    qq = lax.broadcasted_iota(jnp.int32, (tile, tile), 1)
    rel = kk - qq + delta
    half = num_buckets // 2
    max_exact = half // 2
    ret = jnp.where(rel > 0, half, 0)
    n = jnp.abs(rel)
    nf = jnp.maximum(n, 1).astype(_F32)
    large = max_exact + (jnp.log(nf / max_exact) / math.log(_MAX_DISTANCE / max_exact)
                         * (half - max_exact)).astype(jnp.int32)
    large = jnp.minimum(large, half - 1)
    bucket = ret + jnp.where(n < max_exact, n, large)
    acc = jnp.zeros((tile, tile), _F32)
    for b in range(num_buckets):
        acc = jnp.where(bucket == b, tbl_ref[b, m], acc)
    out_ref[0, 0] = acc * _LOG2E - shift_ref[m]


def _bias_tiles(rel_bias, shift, tile):
    num_buckets, n_maps = rel_bias.shape
    assert tile >= _MAX_DISTANCE
    return pl.pallas_call(
        functools.partial(_bias_tile_kernel, tile=tile, num_buckets=num_buckets),
        grid=(5, n_maps),
        in_specs=[pl.BlockSpec(memory_space=pltpu.SMEM), pl.BlockSpec(memory_space=pltpu.SMEM)],
        out_specs=pl.BlockSpec((1, 1, tile, tile), lambda di, m: (di, m // 2, 0, m % 2)),
        out_shape=jax.ShapeDtypeStruct((5, n_maps // 2, tile, 2 * tile), _F32),
        compiler_params=_cparams("arbitrary", "arbitrary"),
        name="bias_tiles",
    )(rel_bias.astype(_F32), shift.astype(_F32))


def _inproj_kernel(x_ref, gmix_ref, w_ref, gq_ref, gk_ref, bd_ref,
                   qT_ref, k_ref, vT_ref, b_ref, u_ref, *, d_att, d_conv, scale):
    x = x_ref[0]
    ms = jnp.mean(x * x, axis=-1, keepdims=True)
    h = (x * lax.rsqrt(ms + _EPS) * gmix_ref[...]).astype(_BF16)

    def proj(lo, n):
        return _dot(h, w_ref[:, lo:lo + n])

    tm = x.shape[0]
    n_maps = d_att // _ATT_HEAD_DIM
    zqT = proj(0, d_att).T.reshape(n_maps, _ATT_HEAD_DIM, tm)
    qms = jnp.mean(zqT * zqT, axis=1, keepdims=True)
    qn = zqT * lax.rsqrt(qms + _EPS) * gq_ref[...][None]
    qT_ref[0] = (qn * scale).reshape(d_att, tm).astype(_BF16)
    zk = proj(d_att, d_att)
    kms = _dot((zk * zk).astype(_BF16), bd_ref[...])
    k_ref[0] = (zk * lax.rsqrt(kms + _EPS) * gk_ref[...]).astype(_BF16)
    vT_ref[0] = proj(2 * d_att, d_att).T.astype(_BF16)
    b_ref[0] = proj(3 * d_att, d_conv).astype(_BF16)
    c = proj(3 * d_att + d_conv, d_conv)
    xc = proj(3 * d_att + 2 * d_conv, d_conv)
    u_ref[0] = (c * xc).astype(_BF16)


def _inproj(x, gmix, w_in, gq_col, gk_row, bd, *, d_att, d_conv, tm):
    B, S, D = x.shape
    d_in = w_in.shape[1]
    grid = (B, S // tm)
    tok = lambda width: pl.BlockSpec((1, tm, width), lambda b, i: (b, i, 0))
    tr = lambda rows: pl.BlockSpec((1, rows, tm), lambda b, i: (b, 0, i))
    full = lambda shape: pl.BlockSpec(shape, lambda b, i: (0,) * len(shape))
    return pl.pallas_call(
        functools.partial(_inproj_kernel, d_att=d_att, d_conv=d_conv,
                          scale=_LOG2E / math.sqrt(_ATT_HEAD_DIM)),
        grid=grid,
        in_specs=[tok(D), full((1, D)), full((D, d_in)), full((_ATT_HEAD_DIM, 1)), full((1, d_att)),
                  full((d_att, d_att))],
        out_specs=[tr(d_att), tok(d_att), tr(d_att), tok(d_conv), tok(d_conv)],
        out_shape=[jax.ShapeDtypeStruct((B, d_att, S), _BF16),
                   jax.ShapeDtypeStruct((B, S, d_att), _BF16),
                   jax.ShapeDtypeStruct((B, d_att, S), _BF16),
                   jax.ShapeDtypeStruct((B, S, d_conv), _BF16),
                   jax.ShapeDtypeStruct((B, S, d_conv), _BF16)],
        compiler_params=_cparams("parallel", "parallel"),
        name="inproj",
    )(x, gmix, w_in, gq_col, gk_row, bd)


def _attention_kernel(lam_ref, qT_ref, k_ref, vT_ref, bias_ref, gsub_ref, o_ref,
                      rhs_ref, s0_ref, s1_ref, p0_ref, p1_ref, mx0_ref, mx1_ref, al0_ref, al1_ref,
                      m_ref, l_ref, acc_ref, *, tile, out_scale):
    qi = pl.program_id(2)
    n_k = k_ref.shape[1] // tile
    hd = _ATT_HEAD_DIM
    s_refs, p_refs = (s0_ref, s1_ref), (p0_ref, p1_ref)
    mx_refs, al_refs = (mx0_ref, mx1_ref), (al0_ref, al1_ref)
    qT = qT_ref[0]
    row = lax.broadcasted_iota(jnp.int32, qT.shape, 0)
    zero = jnp.zeros_like(qT)
    rhs_ref[:, :tile] = jnp.where(row < hd, qT, zero)
    rhs_ref[:, tile:] = jnp.where(row >= hd, qT, zero)
    m_ref[...] = jnp.full(m_ref.shape, _NEG_BIG, _F32)
    l_ref[...] = jnp.zeros(l_ref.shape, _F32)
    acc_ref[...] = jnp.zeros(acc_ref.shape, _F32)

    row_blocks = [slice(r, r + _ATTN_ROWS) for r in range(0, tile, _ATTN_ROWS)]
    lane_blocks = [slice(c, c + _ATTN_LANES) for c in range(0, 2 * tile, _ATTN_LANES)]

    def fold8(x, op):
        return op(x.reshape(x.shape[0] // 8, 8, x.shape[1]), axis=0)

    def stage_a(kj, par):
        off = pl.multiple_of(kj * tile, tile)
        bidx = jnp.clip(kj - qi, -2, 2) + 2
        for cs in lane_blocks:
            rhs_c = rhs_ref[:, cs]
            mx = None
            for rs in row_blocks:
                k_blk = k_ref[0, pl.ds(off + rs.start, _ATTN_ROWS), :]
                s = _dot(k_blk, rhs_c) + bias_ref[bidx, 0, rs, cs]
                s_refs[par][rs, cs] = s
                part = fold8(s, jnp.max)
                mx = part if mx is None else jnp.maximum(mx, part)
            mx_refs[par][:, cs] = jnp.max(mx, axis=0, keepdims=True)

    def stage_b(par):
        m_old = m_ref[...]
        m_new = jnp.maximum(m_old, mx_refs[par][...])
        alpha = jnp.exp2(m_old - m_new)
        al_refs[par][...] = alpha
        m_ref[...] = m_new
        for cs in lane_blocks:
            m_c = m_new[:, cs]
            lsum = None
            for rs in row_blocks:
                p = jnp.exp2(s_refs[par][rs, cs] - m_c)
                p_refs[par][rs, cs] = p.astype(_BF16)
                part = fold8(p, jnp.sum)
                lsum = part if lsum is None else lsum + part
            l_ref[:, cs] = alpha[:, cs] * l_ref[:, cs] + jnp.sum(lsum, axis=0, keepdims=True)

    def stage_c(kj, par):
        off = pl.multiple_of(kj * tile, tile)
        for cs in lane_blocks:
            pv = _dot(vT_ref[0, :, pl.ds(off, tile)], p_refs[par][:, cs])
            acc_ref[:, cs] = al_refs[par][:, cs] * acc_ref[:, cs] + pv

    stage_a(0, 0)
    stage_a(1, 1)
    stage_b(0)

    def pair(t, carry):
        j = 2 * t + 1
        stage_a(j + 1, 0)
        stage_b(1)
        stage_c(j - 1, 0)
        stage_a(j + 2, 1)
        stage_b(0)
        stage_c(j, 1)
        return carry

    lax.fori_loop(0, (n_k - 2) // 2, pair, 0)
    stage_b(1)
    stage_c(n_k - 2, 0)
    stage_c(n_k - 1, 1)

    o_ref[0, 0] = _attention_output(lam_ref, l_ref[...], acc_ref[...], gsub_ref, tile=tile,
                                    out_scale=out_scale)


def _attention_output(lam_ref, l, acc, gsub_ref, *, tile, out_scale):
    o = acc * (1.0 / l)
    oT = o[:, :tile] - lam_ref[0] * o[:, tile:]
    ms = jnp.mean(oT * oT, axis=0, keepdims=True)
    oT = oT * lax.rsqrt(ms + _EPS) * gsub_ref[...] * out_scale
    return oT.T.astype(_BF16)


def _attention_bounded_kernel(lam_ref, qT_ref, k_ref, vT_ref, bias_ref, gsub_ref, o_ref,
                              rhs_ref, p0_ref, p1_ref, l_ref, acc_ref, *, tile, out_scale):
    seq = k_ref.shape[1]
    n_k = seq // tile
    hd = _ATT_HEAD_DIM
    p_refs = (p0_ref, p1_ref)
    row_blocks = [slice(r, r + _ATTN_ROWS) for r in range(0, tile, _ATTN_ROWS)]
    lane_blocks = [slice(c, c + _ATTN_LANES) for c in range(0, 2 * tile, _ATTN_LANES)]

    def setup_q(qi, q_load):
        qpar = qi % 2
        qT = qT_ref[0, :, pl.ds(pl.multiple_of(q_load * tile, tile), tile)]
        row = lax.broadcasted_iota(jnp.int32, qT.shape, 0)
        zero = jnp.zeros_like(qT)
        rhs_ref[qpar, :, :tile] = jnp.where(row < hd, qT, zero)
        rhs_ref[qpar, :, tile:] = jnp.where(row >= hd, qT, zero)
        l_ref[qpar] = jnp.zeros(l_ref.shape[1:], _F32)

    def stage_ab(qi, kj, par):
        qpar = qi % 2
        off = pl.multiple_of(kj * tile, tile)
        bidx = jnp.clip(kj - qi, -2, 2) + 2
        for cs in lane_blocks:
            rhs_c = rhs_ref[qpar, :, cs]
            lsum = None
            for rs in row_blocks:
                k_blk = k_ref[0, pl.ds(off + rs.start, _ATTN_ROWS), :]
                p = jnp.exp2(_dot(k_blk, rhs_c) + bias_ref[bidx, 0, rs, cs])
                p_refs[par][rs, cs] = p.astype(_BF16)
                part = jnp.sum(p.reshape(_ATTN_ROWS // 8, 8, _ATTN_LANES), axis=0)
                lsum = part if lsum is None else lsum + part
            l_ref[qpar, :, cs] += jnp.sum(lsum, axis=0, keepdims=True)

    def stage_c(kj, par):
        off = pl.multiple_of(kj * tile, tile)
        for cs in lane_blocks:
            acc_ref[:, cs] += _dot(vT_ref[0, :, pl.ds(off, tile)], p_refs[par][:, cs])

    n_q = seq // tile
    acc_ref[...] = jnp.zeros(acc_ref.shape, _F32)
    setup_q(0, 0)
    stage_ab(0, 0, 0)

    def q_tile_body(qi, carry):
        def pair(t, c):
            j = 2 * t
            stage_ab(qi, j + 1, 1)
            stage_c(j, 0)
            stage_ab(qi, j + 2, 0)
            stage_c(j + 1, 1)
            return c

        lax.fori_loop(0, n_k // 2 - 1, pair, 0)
        stage_ab(qi, n_k - 1, 1)
        stage_c(n_k - 2, 0)
        q_next = qi + 1
        setup_q(q_next, jnp.minimum(q_next, n_q - 1))
        stage_ab(q_next, 0, 0)
        stage_c(n_k - 1, 1)
        o_ref[0, 0, pl.ds(pl.multiple_of(qi * tile, tile), tile), :] = _attention_output(
            lam_ref, l_ref[qi % 2], acc_ref[...], gsub_ref, tile=tile, out_scale=out_scale)
        acc_ref[...] = jnp.zeros(acc_ref.shape, _F32)
        return carry

    lax.fori_loop(0, n_q, q_tile_body, 0)


def _lambda_kernel(q1_ref, k1_ref, q2_ref, k2_ref, out_ref, *, lambda_init):
    a = jnp.sum(q1_ref[...] * k1_ref[...], axis=-1, keepdims=True)
    b = jnp.sum(q2_ref[...] * k2_ref[...], axis=-1, keepdims=True)
    out_ref[...] = jnp.broadcast_to(jnp.exp(a) - jnp.exp(b) + lambda_init, out_ref.shape)


def _lambda(q1, k1, q2, k2, lambda_init):
    out = pl.pallas_call(
        functools.partial(_lambda_kernel, lambda_init=lambda_init),
        out_shape=jax.ShapeDtypeStruct((1, _LANES), _F32),
        name="lambda_scalar",
    )(q1, k1, q2, k2)
    return out[0, :1]


def _attention(bound_ok, lam, qT, k, vT, bias_tiles, gsub_col, *, tile, out_scale):
    B, d_att, S = qT.shape
    v_dim = 2 * _ATT_HEAD_DIM
    n_heads = d_att // v_dim
    assert S % (2 * tile) == 0
    row = lambda: pltpu.VMEM((1, 2 * tile), _F32)
    s_buf = lambda: pltpu.VMEM((tile, 2 * tile), _F32)
    p_buf = lambda: pltpu.VMEM((tile, 2 * tile), _BF16)
    rhs = pltpu.VMEM((v_dim, 2 * tile), _BF16)
    acc = pltpu.VMEM((v_dim, 2 * tile), _F32)

    out_shape = jax.ShapeDtypeStruct((B, n_heads, S, v_dim), _BF16)
    operands = (lam, qT, k, vT, bias_tiles, gsub_col)

    def bounded():
        return pl.pallas_call(
            functools.partial(_attention_bounded_kernel, tile=tile, out_scale=out_scale),
            grid=(n_heads, B),
            in_specs=[pl.BlockSpec(memory_space=pltpu.SMEM),
                      pl.BlockSpec((1, v_dim, S), lambda h, b: (b, h, 0)),
                      pl.BlockSpec((1, S, v_dim), lambda h, b: (b, 0, h)),
                      pl.BlockSpec((1, v_dim, S), lambda h, b: (b, h, 0)),
                      pl.BlockSpec((5, 1, tile, 2 * tile), lambda h, b: (0, h, 0, 0)),
                      pl.BlockSpec((v_dim, 1), lambda h, b: (0, 0))],
            out_specs=pl.BlockSpec((1, 1, S, v_dim), lambda h, b: (b, h, 0, 0)),
            out_shape=out_shape,
            scratch_shapes=[pltpu.VMEM((2, v_dim, 2 * tile), _BF16), p_buf(), p_buf(),
                            pltpu.VMEM((2, 1, 2 * tile), _F32), acc],
            compiler_params=_cparams("parallel", "parallel"),
            name="diff_attention_bounded",
        )(*operands)

    def exact():
        return pl.pallas_call(
            functools.partial(_attention_kernel, tile=tile, out_scale=out_scale),
            grid=(n_heads, B, S // tile),
            in_specs=[pl.BlockSpec(memory_space=pltpu.SMEM),
                      pl.BlockSpec((1, v_dim, tile), lambda h, b, i: (b, h, i)),
                      pl.BlockSpec((1, S, v_dim), lambda h, b, i: (b, 0, h)),
                      pl.BlockSpec((1, v_dim, S), lambda h, b, i: (b, h, 0)),
                      pl.BlockSpec((5, 1, tile, 2 * tile), lambda h, b, i: (0, h, 0, 0)),
                      pl.BlockSpec((v_dim, 1), lambda h, b, i: (0, 0))],
            out_specs=pl.BlockSpec((1, 1, tile, v_dim), lambda h, b, i: (b, h, i, 0)),
            out_shape=out_shape,
            scratch_shapes=[rhs, s_buf(), s_buf(), p_buf(), p_buf(), row(), row(), row(), row(), row(),
                            row(), acc],
            compiler_params=_cparams("parallel", "parallel", "parallel"),
            name="diff_attention",
        )(*operands)

    return lax.cond(bound_ok, bounded, exact)


def _outproj_kernel(oatt_ref, b_ref, u_ref, uprev_ref, unext_ref, x_ref, cw_ref, wout_ref, gffn_ref,
                    wr_ref, br_ref, x1w_ref, route_ref, *, d_att, n_experts):
    i = pl.program_id(1)
    n_i = pl.num_programs(1)
    u = u_ref[0].astype(_F32)
    tm = u.shape[0]
    prev_row = jnp.where(i > 0, uprev_ref[0, _BF16_SUBLANES - 1:_BF16_SUBLANES, :].astype(_F32), 0.0)
    next_row = jnp.where(i < n_i - 1, unext_ref[0, 0:1, :].astype(_F32), 0.0)
    rows = lax.broadcasted_iota(jnp.int32, (tm, 1), 0)
    u_prev = jnp.where(rows == 0, prev_row, pltpu.roll(u, 1, axis=0))
    u_next = jnp.where(rows == tm - 1, next_row, pltpu.roll(u, tm - 1, axis=0))
    conv = cw_ref[0:1, :] * u_prev + cw_ref[1:2, :] * u + cw_ref[2:3, :] * u_next
    o_conv = (b_ref[0].astype(_F32) * conv).astype(_BF16)
    o_att = jnp.concatenate([oatt_ref[0, h] for h in range(oatt_ref.shape[1])], axis=1)
    mix = _dot(o_att, wout_ref[:d_att, :]) + _dot(o_conv, wout_ref[d_att:, :])
    x1 = x_ref[0] + mix
    d_model = x1.shape[1]
    x1w_ref[0, :, :d_model] = x1
    ms = jnp.mean(x1 * x1, axis=-1, keepdims=True)
    t = x1 * lax.rsqrt(ms + _EPS) * gffn_ref[...]
    t_hi = t.astype(_BF16)
    t_lo = (t - t_hi.astype(_F32)).astype(_BF16)
    a_hi = _dot(t_hi, wr_ref[...])
    a_lo = _dot(t_lo, wr_ref[...])
    logits = a_hi[:, :_LANES] + a_hi[:, _LANES:] + a_lo[:, :_LANES] + br_ref[...]
    n_groups = n_experts // _EXPERTS_PER_GROUP
    lane = lax.broadcasted_iota(jnp.int32, (1, _LANES), 1)
    lane_f = lane.astype(_F32)
    big = float(_LANES)
    gmask = (lane >= n_experts) & (lane < n_experts + n_groups)
    gl = jnp.where(gmask, logits, _NEG_BIG)
    gmax = jnp.max(gl, axis=-1, keepdims=True)
    gsum = jnp.sum(jnp.where(gmask, jnp.exp(gl - gmax), 0.0), axis=-1, keepdims=True)
    g_w = 1.0 / gsum
    g_idx = jnp.min(jnp.where(gmask & (gl == gmax), lane_f - n_experts, big), axis=-1, keepdims=True)
    lo = g_idx * _EXPERTS_PER_GROUP
    emask = (lane_f >= lo) & (lane_f < lo + _EXPERTS_PER_GROUP)
    el = jnp.where(emask, logits, _NEG_BIG)
    emax = jnp.max(el, axis=-1, keepdims=True)
    ep = jnp.where(emask, jnp.exp(el - emax), 0.0)
    p_exp = ep / jnp.sum(ep, axis=-1, keepdims=True)
    top1 = jnp.max(p_exp, axis=-1, keepdims=True)
    i1 = jnp.min(jnp.where(emask & (p_exp == top1), lane_f, big), axis=-1, keepdims=True)
    rest = jnp.where(emask & (lane_f != i1), p_exp, -1.0)
    top2 = jnp.max(rest, axis=-1, keepdims=True)
    i2 = jnp.min(jnp.where(rest == top2, lane_f, big), axis=-1, keepdims=True)
    denom = top1 + top2
    x1w_ref[0, :, d_model:] = jnp.where(lane_f == i1 - lo, g_w * (top1 / denom),
                                        jnp.where(lane_f == i2 - lo, g_w * (top2 / denom), 0.0))
    route_ref[0] = jnp.broadcast_to(g_idx, (tm, _LANES)).T[:_F32_SUBLANES, :]


def _outproj(oatt, b, u, x, conv_w, w_out, gffn, wr, br, *, n_experts, tm):
    B, S, D = x.shape
    n_heads, v_dim = oatt.shape[1], oatt.shape[3]
    d_att = n_heads * v_dim
    d_conv = b.shape[-1]
    hb = _BF16_SUBLANES
    per_tile = tm // hb
    n_halo = S // hb
    tok = lambda width: pl.BlockSpec((1, tm, width), lambda bb, i: (bb, i, 0))
    full = lambda shape: pl.BlockSpec(shape, lambda bb, i: (0,) * len(shape))
    return pl.pallas_call(
        functools.partial(_outproj_kernel, d_att=d_att, n_experts=n_experts),
        grid=(B, S // tm),
        in_specs=[pl.BlockSpec((1, n_heads, tm, v_dim), lambda bb, i: (bb, 0, i, 0)), tok(d_conv), tok(d_conv),
                  pl.BlockSpec((1, hb, d_conv), lambda bb, i: (bb, jnp.maximum(i * per_tile - 1, 0), 0)),
                  pl.BlockSpec((1, hb, d_conv),
                               lambda bb, i: (bb, jnp.minimum((i + 1) * per_tile, n_halo - 1), 0)),
                  tok(D), full(conv_w.shape), full(w_out.shape), full((1, D)), full(wr.shape),
                  full((1, _LANES))],
        out_specs=[tok(D + _LANES), pl.BlockSpec((1, _F32_SUBLANES, tm), lambda bb, i: (bb, 0, i))],
        out_shape=[jax.ShapeDtypeStruct((B, S, D + _LANES), _F32),
                   jax.ShapeDtypeStruct((B, _F32_SUBLANES, S), _F32)],
        compiler_params=_cparams("parallel", "parallel"),
        name="outproj_router",
    )(oatt, b, u, u, u, x, conv_w, w_out, gffn, wr, br)


def _row_copies_wait(src_rows, dst_rows, sem):
    pltpu.make_async_copy(src_rows, dst_rows, sem).wait()


def _stage_indices(idx_ref, idx_smem, slot, idx_sem):
    c = pltpu.make_async_copy(idx_ref.at[0, 0], idx_smem.at[slot], idx_sem)
    c.start()
    c.wait()


def _scatter_rows_kernel(pads_ref, pos_ref, x_ref, out_hbm, idx_smem, idx_sem, sem, *, n_groups):
    i = pl.program_id(0)
    tm = x_ref.shape[0]
    _stage_indices(pos_ref, idx_smem, 0, idx_sem)
    for r in range(tm):
        pltpu.make_async_copy(x_ref.at[pl.ds(r, 1), :], out_hbm.at[pl.ds(idx_smem[0, r], 1), :],
                              sem).start()
    _row_copies_wait(x_ref, out_hbm.at[pl.ds(0, tm), :], sem)

    @pl.when(i == pl.num_programs(0) - 1)
    def _():
        for g in range(n_groups):
            lo = pads_ref[g]
            count = pads_ref[n_groups + g] - lo

            def fill(k, carry, lo=lo):
                pltpu.make_async_copy(x_ref.at[pl.ds(0, 1), :], out_hbm.at[pl.ds(lo + k, 1), :], sem).start()
                return carry

            def drain(k, carry):
                _row_copies_wait(x_ref.at[pl.ds(0, 1), :], out_hbm.at[pl.ds(0, 1), :], sem)
                return carry

            lax.fori_loop(0, count, fill, 0)
            lax.fori_loop(0, count, drain, 0)


def _scatter_rows(pads, pos, x, n_out_rows, *, tm):
    T, width = x.shape
    n_steps = T // tm
    return pl.pallas_call(
        functools.partial(_scatter_rows_kernel, n_groups=pads.shape[0] // 2),
        grid_spec=pltpu.PrefetchScalarGridSpec(
            num_scalar_prefetch=1,
            grid=(n_steps,),
            in_specs=[pl.BlockSpec((1, 1, tm), lambda i, pads: (i, 0, 0)),
                      pl.BlockSpec((tm, width), lambda i, pads: (i, 0))],
            out_specs=pl.BlockSpec(memory_space=pl.ANY),
            scratch_shapes=[pltpu.SMEM((1, tm), jnp.int32), pltpu.SemaphoreType.DMA,
                            pltpu.SemaphoreType.DMA]),
        out_shape=jax.ShapeDtypeStruct((n_out_rows, width), x.dtype),
        compiler_params=_cparams("arbitrary"),
        name="dispatch_rows",
    )(pads, pos.reshape(n_steps, 1, tm), x)


def _moe_sorted_kernel(tile_group_ref, xw_ref, gffn_ref, wgu_ref, wdn_ref, out_ref, *, d_expert):
    del tile_group_ref
    d_model = out_ref.shape[1]
    x1 = xw_ref[:, :d_model]
    gates = xw_ref[:, d_model:]
    ms = jnp.mean(x1 * x1, axis=-1, keepdims=True)
    t = (x1 * lax.rsqrt(ms + _EPS) * gffn_ref[...]).astype(_BF16)
    width = _EXPERTS_PER_GROUP * d_expert
    gu = _dot(t, wgu_ref[0])
    g_lin = gu[:, :width]
    act = g_lin * jax.nn.sigmoid(g_lin) * gu[:, width:]
    parts = [(act[:, e * d_expert:(e + 1) * d_expert] * gates[:, e:e + 1]).astype(_BF16)
             for e in range(_EXPERTS_PER_GROUP)]
    out_ref[...] = _dot(jnp.concatenate(parts, axis=1), wdn_ref[0])


def _moe_sorted(tile_group, xw_sorted, gffn, wgu, wdn, *, d_expert, tm):
    rows, width = xw_sorted.shape
    D = width - _LANES
    full = lambda shape: pl.BlockSpec(shape, lambda i, tg: (0,) * len(shape))
    grp = lambda shape: pl.BlockSpec((1,) + shape, lambda i, tg: (tg[i], 0, 0))
    return pl.pallas_call(
        functools.partial(_moe_sorted_kernel, d_expert=d_expert),
        grid_spec=pltpu.PrefetchScalarGridSpec(
            num_scalar_prefetch=1,
            grid=(rows // tm,),
            in_specs=[pl.BlockSpec((tm, width), lambda i, tg: (i, 0)), full((1, D)),
                      grp(wgu.shape[1:]), grp(wdn.shape[1:])],
            out_specs=pl.BlockSpec((tm, D), lambda i, tg: (i, 0))),
        out_shape=jax.ShapeDtypeStruct((rows, D), _F32),
        compiler_params=_cparams("parallel"),
        name="moe_sorted",
    )(tile_group, xw_sorted, gffn, wgu, wdn)


def _ple_kernel(idx_first_ref, idx_b_ref, idx_a_ref, moe_hbm, x1_ref, p_ref, wproj_ref, wgate_ref,
                gple_ref, y_ref, idx_smem, idx_sem, buf, sem):
    j = pl.program_id(0)
    tm = buf.shape[1]

    def rows_start(idx_slot, slot, lo=0, hi=None):
        for r in range(lo, tm if hi is None else hi):
            pltpu.make_async_copy(moe_hbm.at[pl.ds(idx_smem[idx_slot, r], 1), :],
                                  buf.at[slot, pl.ds(r, 1), :], sem.at[slot]).start()

    def rows_wait(slot):
        _row_copies_wait(moe_hbm.at[pl.ds(0, tm), :], buf.at[slot], sem.at[slot])

    def compute_rows(base, slot, lo, hi):
        rows = slice(base + lo, base + hi)
        x2 = x1_ref[rows, :] + buf[slot, lo:hi, :]
        e_raw = _dot(p_ref[rows, :].astype(_BF16), wproj_ref[...])
        ms = jnp.mean(e_raw * e_raw, axis=-1, keepdims=True)
        emb = e_raw * lax.rsqrt(ms + _EPS) * gple_ref[...]
        gate_p = jax.nn.sigmoid(_dot(x2.astype(_BF16), wgate_ref[...]))
        y_ref[rows, :] = x2 + gate_p * emb

    def compute_and_fetch(base, slot, idx_slot, other):
        for lo in range(0, tm, _PLE_CHUNK):
            rows_start(idx_slot, other, lo, lo + _PLE_CHUNK)
            compute_rows(base, slot, lo, lo + _PLE_CHUNK)

    @pl.when(j == 0)
    def _():
        _stage_indices(idx_first_ref, idx_smem, 0, idx_sem)
        rows_start(0, 0)

    _stage_indices(idx_b_ref, idx_smem, 0, idx_sem)
    _stage_indices(idx_a_ref, idx_smem, 1, idx_sem)
    rows_wait(0)
    compute_and_fetch(0, 0, 0, 1)
    rows_wait(1)
    compute_and_fetch(tm, 1, 1, 0)

    @pl.when(j == pl.num_programs(0) - 1)
    def _():
        rows_wait(0)


def _ple(pos, moe_sorted, x1w, p, wproj, wgate, gple, *, tm):
    T = x1w.shape[0]
    D = moe_sorted.shape[1]
    n_tiles = T // tm
    assert n_tiles % 2 == 0
    idx = pos.reshape(n_tiles, 1, tm)
    tok = lambda width: pl.BlockSpec((2 * tm, width), lambda j: (j, 0))
    full = lambda shape: pl.BlockSpec(shape, lambda j: (0,) * len(shape))
    idx_spec = lambda tile_of_step: pl.BlockSpec((1, 1, tm), lambda j: (tile_of_step(j), 0, 0))
    return pl.pallas_call(
        _ple_kernel,
        grid=(n_tiles // 2,),
        in_specs=[idx_spec(lambda j: 0), idx_spec(lambda j: 2 * j + 1),
                  idx_spec(lambda j: jnp.minimum(2 * j + 2, n_tiles - 1)),
                  pl.BlockSpec(memory_space=pl.ANY), tok(D), tok(p.shape[-1]), full(wproj.shape),
                  full(wgate.shape), full((1, D))],
        out_specs=tok(D),
        out_shape=jax.ShapeDtypeStruct((T, D), _F32),
        scratch_shapes=[pltpu.SMEM((2, tm), jnp.int32), pltpu.SemaphoreType.DMA,
                        pltpu.VMEM((2, tm, D), _F32), pltpu.SemaphoreType.DMA((2,))],
        compiler_params=_cparams("arbitrary"),
        name="ple_gate",
    )(idx, idx, idx, moe_sorted, x1w, p, wproj, wgate, gple)


def _dispatch_plan(group_of_token, n_groups, tm):
    T = group_of_token.shape[0]
    n_tiles = T // tm + n_groups - 1
    onehot = (group_of_token[:, None] == jnp.arange(n_groups, dtype=jnp.int32)[None, :]).astype(jnp.int32)
    rank = jnp.take_along_axis(jnp.cumsum(onehot, axis=0), group_of_token[:, None], axis=1)[:, 0] - 1
    count = jnp.sum(onehot, axis=0)
    tiles_per_group = (count + tm - 1) // tm
    tile_end = jnp.cumsum(tiles_per_group)
    tile_start = tile_end - tiles_per_group
    pos = tile_start[group_of_token] * tm + rank
    tile_group = jnp.searchsorted(tile_end, jnp.arange(n_tiles, dtype=jnp.int32), side="right")
    tile_group = jnp.minimum(tile_group, n_groups - 1).astype(jnp.int32)
    pad_lo = tile_start * tm + count
    pad_hi = (tile_end * tm).at[n_groups - 1].set(n_tiles * tm)
    pads = jnp.concatenate([pad_lo, pad_hi]).astype(jnp.int32)
    return pos.astype(jnp.int32), tile_group, pads, n_tiles * tm


def _prepare_layer(i, norm_mix, w_in, q_norm, k_norm, lambda_q1, lambda_k1, lambda_q2, lambda_k2,
                   attn_sub_norm, conv_w, w_out, norm_ffn, w_group, b_group, w_erouter, b_erouter,
                   w_gate_up, w_down, w_ple_proj, w_ple_gate, ple_norm):
    D = w_in.shape[1]
    d_mix = w_out.shape[1]
    d_att = d_mix // 2
    d_conv = d_mix - d_att
    n_maps = d_att // _ATT_HEAD_DIM
    n_experts = w_gate_up.shape[1]
    n_groups = w_group.shape[-1]
    d_expert = w_down.shape[2]
    assert n_experts == n_groups * _EXPERTS_PER_GROUP and n_experts + n_groups <= _LANES
    lambda_init = 0.8 - 0.6 * math.exp(-0.3 * i)
    head_of = jnp.arange(d_att) // _ATT_HEAD_DIM
    bd = jnp.where(head_of[:, None] == head_of[None, :], 1.0 / _ATT_HEAD_DIM, 0.0).astype(_BF16)
    wr = jnp.zeros((D, _LANES), _F32)
    wr = wr.at[:, :n_experts].set(w_erouter[i]).at[:, n_experts:n_experts + n_groups].set(w_group[i])
    wr_hi = wr.astype(_BF16)
    wr_lo = (wr - wr_hi.astype(_F32)).astype(_BF16)
    br = jnp.zeros((1, _LANES), _F32)
    br = br.at[0, :n_experts].set(b_erouter[i]).at[0, n_experts:n_experts + n_groups].set(b_group[i])
    wg = w_gate_up[i].reshape(n_groups, _EXPERTS_PER_GROUP, D, 2, d_expert)
    wgu = jnp.transpose(wg, (0, 2, 3, 1, 4)).reshape(n_groups, D, 2 * _EXPERTS_PER_GROUP * d_expert)
    wdn = w_down[i].reshape(n_groups, _EXPERTS_PER_GROUP * d_expert, D)
    return dict(
        d_att=d_att, d_conv=d_conv, n_experts=n_experts, d_expert=d_expert, lambda_init=lambda_init,
        gmix=norm_mix[i][None, :], w_in=w_in[i].astype(_BF16),
        gq_col=q_norm[i][:, None], gk_row=jnp.tile(k_norm[i], n_maps)[None, :], bd=bd,
        lam_vecs=tuple(v[i][None, :] for v in (lambda_q1, lambda_k1, lambda_q2, lambda_k2)),
        gsub_col=attn_sub_norm[i][:, None], conv_w=conv_w[i], w_out=w_out[i].astype(_BF16),
        gffn=norm_ffn[i][None, :], wr=jnp.concatenate([wr_hi, wr_lo], axis=1), br=br,
        wgu=wgu.astype(_BF16), wdn=wdn.astype(_BF16), wproj=w_ple_proj[i].astype(_BF16),
        wgate=w_ple_gate[i].astype(_BF16), gple=ple_norm[i][None, :])


def _score_bound(q_gain, k_gain, rel_bias):
    hd = _ATT_HEAD_DIM
    rounding_slack = 1.02
    q_norm_max = math.sqrt(hd) * jnp.max(jnp.abs(q_gain)) * (_LOG2E / math.sqrt(hd))
    k_norm_max = math.sqrt(hd) * jnp.max(jnp.abs(k_gain))
    qk = rounding_slack * q_norm_max * k_norm_max
    b2 = rel_bias.astype(_F32) * _LOG2E
    b_max, b_min = jnp.max(b2, axis=0), jnp.min(b2, axis=0)
    ok = jnp.all(2.0 * qk + (b_max - b_min) <= _SAFE_EXPONENT_SPAN)
    ok = ok & jnp.isfinite(qk) & jnp.all(jnp.isfinite(b2))
    return ok, jnp.where(ok, qk + b_max, jnp.zeros_like(b_max))


def _layer(x, p_i, L, bias_tiles, bound_ok, lam):
    B, S, D = x.shape
    tm = _TOKEN_TILE
    tile = _ATTN_TILE
    assert S % tm == 0 and S % tile == 0 and tm % _BF16_SUBLANES == 0
    qT, k, vT, b, u = _inproj(x, L["gmix"], L["w_in"], L["gq_col"], L["gk_row"], L["bd"],
                              d_att=L["d_att"], d_conv=L["d_conv"], tm=tm)
    oatt = _attention(bound_ok, lam, qT, k, vT, bias_tiles, L["gsub_col"], tile=tile,
                      out_scale=1.0 - L["lambda_init"])
    x1w, route = _outproj(oatt, b, u, x, L["conv_w"], L["w_out"], L["gffn"], L["wr"], L["br"],
                          n_experts=L["n_experts"], tm=tm)
    x1w = x1w.reshape(B * S, D + _LANES)
    group_of_token = route[:, 0, :].reshape(B * S).astype(jnp.int32)
    pos, tile_group, pads, sorted_rows = _dispatch_plan(group_of_token, L["wgu"].shape[0], tm)
    xw_sorted = _scatter_rows(pads, pos, x1w, sorted_rows, tm=tm)
    moe = _moe_sorted(tile_group, xw_sorted, L["gffn"], L["wgu"], L["wdn"], d_expert=L["d_expert"], tm=tm)
    y = _ple(pos, moe, x1w, p_i.reshape(B * S, -1), L["wproj"], L["wgate"], L["gple"], tm=tm)
    return y.reshape(B, S, D)


def kernel(x_prompt, x_sample, p_prompt, p_sample, norm_mix, w_in, q_norm, k_norm, lambda_q1, lambda_k1, lambda_q2, lambda_k2, attn_sub_norm, conv_w, w_out, rel_bias, norm_ffn, w_group, b_group, w_erouter, b_erouter, w_gate_up, w_down, w_ple_proj, w_ple_gate, ple_norm):
    depth = w_in.shape[0]
    layers = []
    for i in range(depth):
        L = _prepare_layer(i, norm_mix, w_in, q_norm, k_norm, lambda_q1, lambda_k1, lambda_q2,
                           lambda_k2, attn_sub_norm, conv_w, w_out, norm_ffn, w_group, b_group,
                           w_erouter, b_erouter, w_gate_up, w_down, w_ple_proj, w_ple_gate, ple_norm)
        bound_ok, shift = _score_bound(q_norm[i], k_norm[i], rel_bias)
        layers.append((L, _bias_tiles(rel_bias, shift, _ATTN_TILE), bound_ok,
                       _lambda(*L["lam_vecs"], L["lambda_init"])))

    def encode(x, p):
        for i, (L, bias_tiles, bound_ok, lam) in enumerate(layers):
            x = _layer(x, p[i], L, bias_tiles, bound_ok, lam)
        return x

    return (encode(x_prompt, p_prompt), encode(x_sample, p_sample))
```

```python
import functools
import math

import jax
import jax.numpy as jnp
from jax import lax
from jax.experimental import pallas as pl
from jax.experimental.pallas import tpu as pltpu

_F32 = jnp.float32
_BF16 = jnp.bfloat16

_EPS = 1e-6
_ATT_HEAD_DIM = 64
_MAX_DISTANCE = 128
_EXPERTS_PER_GROUP = 4
_LANES = 128
_F32_SUBLANES = 8
_BF16_SUBLANES = 16
_ATTN_TILE = 512
_ATTN_ROWS = 128
_ATTN_LANES = 256
_TOKEN_TILE = 512
_PLE_CHUNK = 128
_VMEM_LIMIT_BYTES = 56 * 1024 * 1024
_SAFE_EXPONENT_SPAN = 100.0
_NEG_BIG = -1e30
_LOG2E = math.log2(math.e)


def _cparams(*sem):
    return pltpu.CompilerParams(dimension_semantics=sem, vmem_limit_bytes=_VMEM_LIMIT_BYTES)


def _dot(a, b):
    return jnp.dot(a, b, preferred_element_type=_F32)


def _bias_tile_kernel(tbl_ref, shift_ref, out_ref, *, tile, num_buckets):
    di = pl.program_id(0)
    m = pl.program_id(1)
    delta = (di - 2) * tile
    kk = lax.broadcasted_iota(jnp.int32, (tile, tile), 0)
    qq = lax.broadcasted_iota(jnp.int32, (tile, tile), 1)
    rel = kk - qq + delta
    half = num_buckets // 2
    max_exact = half // 2
    ret = jnp.where(rel > 0, half, 0)
    n = jnp.abs(rel)
    nf = jnp.maximum(n, 1).astype(_F32)
    large = max_exact + (jnp.log(nf / max_exact) / math.log(_MAX_DISTANCE / max_exact)
                         * (half - max_exact)).astype(jnp.int32)
    large = jnp.minimum(large, half - 1)
    bucket = ret + jnp.where(n < max_exact, n, large)
    acc = jnp.zeros((tile, tile), _F32)
    for b in range(num_buckets):
        acc = jnp.where(bucket == b, tbl_ref[b, m], acc)
    out_ref[0, 0] = acc * _LOG2E - shift_ref[m]


def _bias_tiles(rel_bias, shift, tile):
    num_buckets, n_maps = rel_bias.shape
    assert tile >= _MAX_DISTANCE
    return pl.pallas_call(
        functools.partial(_bias_tile_kernel, tile=tile, num_buckets=num_buckets),
        grid=(5, n_maps),
        in_specs=[pl.BlockSpec(memory_space=pltpu.SMEM), pl.BlockSpec(memory_space=pltpu.SMEM)],
        out_specs=pl.BlockSpec((1, 1, tile, tile), lambda di, m: (di, m // 2, 0, m % 2)),
        out_shape=jax.ShapeDtypeStruct((5, n_maps // 2, tile, 2 * tile), _F32),
        compiler_params=_cparams("arbitrary", "arbitrary"),
        name="bias_tiles",
    )(rel_bias.astype(_F32), shift.astype(_F32))


def _inproj_kernel(x_ref, gmix_ref, w_ref, gq_ref, gk_ref, bd_ref,
                   qT_ref, k_ref, vT_ref, b_ref, u_ref, *, d_att, d_conv, scale):
    x = x_ref[0]
    ms = jnp.mean(x * x, axis=-1, keepdims=True)
    h = (x * lax.rsqrt(ms + _EPS) * gmix_ref[...]).astype(_BF16)

    def proj(lo, n):
        return _dot(h, w_ref[:, lo:lo + n])

    tm = x.shape[0]
    n_maps = d_att // _ATT_HEAD_DIM
    zqT = proj(0, d_att).T.reshape(n_maps, _ATT_HEAD_DIM, tm)
    qms = jnp.mean(zqT * zqT, axis=1, keepdims=True)
    qn = zqT * lax.rsqrt(qms + _EPS) * gq_ref[...][None]
    qT_ref[0] = (qn * scale).reshape(d_att, tm).astype(_BF16)
    zk = proj(d_att, d_att)
    kms = _dot((zk * zk).astype(_BF16), bd_ref[...])
    k_ref[0] = (zk * lax.rsqrt(kms + _EPS) * gk_ref[...]).astype(_BF16)
    vT_ref[0] = proj(2 * d_att, d_att).T.astype(_BF16)
    b_ref[0] = proj(3 * d_att, d_conv).astype(_BF16)
    c = proj(3 * d_att + d_conv, d_conv)
    xc = proj(3 * d_att + 2 * d_conv, d_conv)
    u_ref[0] = (c * xc).astype(_BF16)


def _inproj(x, gmix, w_in, gq_col, gk_row, bd, *, d_att, d_conv, tm):
    B, S, D = x.shape
    d_in = w_in.shape[1]
    grid = (B, S // tm)
    tok = lambda width: pl.BlockSpec((1, tm, width), lambda b, i: (b, i, 0))
    tr = lambda rows: pl.BlockSpec((1, rows, tm), lambda b, i: (b, 0, i))
    full = lambda shape: pl.BlockSpec(shape, lambda b, i: (0,) * len(shape))
    return pl.pallas_call(
        functools.partial(_inproj_kernel, d_att=d_att, d_conv=d_conv,
                          scale=_LOG2E / math.sqrt(_ATT_HEAD_DIM)),
        grid=grid,
        in_specs=[tok(D), full((1, D)), full((D, d_in)), full((_ATT_HEAD_DIM, 1)), full((1, d_att)),
                  full((d_att, d_att))],
        out_specs=[tr(d_att), tok(d_att), tr(d_att), tok(d_conv), tok(d_conv)],
        out_shape=[jax.ShapeDtypeStruct((B, d_att, S), _BF16),
                   jax.ShapeDtypeStruct((B, S, d_att), _BF16),
                   jax.ShapeDtypeStruct((B, d_att, S), _BF16),
                   jax.ShapeDtypeStruct((B, S, d_conv), _BF16),
                   jax.ShapeDtypeStruct((B, S, d_conv), _BF16)],
        compiler_params=_cparams("parallel", "parallel"),
        name="inproj",
    )(x, gmix, w_in, gq_col, gk_row, bd)


def _attention_kernel(lam_ref, qT_ref, k_ref, vT_ref, bias_ref, gsub_ref, o_ref,
                      rhs_ref, s0_ref, s1_ref, p0_ref, p1_ref, mx0_ref, mx1_ref, al0_ref, al1_ref,
                      m_ref, l_ref, acc_ref, *, tile, out_scale):
    qi = pl.program_id(2)
    n_k = k_ref.shape[1] // tile
    hd = _ATT_HEAD_DIM
    s_refs, p_refs = (s0_ref, s1_ref), (p0_ref, p1_ref)
    mx_refs, al_refs = (mx0_ref, mx1_ref), (al0_ref, al1_ref)
    qT = qT_ref[0]
    row = lax.broadcasted_iota(jnp.int32, qT.shape, 0)
    zero = jnp.zeros_like(qT)
    rhs_ref[:, :tile] = jnp.where(row < hd, qT, zero)
    rhs_ref[:, tile:] = jnp.where(row >= hd, qT, zero)
    m_ref[...] = jnp.full(m_ref.shape, _NEG_BIG, _F32)
    l_ref[...] = jnp.zeros(l_ref.shape, _F32)
    acc_ref[...] = jnp.zeros(acc_ref.shape, _F32)

    row_blocks = [slice(r, r + _ATTN_ROWS) for r in range(0, tile, _ATTN_ROWS)]
    lane_blocks = [slice(c, c + _ATTN_LANES) for c in range(0, 2 * tile, _ATTN_LANES)]

    def fold8(x, op):
        return op(x.reshape(x.shape[0] // 8, 8, x.shape[1]), axis=0)

    def stage_a(kj, par):
        off = pl.multiple_of(kj * tile, tile)
        bidx = jnp.clip(kj - qi, -2, 2) + 2
        for cs in lane_blocks:
            rhs_c = rhs_ref[:, cs]
            mx = None
            for rs in row_blocks:
                k_blk = k_ref[0, pl.ds(off + rs.start, _ATTN_ROWS), :]
                s = _dot(k_blk, rhs_c) + bias_ref[bidx, 0, rs, cs]
                s_refs[par][rs, cs] = s
                part = fold8(s, jnp.max)
                mx = part if mx is None else jnp.maximum(mx, part)
            mx_refs[par][:, cs] = jnp.max(mx, axis=0, keepdims=True)

    def stage_b(par):
        m_old = m_ref[...]
        m_new = jnp.maximum(m_old, mx_refs[par][...])
        alpha = jnp.exp2(m_old - m_new)
        al_refs[par][...] = alpha
        m_ref[...] = m_new
        for cs in lane_blocks:
            m_c = m_new[:, cs]
            lsum = None
            for rs in row_blocks:
                p = jnp.exp2(s_refs[par][rs, cs] - m_c)
                p_refs[par][rs, cs] = p.astype(_BF16)
                part = fold8(p, jnp.sum)
                lsum = part if lsum is None else lsum + part
            l_ref[:, cs] = alpha[:, cs] * l_ref[:, cs] + jnp.sum(lsum, axis=0, keepdims=True)

    def stage_c(kj, par):
        off = pl.multiple_of(kj * tile, tile)
        for cs in lane_blocks:
            pv = _dot(vT_ref[0, :, pl.ds(off, tile)], p_refs[par][:, cs])
            acc_ref[:, cs] = al_refs[par][:, cs] * acc_ref[:, cs] + pv

    stage_a(0, 0)
    stage_a(1, 1)
    stage_b(0)

    def pair(t, carry):
        j = 2 * t + 1
        stage_a(j + 1, 0)
        stage_b(1)
        stage_c(j - 1, 0)
        stage_a(j + 2, 1)
        stage_b(0)
        stage_c(j, 1)
        return carry

    lax.fori_loop(0, (n_k - 2) // 2, pair, 0)
    stage_b(1)
    stage_c(n_k - 2, 0)
    stage_c(n_k - 1, 1)

    o_ref[0, 0] = _attention_output(lam_ref, l_ref[...], acc_ref[...], gsub_ref, tile=tile,
                                    out_scale=out_scale)


def _attention_output(lam_ref, l, acc, gsub_ref, *, tile, out_scale):
    o = acc * (1.0 / l)
    oT = o[:, :tile] - lam_ref[0] * o[:, tile:]
    ms = jnp.mean(oT * oT, axis=0, keepdims=True)
    oT = oT * lax.rsqrt(ms + _EPS) * gsub_ref[...] * out_scale
    return oT.T.astype(_BF16)


def _attention_bounded_kernel(lam_ref, qT_ref, k_ref, vT_ref, bias_ref, gsub_ref, o_ref,
                              rhs_ref, p0_ref, p1_ref, l_ref, acc_ref, *, tile, out_scale):
    seq = k_ref.shape[1]
    n_k = seq // tile
    hd = _ATT_HEAD_DIM
    p_refs = (p0_ref, p1_ref)
    row_blocks = [slice(r, r + _ATTN_ROWS) for r in range(0, tile, _ATTN_ROWS)]
    lane_blocks = [slice(c, c + _ATTN_LANES) for c in range(0, 2 * tile, _ATTN_LANES)]

    def setup_q(qi, q_load):
        qpar = qi % 2
        qT = qT_ref[0, :, pl.ds(pl.multiple_of(q_load * tile, tile), tile)]
        row = lax.broadcasted_iota(jnp.int32, qT.shape, 0)
        zero = jnp.zeros_like(qT)
        rhs_ref[qpar, :, :tile] = jnp.where(row < hd, qT, zero)
        rhs_ref[qpar, :, tile:] = jnp.where(row >= hd, qT, zero)
        l_ref[qpar] = jnp.zeros(l_ref.shape[1:], _F32)

    def stage_ab(qi, kj, par):
        qpar = qi % 2
        off = pl.multiple_of(kj * tile, tile)
        bidx = jnp.clip(kj - qi, -2, 2) + 2
        k_tile = k_ref[0, pl.ds(off, tile), :]
        for cs in lane_blocks:
            s = _dot(k_tile, rhs_ref[qpar, :, cs])
            lsum = None
            for rs in row_blocks:
                p = jnp.exp2(s[rs, :] + bias_ref[bidx, 0, rs, cs])
                p_refs[par][rs, cs] = p.astype(_BF16)
                part = jnp.sum(p.reshape(_ATTN_ROWS // 8, 8, _ATTN_LANES), axis=0)
                lsum = part if lsum is None else lsum + part
            l_ref[qpar, :, cs] += jnp.sum(lsum, axis=0, keepdims=True)

    def stage_c(kj, par):
        off = pl.multiple_of(kj * tile, tile)
        for cs in lane_blocks:
            acc_ref[:, cs] += _dot(vT_ref[0, :, pl.ds(off, tile)], p_refs[par][:, cs])

    n_q = seq // tile
    acc_ref[...] = jnp.zeros(acc_ref.shape, _F32)
    setup_q(0, 0)
    stage_ab(0, 0, 0)

    def q_tile_body(qi, carry):
        def pair(t, c):
            j = 2 * t
            stage_ab(qi, j + 1, 1)
            stage_c(j, 0)
            stage_ab(qi, j + 2, 0)
            stage_c(j + 1, 1)
            return c

        lax.fori_loop(0, n_k // 2 - 1, pair, 0)
        stage_ab(qi, n_k - 1, 1)
        stage_c(n_k - 2, 0)
        q_next = qi + 1
        setup_q(q_next, jnp.minimum(q_next, n_q - 1))
        stage_ab(q_next, 0, 0)
        stage_c(n_k - 1, 1)
        o_ref[0, 0, pl.ds(pl.multiple_of(qi * tile, tile), tile), :] = _attention_output(
            lam_ref, l_ref[qi % 2], acc_ref[...], gsub_ref, tile=tile, out_scale=out_scale)
        acc_ref[...] = jnp.zeros(acc_ref.shape, _F32)
        return carry

    lax.fori_loop(0, n_q, q_tile_body, 0)


def _lambda_kernel(q1_ref, k1_ref, q2_ref, k2_ref, out_ref, *, lambda_init):
    a = jnp.sum(q1_ref[...] * k1_ref[...], axis=-1, keepdims=True)
    b = jnp.sum(q2_ref[...] * k2_ref[...], axis=-1, keepdims=True)
    out_ref[...] = jnp.broadcast_to(jnp.exp(a) - jnp.exp(b) + lambda_init, out_ref.shape)


def _lambda(q1, k1, q2, k2, lambda_init):
    out = pl.pallas_call(
        functools.partial(_lambda_kernel, lambda_init=lambda_init),
        out_shape=jax.ShapeDtypeStruct((1, _LANES), _F32),
        name="lambda_scalar",
    )(q1, k1, q2, k2)
    return out[0, :1]


def _attention(bound_ok, lam, qT, k, vT, bias_tiles, gsub_col, *, tile, out_scale):
    B, d_att, S = qT.shape
    v_dim = 2 * _ATT_HEAD_DIM
    n_heads = d_att // v_dim
    assert S % (2 * tile) == 0
    row = lambda: pltpu.VMEM((1, 2 * tile), _F32)
    s_buf = lambda: pltpu.VMEM((tile, 2 * tile), _F32)
    p_buf = lambda: pltpu.VMEM((tile, 2 * tile), _BF16)
    rhs = pltpu.VMEM((v_dim, 2 * tile), _BF16)
    acc = pltpu.VMEM((v_dim, 2 * tile), _F32)

    out_shape = jax.ShapeDtypeStruct((B, n_heads, S, v_dim), _BF16)
    operands = (lam, qT, k, vT, bias_tiles, gsub_col)

    def bounded():
        return pl.pallas_call(
            functools.partial(_attention_bounded_kernel, tile=tile, out_scale=out_scale),
            grid=(n_heads, B),
            in_specs=[pl.BlockSpec(memory_space=pltpu.SMEM),
                      pl.BlockSpec((1, v_dim, S), lambda h, b: (b, h, 0)),
                      pl.BlockSpec((1, S, v_dim), lambda h, b: (b, 0, h)),
                      pl.BlockSpec((1, v_dim, S), lambda h, b: (b, h, 0)),
                      pl.BlockSpec((5, 1, tile, 2 * tile), lambda h, b: (0, h, 0, 0)),
                      pl.BlockSpec((v_dim, 1), lambda h, b: (0, 0))],
            out_specs=pl.BlockSpec((1, 1, S, v_dim), lambda h, b: (b, h, 0, 0)),
            out_shape=out_shape,
            scratch_shapes=[pltpu.VMEM((2, v_dim, 2 * tile), _BF16), p_buf(), p_buf(),
                            pltpu.VMEM((2, 1, 2 * tile), _F32), acc],
            compiler_params=_cparams("parallel", "parallel"),
            name="diff_attention_bounded",
        )(*operands)

    def exact():
        return pl.pallas_call(
            functools.partial(_attention_kernel, tile=tile, out_scale=out_scale),
            grid=(n_heads, B, S // tile),
            in_specs=[pl.BlockSpec(memory_space=pltpu.SMEM),
                      pl.BlockSpec((1, v_dim, tile), lambda h, b, i: (b, h, i)),
                      pl.BlockSpec((1, S, v_dim), lambda h, b, i: (b, 0, h)),
                      pl.BlockSpec((1, v_dim, S), lambda h, b, i: (b, h, 0)),
                      pl.BlockSpec((5, 1, tile, 2 * tile), lambda h, b, i: (0, h, 0, 0)),
                      pl.BlockSpec((v_dim, 1), lambda h, b, i: (0, 0))],
            out_specs=pl.BlockSpec((1, 1, tile, v_dim), lambda h, b, i: (b, h, i, 0)),
            out_shape=out_shape,
            scratch_shapes=[rhs, s_buf(), s_buf(), p_buf(), p_buf(), row(), row(), row(), row(), row(),
                            row(), acc],
            compiler_params=_cparams("parallel", "parallel", "parallel"),
            name="diff_attention",
        )(*operands)

    return lax.cond(bound_ok, bounded, exact)


def _outproj_kernel(oatt_ref, b_ref, u_ref, uprev_ref, unext_ref, x_ref, cw_ref, wout_ref, gffn_ref,
                    wr_ref, br_ref, x1w_ref, route_ref, *, d_att, n_experts):
    i = pl.program_id(1)
    n_i = pl.num_programs(1)
    u = u_ref[0].astype(_F32)
    tm = u.shape[0]
    prev_row = jnp.where(i > 0, uprev_ref[0, _BF16_SUBLANES - 1:_BF16_SUBLANES, :].astype(_F32), 0.0)
    next_row = jnp.where(i < n_i - 1, unext_ref[0, 0:1, :].astype(_F32), 0.0)
    rows = lax.broadcasted_iota(jnp.int32, (tm, 1), 0)
    u_prev = jnp.where(rows == 0, prev_row, pltpu.roll(u, 1, axis=0))
    u_next = jnp.where(rows == tm - 1, next_row, pltpu.roll(u, tm - 1, axis=0))
    conv = cw_ref[0:1, :] * u_prev + cw_ref[1:2, :] * u + cw_ref[2:3, :] * u_next
    o_conv = (b_ref[0].astype(_F32) * conv).astype(_BF16)
    o_att = jnp.concatenate([oatt_ref[0, h] for h in range(oatt_ref.shape[1])], axis=1)
    mix = _dot(o_att, wout_ref[:d_att, :]) + _dot(o_conv, wout_ref[d_att:, :])
    x1 = x_ref[0] + mix
    d_model = x1.shape[1]
    x1w_ref[0, :, :d_model] = x1
    ms = jnp.mean(x1 * x1, axis=-1, keepdims=True)
    t = x1 * lax.rsqrt(ms + _EPS) * gffn_ref[...]
    t_hi = t.astype(_BF16)
    t_lo = (t - t_hi.astype(_F32)).astype(_BF16)
    a_hi = _dot(t_hi, wr_ref[...])
    a_lo = _dot(t_lo, wr_ref[...])
    logits = a_hi[:, :_LANES] + a_hi[:, _LANES:] + a_lo[:, :_LANES] + br_ref[...]
    n_groups = n_experts // _EXPERTS_PER_GROUP
    lane = lax.broadcasted_iota(jnp.int32, (1, _LANES), 1)
    lane_f = lane.astype(_F32)
    big = float(_LANES)
    gmask = (lane >= n_experts) & (lane < n_experts + n_groups)
    gl = jnp.where(gmask, logits, _NEG_BIG)
    gmax = jnp.max(gl, axis=-1, keepdims=True)
    gsum = jnp.sum(jnp.where(gmask, jnp.exp(gl - gmax), 0.0), axis=-1, keepdims=True)
    g_w = 1.0 / gsum
    g_idx = jnp.min(jnp.where(gmask & (gl == gmax), lane_f - n_experts, big), axis=-1, keepdims=True)
    lo = g_idx * _EXPERTS_PER_GROUP
    emask = (lane_f >= lo) & (lane_f < lo + _EXPERTS_PER_GROUP)
    el = jnp.where(emask, logits, _NEG_BIG)
    emax = jnp.max(el, axis=-1, keepdims=True)
    ep = jnp.where(emask, jnp.exp(el - emax), 0.0)
    p_exp = ep / jnp.sum(ep, axis=-1, keepdims=True)
    top1 = jnp.max(p_exp, axis=-1, keepdims=True)
    i1 = jnp.min(jnp.where(emask & (p_exp == top1), lane_f, big), axis=-1, keepdims=True)
    rest = jnp.where(emask & (lane_f != i1), p_exp, -1.0)
    top2 = jnp.max(rest, axis=-1, keepdims=True)
    i2 = jnp.min(jnp.where(rest == top2, lane_f, big), axis=-1, keepdims=True)
    denom = top1 + top2
    x1w_ref[0, :, d_model:] = jnp.where(lane_f == i1 - lo, g_w * (top1 / denom),
                                        jnp.where(lane_f == i2 - lo, g_w * (top2 / denom), 0.0))
    route_ref[0] = jnp.broadcast_to(g_idx, (tm, _LANES)).T[:_F32_SUBLANES, :]


def _outproj(oatt, b, u, x, conv_w, w_out, gffn, wr, br, *, n_experts, tm):
    B, S, D = x.shape
    n_heads, v_dim = oatt.shape[1], oatt.shape[3]
    d_att = n_heads * v_dim
    d_conv = b.shape[-1]
    hb = _BF16_SUBLANES
    per_tile = tm // hb
    n_halo = S // hb
    tok = lambda width: pl.BlockSpec((1, tm, width), lambda bb, i: (bb, i, 0))
    full = lambda shape: pl.BlockSpec(shape, lambda bb, i: (0,) * len(shape))
    return pl.pallas_call(
        functools.partial(_outproj_kernel, d_att=d_att, n_experts=n_experts),
        grid=(B, S // tm),
        in_specs=[pl.BlockSpec((1, n_heads, tm, v_dim), lambda bb, i: (bb, 0, i, 0)), tok(d_conv), tok(d_conv),
                  pl.BlockSpec((1, hb, d_conv), lambda bb, i: (bb, jnp.maximum(i * per_tile - 1, 0), 0)),
                  pl.BlockSpec((1, hb, d_conv),
                               lambda bb, i: (bb, jnp.minimum((i + 1) * per_tile, n_halo - 1), 0)),
                  tok(D), full(conv_w.shape), full(w_out.shape), full((1, D)), full(wr.shape),
                  full((1, _LANES))],
        out_specs=[tok(D + _LANES), pl.BlockSpec((1, _F32_SUBLANES, tm), lambda bb, i: (bb, 0, i))],
        out_shape=[jax.ShapeDtypeStruct((B, S, D + _LANES), _F32),
                   jax.ShapeDtypeStruct((B, _F32_SUBLANES, S), _F32)],
        compiler_params=_cparams("parallel", "parallel"),
        name="outproj_router",
    )(oatt, b, u, u, u, x, conv_w, w_out, gffn, wr, br)


def _row_copies_wait(src_rows, dst_rows, sem):
    pltpu.make_async_copy(src_rows, dst_rows, sem).wait()


def _stage_indices(idx_ref, idx_smem, slot, idx_sem):
    c = pltpu.make_async_copy(idx_ref.at[0, 0], idx_smem.at[slot], idx_sem)
    c.start()
    c.wait()


def _scatter_rows_kernel(pads_ref, pos_ref, x_ref, out_hbm, idx_smem, idx_sem, sem, *, n_groups):
    i = pl.program_id(0)
    tm = x_ref.shape[0]
    _stage_indices(pos_ref, idx_smem, 0, idx_sem)
    for r in range(tm):
        pltpu.make_async_copy(x_ref.at[pl.ds(r, 1), :], out_hbm.at[pl.ds(idx_smem[0, r], 1), :],
                              sem).start()
    _row_copies_wait(x_ref, out_hbm.at[pl.ds(0, tm), :], sem)

    @pl.when(i == pl.num_programs(0) - 1)
    def _():
        for g in range(n_groups):
            lo = pads_ref[g]
            count = pads_ref[n_groups + g] - lo

            def fill(k, carry, lo=lo):
                pltpu.make_async_copy(x_ref.at[pl.ds(0, 1), :], out_hbm.at[pl.ds(lo + k, 1), :], sem).start()
                return carry

            def drain(k, carry):
                _row_copies_wait(x_ref.at[pl.ds(0, 1), :], out_hbm.at[pl.ds(0, 1), :], sem)
                return carry

            lax.fori_loop(0, count, fill, 0)
            lax.fori_loop(0, count, drain, 0)


def _scatter_rows(pads, pos, x, n_out_rows, *, tm):
    T, width = x.shape
    n_steps = T // tm
    return pl.pallas_call(
        functools.partial(_scatter_rows_kernel, n_groups=pads.shape[0] // 2),
        grid_spec=pltpu.PrefetchScalarGridSpec(
            num_scalar_prefetch=1,
            grid=(n_steps,),
            in_specs=[pl.BlockSpec((1, 1, tm), lambda i, pads: (i, 0, 0)),
                      pl.BlockSpec((tm, width), lambda i, pads: (i, 0))],
            out_specs=pl.BlockSpec(memory_space=pl.ANY),
            scratch_shapes=[pltpu.SMEM((1, tm), jnp.int32), pltpu.SemaphoreType.DMA,
                            pltpu.SemaphoreType.DMA]),
        out_shape=jax.ShapeDtypeStruct((n_out_rows, width), x.dtype),
        compiler_params=_cparams("arbitrary"),
        name="dispatch_rows",
    )(pads, pos.reshape(n_steps, 1, tm), x)


def _moe_sorted_kernel(tile_group_ref, xw_ref, gffn_ref, wgu_ref, wdn_ref, out_ref, *, d_expert):
    del tile_group_ref
    d_model = out_ref.shape[1]
    x1 = xw_ref[:, :d_model]
    gates = xw_ref[:, d_model:]
    ms = jnp.mean(x1 * x1, axis=-1, keepdims=True)
    t = (x1 * lax.rsqrt(ms + _EPS) * gffn_ref[...]).astype(_BF16)
    width = _EXPERTS_PER_GROUP * d_expert
    gu = _dot(t, wgu_ref[0])
    g_lin = gu[:, :width]
    act = g_lin * jax.nn.sigmoid(g_lin) * gu[:, width:]
    parts = [(act[:, e * d_expert:(e + 1) * d_expert] * gates[:, e:e + 1]).astype(_BF16)
             for e in range(_EXPERTS_PER_GROUP)]
    out_ref[...] = _dot(jnp.concatenate(parts, axis=1), wdn_ref[0])


def _moe_sorted(tile_group, xw_sorted, gffn, wgu, wdn, *, d_expert, tm):
    rows, width = xw_sorted.shape
    D = width - _LANES
    full = lambda shape: pl.BlockSpec(shape, lambda i, tg: (0,) * len(shape))
    grp = lambda shape: pl.BlockSpec((1,) + shape, lambda i, tg: (tg[i], 0, 0))
    return pl.pallas_call(
        functools.partial(_moe_sorted_kernel, d_expert=d_expert),
        grid_spec=pltpu.PrefetchScalarGridSpec(
            num_scalar_prefetch=1,
            grid=(rows // tm,),
            in_specs=[pl.BlockSpec((tm, width), lambda i, tg: (i, 0)), full((1, D)),
                      grp(wgu.shape[1:]), grp(wdn.shape[1:])],
            out_specs=pl.BlockSpec((tm, D), lambda i, tg: (i, 0))),
        out_shape=jax.ShapeDtypeStruct((rows, D), _F32),
        compiler_params=_cparams("parallel"),
        name="moe_sorted",
    )(tile_group, xw_sorted, gffn, wgu, wdn)


def _ple_kernel(idx_first_ref, idx_b_ref, idx_a_ref, moe_hbm, x1_ref, p_ref, wproj_ref, wgate_ref,
                gple_ref, y_ref, idx_smem, idx_sem, buf, sem):
    j = pl.program_id(0)
    tm = buf.shape[1]

    def rows_start(idx_slot, slot, lo=0, hi=None):
        for r in range(lo, tm if hi is None else hi):
            pltpu.make_async_copy(moe_hbm.at[pl.ds(idx_smem[idx_slot, r], 1), :],
                                  buf.at[slot, pl.ds(r, 1), :], sem.at[slot]).start()

    def rows_wait(slot):
        _row_copies_wait(moe_hbm.at[pl.ds(0, tm), :], buf.at[slot], sem.at[slot])

    def compute_rows(base, slot, lo, hi):
        rows = slice(base + lo, base + hi)
        x2 = x1_ref[rows, :] + buf[slot, lo:hi, :]
        e_raw = _dot(p_ref[rows, :].astype(_BF16), wproj_ref[...])
        ms = jnp.mean(e_raw * e_raw, axis=-1, keepdims=True)
        emb = e_raw * lax.rsqrt(ms + _EPS) * gple_ref[...]
        gate_p = jax.nn.sigmoid(_dot(x2.astype(_BF16), wgate_ref[...]))
        y_ref[rows, :] = x2 + gate_p * emb

    def compute_and_fetch(base, slot, idx_slot, other):
        for lo in range(0, tm, _PLE_CHUNK):
            rows_start(idx_slot, other, lo, lo + _PLE_CHUNK)
            compute_rows(base, slot, lo, lo + _PLE_CHUNK)

    @pl.when(j == 0)
    def _():
        _stage_indices(idx_first_ref, idx_smem, 0, idx_sem)
        rows_start(0, 0)

    _stage_indices(idx_b_ref, idx_smem, 0, idx_sem)
    _stage_indices(idx_a_ref, idx_smem, 1, idx_sem)
    rows_wait(0)
    compute_and_fetch(0, 0, 0, 1)
    rows_wait(1)
    compute_and_fetch(tm, 1, 1, 0)

    @pl.when(j == pl.num_programs(0) - 1)
    def _():
        rows_wait(0)


def _ple(pos, moe_sorted, x1w, p, wproj, wgate, gple, *, tm):
    T = x1w.shape[0]
    D = moe_sorted.shape[1]
    n_tiles = T // tm
    assert n_tiles % 2 == 0
    idx = pos.reshape(n_tiles, 1, tm)
    tok = lambda width: pl.BlockSpec((2 * tm, width), lambda j: (j, 0))
    full = lambda shape: pl.BlockSpec(shape, lambda j: (0,) * len(shape))
    idx_spec = lambda tile_of_step: pl.BlockSpec((1, 1, tm), lambda j: (tile_of_step(j), 0, 0))
    return pl.pallas_call(
        _ple_kernel,
        grid=(n_tiles // 2,),
        in_specs=[idx_spec(lambda j: 0), idx_spec(lambda j: 2 * j + 1),
                  idx_spec(lambda j: jnp.minimum(2 * j + 2, n_tiles - 1)),
                  pl.BlockSpec(memory_space=pl.ANY), tok(D), tok(p.shape[-1]), full(wproj.shape),
                  full(wgate.shape), full((1, D))],
        out_specs=tok(D),
        out_shape=jax.ShapeDtypeStruct((T, D), _F32),
        scratch_shapes=[pltpu.SMEM((2, tm), jnp.int32), pltpu.SemaphoreType.DMA,
                        pltpu.VMEM((2, tm, D), _F32), pltpu.SemaphoreType.DMA((2,))],
        compiler_params=_cparams("arbitrary"),
        name="ple_gate",
    )(idx, idx, idx, moe_sorted, x1w, p, wproj, wgate, gple)


def _dispatch_plan(group_of_token, n_groups, tm):
    T = group_of_token.shape[0]
    n_tiles = T // tm + n_groups - 1
    onehot = (group_of_token[:, None] == jnp.arange(n_groups, dtype=jnp.int32)[None, :]).astype(jnp.int32)
    rank = jnp.take_along_axis(jnp.cumsum(onehot, axis=0), group_of_token[:, None], axis=1)[:, 0] - 1
    count = jnp.sum(onehot, axis=0)
    tiles_per_group = (count + tm - 1) // tm
    tile_end = jnp.cumsum(tiles_per_group)
    tile_start = tile_end - tiles_per_group
    pos = tile_start[group_of_token] * tm + rank
    tile_group = jnp.searchsorted(tile_end, jnp.arange(n_tiles, dtype=jnp.int32), side="right")
    tile_group = jnp.minimum(tile_group, n_groups - 1).astype(jnp.int32)
    pad_lo = tile_start * tm + count
    pad_hi = (tile_end * tm).at[n_groups - 1].set(n_tiles * tm)
    pads = jnp.concatenate([pad_lo, pad_hi]).astype(jnp.int32)
    return pos.astype(jnp.int32), tile_group, pads, n_tiles * tm


def _prepare_layer(i, norm_mix, w_in, q_norm, k_norm, lambda_q1, lambda_k1, lambda_q2, lambda_k2,
                   attn_sub_norm, conv_w, w_out, norm_ffn, w_group, b_group, w_erouter, b_erouter,
                   w_gate_up, w_down, w_ple_proj, w_ple_gate, ple_norm):
    D = w_in.shape[1]
    d_mix = w_out.shape[1]
    d_att = d_mix // 2
    d_conv = d_mix - d_att
    n_maps = d_att // _ATT_HEAD_DIM
    n_experts = w_gate_up.shape[1]
    n_groups = w_group.shape[-1]
    d_expert = w_down.shape[2]
    assert n_experts == n_groups * _EXPERTS_PER_GROUP and n_experts + n_groups <= _LANES
    lambda_init = 0.8 - 0.6 * math.exp(-0.3 * i)
    head_of = jnp.arange(d_att) // _ATT_HEAD_DIM
    bd = jnp.where(head_of[:, None] == head_of[None, :], 1.0 / _ATT_HEAD_DIM, 0.0).astype(_BF16)
    wr = jnp.zeros((D, _LANES), _F32)
    wr = wr.at[:, :n_experts].set(w_erouter[i]).at[:, n_experts:n_experts + n_groups].set(w_group[i])
    wr_hi = wr.astype(_BF16)
    wr_lo = (wr - wr_hi.astype(_F32)).astype(_BF16)
    br = jnp.zeros((1, _LANES), _F32)
    br = br.at[0, :n_experts].set(b_erouter[i]).at[0, n_experts:n_experts + n_groups].set(b_group[i])
    wg = w_gate_up[i].reshape(n_groups, _EXPERTS_PER_GROUP, D, 2, d_expert)
    wgu = jnp.transpose(wg, (0, 2, 3, 1, 4)).reshape(n_groups, D, 2 * _EXPERTS_PER_GROUP * d_expert)
    wdn = w_down[i].reshape(n_groups, _EXPERTS_PER_GROUP * d_expert, D)
    return dict(
        d_att=d_att, d_conv=d_conv, n_experts=n_experts, d_expert=d_expert, lambda_init=lambda_init,
        gmix=norm_mix[i][None, :], w_in=w_in[i].astype(_BF16),
        gq_col=q_norm[i][:, None], gk_row=jnp.tile(k_norm[i], n_maps)[None, :], bd=bd,
        lam_vecs=tuple(v[i][None, :] for v in (lambda_q1, lambda_k1, lambda_q2, lambda_k2)),
        gsub_col=attn_sub_norm[i][:, None], conv_w=conv_w[i], w_out=w_out[i].astype(_BF16),
        gffn=norm_ffn[i][None, :], wr=jnp.concatenate([wr_hi, wr_lo], axis=1), br=br,
        wgu=wgu.astype(_BF16), wdn=wdn.astype(_BF16), wproj=w_ple_proj[i].astype(_BF16),
        wgate=w_ple_gate[i].astype(_BF16), gple=ple_norm[i][None, :])


def _score_bound(q_gain, k_gain, rel_bias):
    hd = _ATT_HEAD_DIM
    rounding_slack = 1.02
    q_norm_max = math.sqrt(hd) * jnp.max(jnp.abs(q_gain)) * (_LOG2E / math.sqrt(hd))
    k_norm_max = math.sqrt(hd) * jnp.max(jnp.abs(k_gain))
    qk = rounding_slack * q_norm_max * k_norm_max
    b2 = rel_bias.astype(_F32) * _LOG2E
    b_max, b_min = jnp.max(b2, axis=0), jnp.min(b2, axis=0)
    ok = jnp.all(2.0 * qk + (b_max - b_min) <= _SAFE_EXPONENT_SPAN)
    ok = ok & jnp.isfinite(qk) & jnp.all(jnp.isfinite(b2))
    return ok, jnp.where(ok, qk + b_max, jnp.zeros_like(b_max))


def _layer(x, p_i, L, bias_tiles, bound_ok, lam):
    B, S, D = x.shape
    tm = _TOKEN_TILE
    tile = _ATTN_TILE
    assert S % tm == 0 and S % tile == 0 and tm % _BF16_SUBLANES == 0
    qT, k, vT, b, u = _inproj(x, L["gmix"], L["w_in"], L["gq_col"], L["gk_row"], L["bd"],
                              d_att=L["d_att"], d_conv=L["d_conv"], tm=tm)
    oatt = _attention(bound_ok, lam, qT, k, vT, bias_tiles, L["gsub_col"], tile=tile,
                      out_scale=1.0 - L["lambda_init"])
    x1w, route = _outproj(oatt, b, u, x, L["conv_w"], L["w_out"], L["gffn"], L["wr"], L["br"],
                          n_experts=L["n_experts"], tm=tm)
    x1w = x1w.reshape(B * S, D + _LANES)
    group_of_token = route[:, 0, :].reshape(B * S).astype(jnp.int32)
    pos, tile_group, pads, sorted_rows = _dispatch_plan(group_of_token, L["wgu"].shape[0], tm)
    xw_sorted = _scatter_rows(pads, pos, x1w, sorted_rows, tm=tm)
    moe = _moe_sorted(tile_group, xw_sorted, L["gffn"], L["wgu"], L["wdn"], d_expert=L["d_expert"], tm=tm)
    y = _ple(pos, moe, x1w, p_i.reshape(B * S, -1), L["wproj"], L["wgate"], L["gple"], tm=tm)
    return y.reshape(B, S, D)


def kernel(x_prompt, x_sample, p_prompt, p_sample, norm_mix, w_in, q_norm, k_norm, lambda_q1, lambda_k1, lambda_q2, lambda_k2, attn_sub_norm, conv_w, w_out, rel_bias, norm_ffn, w_group, b_group, w_erouter, b_erouter, w_gate_up, w_down, w_ple_proj, w_ple_gate, ple_norm):
    depth = w_in.shape[0]
    layers = []
    for i in range(depth):
        L = _prepare_layer(i, norm_mix, w_in, q_norm, k_norm, lambda_q1, lambda_k1, lambda_q2,
                           lambda_k2, attn_sub_norm, conv_w, w_out, norm_ffn, w_group, b_group,
                           w_erouter, b_erouter, w_gate_up, w_down, w_ple_proj, w_ple_gate, ple_norm)
        bound_ok, shift = _score_bound(q_norm[i], k_norm[i], rel_bias)
        layers.append((L, _bias_tiles(rel_bias, shift, _ATTN_TILE), bound_ok,
                       _lambda(*L["lam_vecs"], L["lambda_init"])))

    def encode(x, p):
        for i, (L, bias_tiles, bound_ok, lam) in enumerate(layers):
            x = _layer(x, p[i], L, bias_tiles, bound_ok, lam)
        return x

    return (encode(x_prompt, p_prompt), encode(x_sample, p_sample))
```

```python
import functools
import math

import jax
import jax.numpy as jnp
from jax import lax
from jax.experimental import pallas as pl
from jax.experimental.pallas import tpu as pltpu

_F32 = jnp.float32
_BF16 = jnp.bfloat16

_EPS = 1e-6
_ATT_HEAD_DIM = 64
_MAX_DISTANCE = 128
_EXPERTS_PER_GROUP = 4
_LANES = 128
_F32_SUBLANES = 8
_BF16_SUBLANES = 16
_ATTN_TILE = 512
_ATTN_ROWS = 128
_ATTN_LANES = 256
_TOKEN_TILE = 512
_PLE_CHUNK = 128
_VMEM_LIMIT_BYTES = 56 * 1024 * 1024
_SAFE_EXPONENT_SPAN = 100.0
_NEG_BIG = -1e30
_LOG2E = math.log2(math.e)


def _cparams(*sem):
    return pltpu.CompilerParams(dimension_semantics=sem, vmem_limit_bytes=_VMEM_LIMIT_BYTES)


def _dot(a, b):
    return jnp.dot(a, b, preferred_element_type=_F32)


def _pack_bf16_pairs(x):
    n = x.shape[1] // 2
    bits = lax.bitcast_convert_type(x.astype(_BF16).astype(_F32), jnp.uint32)
    return (bits[:, :n] & jnp.uint32(0xFFFF0000)) | (bits[:, n:] >> 16)


def _unpack_bf16_pairs(u):
    hi = lax.bitcast_convert_type(u & jnp.uint32(0xFFFF0000), _F32)
    lo = lax.bitcast_convert_type(u << 16, _F32)
    return jnp.concatenate([hi, lo], axis=1)


def _bias_tile_kernel(tbl_ref, shift_ref, out_ref, *, tile, num_buckets):
    di = pl.program_id(0)
    m = pl.program_id(1)
    delta = (di - 2) * tile
    kk = lax.broadcasted_iota(jnp.int32, (tile, tile), 0)
    qq = lax.broadcasted_iota(jnp.int32, (tile, tile), 1)
    rel = kk - qq + delta
    half = num_buckets // 2
    max_exact = half // 2
    ret = jnp.where(rel > 0, half, 0)
    n = jnp.abs(rel)
    nf = jnp.maximum(n, 1).astype(_F32)
    large = max_exact + (jnp.log(nf / max_exact) / math.log(_MAX_DISTANCE / max_exact)
                         * (half - max_exact)).astype(jnp.int32)
    large = jnp.minimum(large, half - 1)
    bucket = ret + jnp.where(n < max_exact, n, large)
    acc = jnp.zeros((tile, tile), _F32)
    for b in range(num_buckets):
        acc = jnp.where(bucket == b, tbl_ref[b, m], acc)
    out_ref[0, 0] = acc * _LOG2E - shift_ref[m]


def _bias_tiles(rel_bias, shift, tile):
    num_buckets, n_maps = rel_bias.shape
    assert tile >= _MAX_DISTANCE
    return pl.pallas_call(
        functools.partial(_bias_tile_kernel, tile=tile, num_buckets=num_buckets),
        grid=(5, n_maps),
        in_specs=[pl.BlockSpec(memory_space=pltpu.SMEM), pl.BlockSpec(memory_space=pltpu.SMEM)],
        out_specs=pl.BlockSpec((1, 1, tile, tile), lambda di, m: (di, m // 2, 0, m % 2)),
        out_shape=jax.ShapeDtypeStruct((5, n_maps // 2, tile, 2 * tile), _F32),
        compiler_params=_cparams("arbitrary", "arbitrary"),
        name="bias_tiles",
    )(rel_bias.astype(_F32), shift.astype(_F32))


def _inproj_kernel(x_ref, gmix_ref, w_ref, gq_ref, gk_ref, bd_ref,
                   qT_ref, k_ref, vT_ref, b_ref, u_ref, *, d_att, d_conv, scale):
    x = x_ref[0]
    ms = jnp.mean(x * x, axis=-1, keepdims=True)
    h = (x * lax.rsqrt(ms + _EPS) * gmix_ref[...]).astype(_BF16)

    def proj(lo, n):
        return _dot(h, w_ref[:, lo:lo + n])

    tm = x.shape[0]
    n_maps = d_att // _ATT_HEAD_DIM
    zqT = proj(0, d_att).T.reshape(n_maps, _ATT_HEAD_DIM, tm)
    qms = jnp.mean(zqT * zqT, axis=1, keepdims=True)
    qn = zqT * lax.rsqrt(qms + _EPS) * gq_ref[...][None]
    qT_ref[0] = (qn * scale).reshape(d_att, tm).astype(_BF16)
    zk = proj(d_att, d_att)
    kms = _dot((zk * zk).astype(_BF16), bd_ref[...])
    k_ref[0] = (zk * lax.rsqrt(kms + _EPS) * gk_ref[...]).astype(_BF16)
    vT_ref[0] = proj(2 * d_att, d_att).T.astype(_BF16)
    b_ref[0] = proj(3 * d_att, d_conv).astype(_BF16)
    c = proj(3 * d_att + d_conv, d_conv)
    xc = proj(3 * d_att + 2 * d_conv, d_conv)
    u_ref[0] = (c * xc).astype(_BF16)


def _inproj(x, gmix, w_in, gq_col, gk_row, bd, *, d_att, d_conv, tm):
    B, S, D = x.shape
    d_in = w_in.shape[1]
    grid = (B, S // tm)
    tok = lambda width: pl.BlockSpec((1, tm, width), lambda b, i: (b, i, 0))
    tr = lambda rows: pl.BlockSpec((1, rows, tm), lambda b, i: (b, 0, i))
    full = lambda shape: pl.BlockSpec(shape, lambda b, i: (0,) * len(shape))
    return pl.pallas_call(
        functools.partial(_inproj_kernel, d_att=d_att, d_conv=d_conv,
                          scale=_LOG2E / math.sqrt(_ATT_HEAD_DIM)),
        grid=grid,
        in_specs=[tok(D), full((1, D)), full((D, d_in)), full((_ATT_HEAD_DIM, 1)), full((1, d_att)),
                  full((d_att, d_att))],
        out_specs=[tr(d_att), tok(d_att), tr(d_att), tok(d_conv), tok(d_conv)],
        out_shape=[jax.ShapeDtypeStruct((B, d_att, S), _BF16),
                   jax.ShapeDtypeStruct((B, S, d_att), _BF16),
                   jax.ShapeDtypeStruct((B, d_att, S), _BF16),
                   jax.ShapeDtypeStruct((B, S, d_conv), _BF16),
                   jax.ShapeDtypeStruct((B, S, d_conv), _BF16)],
        compiler_params=_cparams("parallel", "parallel"),
        name="inproj",
    )(x, gmix, w_in, gq_col, gk_row, bd)


def _attention_kernel(lam_ref, qT_ref, k_ref, vT_ref, bias_ref, gsub_ref, o_ref,
                      rhs_ref, s0_ref, s1_ref, p0_ref, p1_ref, mx0_ref, mx1_ref, al0_ref, al1_ref,
                      m_ref, l_ref, acc_ref, *, tile, out_scale):
    qi = pl.program_id(2)
    n_k = k_ref.shape[1] // tile
    hd = _ATT_HEAD_DIM
    s_refs, p_refs = (s0_ref, s1_ref), (p0_ref, p1_ref)
    mx_refs, al_refs = (mx0_ref, mx1_ref), (al0_ref, al1_ref)
    qT = qT_ref[0]
    row = lax.broadcasted_iota(jnp.int32, qT.shape, 0)
    zero = jnp.zeros_like(qT)
    rhs_ref[:, :tile] = jnp.where(row < hd, qT, zero)
    rhs_ref[:, tile:] = jnp.where(row >= hd, qT, zero)
    m_ref[...] = jnp.full(m_ref.shape, _NEG_BIG, _F32)
    l_ref[...] = jnp.zeros(l_ref.shape, _F32)
    acc_ref[...] = jnp.zeros(acc_ref.shape, _F32)

    row_blocks = [slice(r, r + _ATTN_ROWS) for r in range(0, tile, _ATTN_ROWS)]
    lane_blocks = [slice(c, c + _ATTN_LANES) for c in range(0, 2 * tile, _ATTN_LANES)]

    def fold8(x, op):
        return op(x.reshape(x.shape[0] // 8, 8, x.shape[1]), axis=0)

    def stage_a(kj, par):
        off = pl.multiple_of(kj * tile, tile)
        bidx = jnp.clip(kj - qi, -2, 2) + 2
        for cs in lane_blocks:
            rhs_c = rhs_ref[:, cs]
            mx = None
            for rs in row_blocks:
                k_blk = k_ref[0, pl.ds(off + rs.start, _ATTN_ROWS), :]
                s = _dot(k_blk, rhs_c) + bias_ref[bidx, 0, rs, cs]
                s_refs[par][rs, cs] = s
                part = fold8(s, jnp.max)
                mx = part if mx is None else jnp.maximum(mx, part)
            mx_refs[par][:, cs] = jnp.max(mx, axis=0, keepdims=True)

    def stage_b(par):
        m_old = m_ref[...]
        m_new = jnp.maximum(m_old, mx_refs[par][...])
        alpha = jnp.exp2(m_old - m_new)
        al_refs[par][...] = alpha
        m_ref[...] = m_new
        for cs in lane_blocks:
            m_c = m_new[:, cs]
            lsum = None
            for rs in row_blocks:
                p = jnp.exp2(s_refs[par][rs, cs] - m_c)
                p_refs[par][rs, cs] = p.astype(_BF16)
                part = fold8(p, jnp.sum)
                lsum = part if lsum is None else lsum + part
            l_ref[:, cs] = alpha[:, cs] * l_ref[:, cs] + jnp.sum(lsum, axis=0, keepdims=True)

    def stage_c(kj, par):
        off = pl.multiple_of(kj * tile, tile)
        for cs in lane_blocks:
            pv = _dot(vT_ref[0, :, pl.ds(off, tile)], p_refs[par][:, cs])
            acc_ref[:, cs] = al_refs[par][:, cs] * acc_ref[:, cs] + pv

    stage_a(0, 0)
    stage_a(1, 1)
    stage_b(0)

    def pair(t, carry):
        j = 2 * t + 1
        stage_a(j + 1, 0)
        stage_b(1)
        stage_c(j - 1, 0)
        stage_a(j + 2, 1)
        stage_b(0)
        stage_c(j, 1)
        return carry

    lax.fori_loop(0, (n_k - 2) // 2, pair, 0)
    stage_b(1)
    stage_c(n_k - 2, 0)
    stage_c(n_k - 1, 1)

    o_ref[0, 0] = _attention_output(lam_ref, l_ref[...], acc_ref[...], gsub_ref, tile=tile,
                                    out_scale=out_scale)


def _attention_output(lam_ref, l, acc, gsub_ref, *, tile, out_scale):
    o = acc * (1.0 / l)
    oT = o[:, :tile] - lam_ref[0] * o[:, tile:]
    ms = jnp.mean(oT * oT, axis=0, keepdims=True)
    oT = oT * lax.rsqrt(ms + _EPS) * gsub_ref[...] * out_scale
    return oT.T.astype(_BF16)


def _attention_bounded_kernel(lam_ref, qT_ref, k_ref, vT_ref, bias_ref, gsub_ref, o_ref,
                              rhs_ref, p0_ref, p1_ref, l_ref, acc_ref, *, tile, out_scale):
    seq = k_ref.shape[1]
    n_k = seq // tile
    hd = _ATT_HEAD_DIM
    p_refs = (p0_ref, p1_ref)
    row_blocks = [slice(r, r + _ATTN_ROWS) for r in range(0, tile, _ATTN_ROWS)]
    lane_blocks = [slice(c, c + _ATTN_LANES) for c in range(0, 2 * tile, _ATTN_LANES)]

    def setup_q(qi, q_load):
        qpar = qi % 2
        qT = qT_ref[0, :, pl.ds(pl.multiple_of(q_load * tile, tile), tile)]
        row = lax.broadcasted_iota(jnp.int32, qT.shape, 0)
        zero = jnp.zeros_like(qT)
        rhs_ref[qpar, :, :tile] = jnp.where(row < hd, qT, zero)
        rhs_ref[qpar, :, tile:] = jnp.where(row >= hd, qT, zero)
        l_ref[qpar] = jnp.zeros(l_ref.shape[1:], _F32)

    def stage_ab(qi, kj, par):
        qpar = qi % 2
        off = pl.multiple_of(kj * tile, tile)
        bidx = jnp.clip(kj - qi, -2, 2) + 2
        k_tile = k_ref[0, pl.ds(off, tile), :]
        for cs in lane_blocks:
            s = _dot(k_tile, rhs_ref[qpar, :, cs])
            lsum = None
            for rs in row_blocks:
                p = jnp.exp2(s[rs, :] + bias_ref[bidx, 0, rs, cs])
                p_refs[par][rs, cs] = p.astype(_BF16)
                part = jnp.sum(p.reshape(_ATTN_ROWS // 8, 8, _ATTN_LANES), axis=0)
                lsum = part if lsum is None else lsum + part
            l_ref[qpar, :, cs] += jnp.sum(lsum, axis=0, keepdims=True)

    def stage_c(kj, par):
        off = pl.multiple_of(kj * tile, tile)
        for cs in lane_blocks:
            acc_ref[:, cs] += _dot(vT_ref[0, :, pl.ds(off, tile)], p_refs[par][:, cs])

    n_q = seq // tile
    acc_ref[...] = jnp.zeros(acc_ref.shape, _F32)
    setup_q(0, 0)
    stage_ab(0, 0, 0)

    def q_tile_body(qi, carry):
        def pair(t, c):
            j = 2 * t
            stage_ab(qi, j + 1, 1)
            stage_c(j, 0)
            stage_ab(qi, j + 2, 0)
            stage_c(j + 1, 1)
            return c

        lax.fori_loop(0, n_k // 2 - 1, pair, 0)
        stage_ab(qi, n_k - 1, 1)
        stage_c(n_k - 2, 0)
        q_next = qi + 1
        setup_q(q_next, jnp.minimum(q_next, n_q - 1))
        stage_ab(q_next, 0, 0)
        stage_c(n_k - 1, 1)
        o_ref[0, 0, pl.ds(pl.multiple_of(qi * tile, tile), tile), :] = _attention_output(
            lam_ref, l_ref[qi % 2], acc_ref[...], gsub_ref, tile=tile, out_scale=out_scale)
        acc_ref[...] = jnp.zeros(acc_ref.shape, _F32)
        return carry

    lax.fori_loop(0, n_q, q_tile_body, 0)


def _lambda_kernel(q1_ref, k1_ref, q2_ref, k2_ref, out_ref, *, lambda_init):
    a = jnp.sum(q1_ref[...] * k1_ref[...], axis=-1, keepdims=True)
    b = jnp.sum(q2_ref[...] * k2_ref[...], axis=-1, keepdims=True)
    out_ref[...] = jnp.broadcast_to(jnp.exp(a) - jnp.exp(b) + lambda_init, out_ref.shape)


def _lambda(q1, k1, q2, k2, lambda_init):
    out = pl.pallas_call(
        functools.partial(_lambda_kernel, lambda_init=lambda_init),
        out_shape=jax.ShapeDtypeStruct((1, _LANES), _F32),
        name="lambda_scalar",
    )(q1, k1, q2, k2)
    return out[0, :1]


def _attention(bound_ok, lam, qT, k, vT, bias_tiles, gsub_col, *, tile, out_scale):
    B, d_att, S = qT.shape
    v_dim = 2 * _ATT_HEAD_DIM
    n_heads = d_att // v_dim
    assert S % (2 * tile) == 0
    row = lambda: pltpu.VMEM((1, 2 * tile), _F32)
    s_buf = lambda: pltpu.VMEM((tile, 2 * tile), _F32)
    p_buf = lambda: pltpu.VMEM((tile, 2 * tile), _BF16)
    rhs = pltpu.VMEM((v_dim, 2 * tile), _BF16)
    acc = pltpu.VMEM((v_dim, 2 * tile), _F32)

    out_shape = jax.ShapeDtypeStruct((B, n_heads, S, v_dim), _BF16)
    operands = (lam, qT, k, vT, bias_tiles, gsub_col)

    def bounded():
        return pl.pallas_call(
            functools.partial(_attention_bounded_kernel, tile=tile, out_scale=out_scale),
            grid=(n_heads, B),
            in_specs=[pl.BlockSpec(memory_space=pltpu.SMEM),
                      pl.BlockSpec((1, v_dim, S), lambda h, b: (b, h, 0)),
                      pl.BlockSpec((1, S, v_dim), lambda h, b: (b, 0, h)),
                      pl.BlockSpec((1, v_dim, S), lambda h, b: (b, h, 0)),
                      pl.BlockSpec((5, 1, tile, 2 * tile), lambda h, b: (0, h, 0, 0)),
                      pl.BlockSpec((v_dim, 1), lambda h, b: (0, 0))],
            out_specs=pl.BlockSpec((1, 1, S, v_dim), lambda h, b: (b, h, 0, 0)),
            out_shape=out_shape,
            scratch_shapes=[pltpu.VMEM((2, v_dim, 2 * tile), _BF16), p_buf(), p_buf(),
                            pltpu.VMEM((2, 1, 2 * tile), _F32), acc],
            compiler_params=_cparams("parallel", "parallel"),
            name="diff_attention_bounded",
        )(*operands)

    def exact():
        return pl.pallas_call(
            functools.partial(_attention_kernel, tile=tile, out_scale=out_scale),
            grid=(n_heads, B, S // tile),
            in_specs=[pl.BlockSpec(memory_space=pltpu.SMEM),
                      pl.BlockSpec((1, v_dim, tile), lambda h, b, i: (b, h, i)),
                      pl.BlockSpec((1, S, v_dim), lambda h, b, i: (b, 0, h)),
                      pl.BlockSpec((1, v_dim, S), lambda h, b, i: (b, h, 0)),
                      pl.BlockSpec((5, 1, tile, 2 * tile), lambda h, b, i: (0, h, 0, 0)),
                      pl.BlockSpec((v_dim, 1), lambda h, b, i: (0, 0))],
            out_specs=pl.BlockSpec((1, 1, tile, v_dim), lambda h, b, i: (b, h, i, 0)),
            out_shape=out_shape,
            scratch_shapes=[rhs, s_buf(), s_buf(), p_buf(), p_buf(), row(), row(), row(), row(), row(),
                            row(), acc],
            compiler_params=_cparams("parallel", "parallel", "parallel"),
            name="diff_attention",
        )(*operands)

    return lax.cond(bound_ok, bounded, exact)


def _outproj_kernel(oatt_ref, b_ref, u_ref, uprev_ref, unext_ref, x_ref, cw_ref, wout_ref, gffn_ref,
                    wr_ref, br_ref, x1_ref, tw_ref, route_ref, *, d_att, n_experts):
    i = pl.program_id(1)
    n_i = pl.num_programs(1)
    u = u_ref[0].astype(_F32)
    tm = u.shape[0]
    prev_row = jnp.where(i > 0, uprev_ref[0, _BF16_SUBLANES - 1:_BF16_SUBLANES, :].astype(_F32), 0.0)
    next_row = jnp.where(i < n_i - 1, unext_ref[0, 0:1, :].astype(_F32), 0.0)
    rows = lax.broadcasted_iota(jnp.int32, (tm, 1), 0)
    u_prev = jnp.where(rows == 0, prev_row, pltpu.roll(u, 1, axis=0))
    u_next = jnp.where(rows == tm - 1, next_row, pltpu.roll(u, tm - 1, axis=0))
    conv = cw_ref[0:1, :] * u_prev + cw_ref[1:2, :] * u + cw_ref[2:3, :] * u_next
    o_conv = (b_ref[0].astype(_F32) * conv).astype(_BF16)
    o_att = jnp.concatenate([oatt_ref[0, h] for h in range(oatt_ref.shape[1])], axis=1)
    mix = _dot(o_att, wout_ref[:d_att, :]) + _dot(o_conv, wout_ref[d_att:, :])
    x1 = x_ref[0] + mix
    x1_ref[0] = x1
    ms = jnp.mean(x1 * x1, axis=-1, keepdims=True)
    t = x1 * lax.rsqrt(ms + _EPS) * gffn_ref[...]
    t_hi = t.astype(_BF16)
    half = x1.shape[1] // 2
    tw_ref[0, :, :half] = _pack_bf16_pairs(t)
    t_lo = (t - t_hi.astype(_F32)).astype(_BF16)
    a_hi = _dot(t_hi, wr_ref[...])
    a_lo = _dot(t_lo, wr_ref[...])
    logits = a_hi[:, :_LANES] + a_hi[:, _LANES:] + a_lo[:, :_LANES] + br_ref[...]
    n_groups = n_experts // _EXPERTS_PER_GROUP
    lane = lax.broadcasted_iota(jnp.int32, (1, _LANES), 1)
    lane_f = lane.astype(_F32)
    big = float(_LANES)
    gmask = (lane >= n_experts) & (lane < n_experts + n_groups)
    gl = jnp.where(gmask, logits, _NEG_BIG)
    gmax = jnp.max(gl, axis=-1, keepdims=True)
    gsum = jnp.sum(jnp.where(gmask, jnp.exp(gl - gmax), 0.0), axis=-1, keepdims=True)
    g_w = 1.0 / gsum
    g_idx = jnp.min(jnp.where(gmask & (gl == gmax), lane_f - n_experts, big), axis=-1, keepdims=True)
    lo = g_idx * _EXPERTS_PER_GROUP
    emask = (lane_f >= lo) & (lane_f < lo + _EXPERTS_PER_GROUP)
    el = jnp.where(emask, logits, _NEG_BIG)
    emax = jnp.max(el, axis=-1, keepdims=True)
    ep = jnp.where(emask, jnp.exp(el - emax), 0.0)
    p_exp = ep / jnp.sum(ep, axis=-1, keepdims=True)
    top1 = jnp.max(p_exp, axis=-1, keepdims=True)
    i1 = jnp.min(jnp.where(emask & (p_exp == top1), lane_f, big), axis=-1, keepdims=True)
    rest = jnp.where(emask & (lane_f != i1), p_exp, -1.0)
    top2 = jnp.max(rest, axis=-1, keepdims=True)
    i2 = jnp.min(jnp.where(rest == top2, lane_f, big), axis=-1, keepdims=True)
    denom = top1 + top2
    gates = jnp.where(lane_f == i1 - lo, g_w * (top1 / denom),
                      jnp.where(lane_f == i2 - lo, g_w * (top2 / denom), 0.0))
    tw_ref[0, :, half:] = lax.bitcast_convert_type(gates, jnp.uint32)
    route_ref[0] = jnp.broadcast_to(g_idx, (tm, _LANES)).T[:_F32_SUBLANES, :]


def _outproj(oatt, b, u, x, conv_w, w_out, gffn, wr, br, *, n_experts, tm):
    B, S, D = x.shape
    n_heads, v_dim = oatt.shape[1], oatt.shape[3]
    d_att = n_heads * v_dim
    d_conv = b.shape[-1]
    hb = _BF16_SUBLANES
    per_tile = tm // hb
    n_halo = S // hb
    tok = lambda width: pl.BlockSpec((1, tm, width), lambda bb, i: (bb, i, 0))
    full = lambda shape: pl.BlockSpec(shape, lambda bb, i: (0,) * len(shape))
    return pl.pallas_call(
        functools.partial(_outproj_kernel, d_att=d_att, n_experts=n_experts),
        grid=(B, S // tm),
        in_specs=[pl.BlockSpec((1, n_heads, tm, v_dim), lambda bb, i: (bb, 0, i, 0)), tok(d_conv), tok(d_conv),
                  pl.BlockSpec((1, hb, d_conv), lambda bb, i: (bb, jnp.maximum(i * per_tile - 1, 0), 0)),
                  pl.BlockSpec((1, hb, d_conv),
                               lambda bb, i: (bb, jnp.minimum((i + 1) * per_tile, n_halo - 1), 0)),
                  tok(D), full(conv_w.shape), full(w_out.shape), full((1, D)), full(wr.shape),
                  full((1, _LANES))],
        out_specs=[tok(D), tok(D // 2 + _LANES),
                   pl.BlockSpec((1, _F32_SUBLANES, tm), lambda bb, i: (bb, 0, i))],
        out_shape=[jax.ShapeDtypeStruct((B, S, D), _F32),
                   jax.ShapeDtypeStruct((B, S, D // 2 + _LANES), jnp.uint32),
                   jax.ShapeDtypeStruct((B, _F32_SUBLANES, S), _F32)],
        compiler_params=_cparams("parallel", "parallel"),
        name="outproj_router",
    )(oatt, b, u, u, u, x, conv_w, w_out, gffn, wr, br)


def _row_copies_wait(src_rows, dst_rows, sem):
    pltpu.make_async_copy(src_rows, dst_rows, sem).wait()


def _stage_indices(idx_ref, idx_smem, slot, idx_sem):
    c = pltpu.make_async_copy(idx_ref.at[0, 0], idx_smem.at[slot], idx_sem)
    c.start()
    c.wait()


def _scatter_rows_kernel(pads_ref, pos_ref, x_ref, out_hbm, idx_smem, idx_sem, sem, *, n_groups):
    i = pl.program_id(0)
    tm = x_ref.shape[0]
    _stage_indices(pos_ref, idx_smem, 0, idx_sem)
    for r in range(tm):
        pltpu.make_async_copy(x_ref.at[pl.ds(r, 1), :], out_hbm.at[pl.ds(idx_smem[0, r], 1), :],
                              sem).start()
    _row_copies_wait(x_ref, out_hbm.at[pl.ds(0, tm), :], sem)

    @pl.when(i == pl.num_programs(0) - 1)
    def _():
        for g in range(n_groups):
            lo = pads_ref[g]
            count = pads_ref[n_groups + g] - lo

            def fill(k, carry, lo=lo):
                pltpu.make_async_copy(x_ref.at[pl.ds(0, 1), :], out_hbm.at[pl.ds(lo + k, 1), :], sem).start()
                return carry

            def drain(k, carry):
                _row_copies_wait(x_ref.at[pl.ds(0, 1), :], out_hbm.at[pl.ds(0, 1), :], sem)
                return carry

            lax.fori_loop(0, count, fill, 0)
            lax.fori_loop(0, count, drain, 0)


def _scatter_rows(pads, pos, x, n_out_rows, *, tm):
    T, width = x.shape
    n_steps = T // tm
    return pl.pallas_call(
        functools.partial(_scatter_rows_kernel, n_groups=pads.shape[0] // 2),
        grid_spec=pltpu.PrefetchScalarGridSpec(
            num_scalar_prefetch=1,
            grid=(n_steps,),
            in_specs=[pl.BlockSpec((1, 1, tm), lambda i, pads: (i, 0, 0)),
                      pl.BlockSpec((tm, width), lambda i, pads: (i, 0))],
            out_specs=pl.BlockSpec(memory_space=pl.ANY),
            scratch_shapes=[pltpu.SMEM((1, tm), jnp.int32), pltpu.SemaphoreType.DMA,
                            pltpu.SemaphoreType.DMA]),
        out_shape=jax.ShapeDtypeStruct((n_out_rows, width), x.dtype),
        compiler_params=_cparams("arbitrary"),
        name="dispatch_rows",
    )(pads, pos.reshape(n_steps, 1, tm), x)


def _moe_sorted_kernel(tile_group_ref, tw_ref, wgu_ref, wdn_ref, out_ref, *, d_expert):
    del tile_group_ref
    half = out_ref.shape[1]
    t = _unpack_bf16_pairs(tw_ref[:, :half]).astype(_BF16)
    gates = lax.bitcast_convert_type(tw_ref[:, half:], _F32)
    width = _EXPERTS_PER_GROUP * d_expert
    gu = _dot(t, wgu_ref[0])
    g_lin = gu[:, :width]
    act = g_lin * jax.nn.sigmoid(g_lin) * gu[:, width:]
    parts = [(act[:, e * d_expert:(e + 1) * d_expert] * gates[:, e:e + 1]).astype(_BF16)
             for e in range(_EXPERTS_PER_GROUP)]
    out_ref[...] = _pack_bf16_pairs(_dot(jnp.concatenate(parts, axis=1), wdn_ref[0]))


def _moe_sorted(tile_group, tw_sorted, wgu, wdn, *, d_expert, tm):
    rows, width = tw_sorted.shape
    half = width - _LANES
    grp = lambda shape: pl.BlockSpec((1,) + shape, lambda i, tg: (tg[i], 0, 0))
    return pl.pallas_call(
        functools.partial(_moe_sorted_kernel, d_expert=d_expert),
        grid_spec=pltpu.PrefetchScalarGridSpec(
            num_scalar_prefetch=1,
            grid=(rows // tm,),
            in_specs=[pl.BlockSpec((tm, width), lambda i, tg: (i, 0)),
                      grp(wgu.shape[1:]), grp(wdn.shape[1:])],
            out_specs=pl.BlockSpec((tm, half), lambda i, tg: (i, 0))),
        out_shape=jax.ShapeDtypeStruct((rows, half), jnp.uint32),
        compiler_params=_cparams("parallel"),
        name="moe_sorted",
    )(tile_group, tw_sorted, wgu, wdn)


def _ple_kernel(idx_first_ref, idx_b_ref, idx_a_ref, moe_hbm, x1_ref, p_ref, wproj_ref, wgate_ref,
                gple_ref, y_ref, idx_smem, idx_sem, buf, sem):
    j = pl.program_id(0)
    tm = buf.shape[1]

    def rows_start(idx_slot, slot, lo=0, hi=None):
        for r in range(lo, tm if hi is None else hi):
            pltpu.make_async_copy(moe_hbm.at[pl.ds(idx_smem[idx_slot, r], 1), :],
                                  buf.at[slot, pl.ds(r, 1), :], sem.at[slot]).start()

    def rows_wait(slot):
        _row_copies_wait(moe_hbm.at[pl.ds(0, tm), :], buf.at[slot], sem.at[slot])

    def compute_rows(base, slot, lo, hi):
        rows = slice(base + lo, base + hi)
        x2 = x1_ref[rows, :] + _unpack_bf16_pairs(buf[slot, lo:hi, :])
        e_raw = _dot(p_ref[rows, :].astype(_BF16), wproj_ref[...])
        ms = jnp.mean(e_raw * e_raw, axis=-1, keepdims=True)
        emb = e_raw * lax.rsqrt(ms + _EPS) * gple_ref[...]
        gate_p = jax.nn.sigmoid(_dot(x2.astype(_BF16), wgate_ref[...]))
        y_ref[rows, :] = x2 + gate_p * emb

    def compute_and_fetch(base, slot, idx_slot, other):
        for lo in range(0, tm, _PLE_CHUNK):
            rows_start(idx_slot, other, lo, lo + _PLE_CHUNK)
            compute_rows(base, slot, lo, lo + _PLE_CHUNK)

    @pl.when(j == 0)
    def _():
        _stage_indices(idx_first_ref, idx_smem, 0, idx_sem)
        rows_start(0, 0)

    _stage_indices(idx_b_ref, idx_smem, 0, idx_sem)
    _stage_indices(idx_a_ref, idx_smem, 1, idx_sem)
    rows_wait(0)
    compute_and_fetch(0, 0, 0, 1)
    rows_wait(1)
    compute_and_fetch(tm, 1, 1, 0)

    @pl.when(j == pl.num_programs(0) - 1)
    def _():
        rows_wait(0)


def _ple(pos, moe_sorted, x1, p, wproj, wgate, gple, *, tm):
    T, D = x1.shape
    half = moe_sorted.shape[1]
    n_tiles = T // tm
    assert n_tiles % 2 == 0
    idx = pos.reshape(n_tiles, 1, tm)
    tok = lambda width: pl.BlockSpec((2 * tm, width), lambda j: (j, 0))
    full = lambda shape: pl.BlockSpec(shape, lambda j: (0,) * len(shape))
    idx_spec = lambda tile_of_step: pl.BlockSpec((1, 1, tm), lambda j: (tile_of_step(j), 0, 0))
    return pl.pallas_call(
        _ple_kernel,
        grid=(n_tiles // 2,),
        in_specs=[idx_spec(lambda j: 0), idx_spec(lambda j: 2 * j + 1),
                  idx_spec(lambda j: jnp.minimum(2 * j + 2, n_tiles - 1)),
                  pl.BlockSpec(memory_space=pl.ANY), tok(D), tok(p.shape[-1]), full(wproj.shape),
                  full(wgate.shape), full((1, D))],
        out_specs=tok(D),
        out_shape=jax.ShapeDtypeStruct((T, D), _F32),
        scratch_shapes=[pltpu.SMEM((2, tm), jnp.int32), pltpu.SemaphoreType.DMA,
                        pltpu.VMEM((2, tm, half), jnp.uint32), pltpu.SemaphoreType.DMA((2,))],
        compiler_params=_cparams("arbitrary"),
        name="ple_gate",
    )(idx, idx, idx, moe_sorted, x1, p, wproj, wgate, gple)


def _dispatch_plan(group_of_token, n_groups, tm):
    T = group_of_token.shape[0]
    n_tiles = T // tm + n_groups - 1
    onehot = (group_of_token[:, None] == jnp.arange(n_groups, dtype=jnp.int32)[None, :]).astype(jnp.int32)
    rank = jnp.take_along_axis(jnp.cumsum(onehot, axis=0), group_of_token[:, None], axis=1)[:, 0] - 1
    count = jnp.sum(onehot, axis=0)
    tiles_per_group = (count + tm - 1) // tm
    tile_end = jnp.cumsum(tiles_per_group)
    tile_start = tile_end - tiles_per_group
    pos = tile_start[group_of_token] * tm + rank
    tile_group = jnp.searchsorted(tile_end, jnp.arange(n_tiles, dtype=jnp.int32), side="right")
    tile_group = jnp.minimum(tile_group, n_groups - 1).astype(jnp.int32)
    pad_lo = tile_start * tm + count
    pad_hi = (tile_end * tm).at[n_groups - 1].set(n_tiles * tm)
    pads = jnp.concatenate([pad_lo, pad_hi]).astype(jnp.int32)
    return pos.astype(jnp.int32), tile_group, pads, n_tiles * tm


def _prepare_layer(i, norm_mix, w_in, q_norm, k_norm, lambda_q1, lambda_k1, lambda_q2, lambda_k2,
                   attn_sub_norm, conv_w, w_out, norm_ffn, w_group, b_group, w_erouter, b_erouter,
                   w_gate_up, w_down, w_ple_proj, w_ple_gate, ple_norm):
    D = w_in.shape[1]
    d_mix = w_out.shape[1]
    d_att = d_mix // 2
    d_conv = d_mix - d_att
    n_maps = d_att // _ATT_HEAD_DIM
    n_experts = w_gate_up.shape[1]
    n_groups = w_group.shape[-1]
    d_expert = w_down.shape[2]
    assert n_experts == n_groups * _EXPERTS_PER_GROUP and n_experts + n_groups <= _LANES
    lambda_init = 0.8 - 0.6 * math.exp(-0.3 * i)
    head_of = jnp.arange(d_att) // _ATT_HEAD_DIM
    bd = jnp.where(head_of[:, None] == head_of[None, :], 1.0 / _ATT_HEAD_DIM, 0.0).astype(_BF16)
    wr = jnp.zeros((D, _LANES), _F32)
    wr = wr.at[:, :n_experts].set(w_erouter[i]).at[:, n_experts:n_experts + n_groups].set(w_group[i])
    wr_hi = wr.astype(_BF16)
    wr_lo = (wr - wr_hi.astype(_F32)).astype(_BF16)
    br = jnp.zeros((1, _LANES), _F32)
    br = br.at[0, :n_experts].set(b_erouter[i]).at[0, n_experts:n_experts + n_groups].set(b_group[i])
    wg = w_gate_up[i].reshape(n_groups, _EXPERTS_PER_GROUP, D, 2, d_expert)
    wgu = jnp.transpose(wg, (0, 2, 3, 1, 4)).reshape(n_groups, D, 2 * _EXPERTS_PER_GROUP * d_expert)
    wdn = w_down[i].reshape(n_groups, _EXPERTS_PER_GROUP * d_expert, D)
    return dict(
        d_att=d_att, d_conv=d_conv, n_experts=n_experts, d_expert=d_expert, lambda_init=lambda_init,
        gmix=norm_mix[i][None, :], w_in=w_in[i].astype(_BF16),
        gq_col=q_norm[i][:, None], gk_row=jnp.tile(k_norm[i], n_maps)[None, :], bd=bd,
        lam_vecs=tuple(v[i][None, :] for v in (lambda_q1, lambda_k1, lambda_q2, lambda_k2)),
        gsub_col=attn_sub_norm[i][:, None], conv_w=conv_w[i], w_out=w_out[i].astype(_BF16),
        gffn=norm_ffn[i][None, :], wr=jnp.concatenate([wr_hi, wr_lo], axis=1), br=br,
        wgu=wgu.astype(_BF16), wdn=wdn.astype(_BF16), wproj=w_ple_proj[i].astype(_BF16),
        wgate=w_ple_gate[i].astype(_BF16), gple=ple_norm[i][None, :])


def _score_bound(q_gain, k_gain, rel_bias):
    hd = _ATT_HEAD_DIM
    rounding_slack = 1.02
    q_norm_max = math.sqrt(hd) * jnp.max(jnp.abs(q_gain)) * (_LOG2E / math.sqrt(hd))
    k_norm_max = math.sqrt(hd) * jnp.max(jnp.abs(k_gain))
    qk = rounding_slack * q_norm_max * k_norm_max
    b2 = rel_bias.astype(_F32) * _LOG2E
    b_max, b_min = jnp.max(b2, axis=0), jnp.min(b2, axis=0)
    ok = jnp.all(2.0 * qk + (b_max - b_min) <= _SAFE_EXPONENT_SPAN)
    ok = ok & jnp.isfinite(qk) & jnp.all(jnp.isfinite(b2))
    return ok, jnp.where(ok, qk + b_max, jnp.zeros_like(b_max))


def _layer(x, p_i, L, bias_tiles, bound_ok, lam):
    B, S, D = x.shape
    tm = _TOKEN_TILE
    tile = _ATTN_TILE
    assert S % tm == 0 and S % tile == 0 and tm % _BF16_SUBLANES == 0
    qT, k, vT, b, u = _inproj(x, L["gmix"], L["w_in"], L["gq_col"], L["gk_row"], L["bd"],
                              d_att=L["d_att"], d_conv=L["d_conv"], tm=tm)
    oatt = _attention(bound_ok, lam, qT, k, vT, bias_tiles, L["gsub_col"], tile=tile,
                      out_scale=1.0 - L["lambda_init"])
    x1, tw, route = _outproj(oatt, b, u, x, L["conv_w"], L["w_out"], L["gffn"], L["wr"], L["br"],
                             n_experts=L["n_experts"], tm=tm)
    x1 = x1.reshape(B * S, D)
    tw = tw.reshape(B * S, D // 2 + _LANES)
    group_of_token = route[:, 0, :].reshape(B * S).astype(jnp.int32)
    pos, tile_group, pads, sorted_rows = _dispatch_plan(group_of_token, L["wgu"].shape[0], tm)
    tw_sorted = _scatter_rows(pads, pos, tw, sorted_rows, tm=tm)
    moe = _moe_sorted(tile_group, tw_sorted, L["wgu"], L["wdn"], d_expert=L["d_expert"], tm=tm)
    y = _ple(pos, moe, x1, p_i.reshape(B * S, -1), L["wproj"], L["wgate"], L["gple"], tm=tm)
    return y.reshape(B, S, D)


def kernel(x_prompt, x_sample, p_prompt, p_sample, norm_mix, w_in, q_norm, k_norm, lambda_q1, lambda_k1, lambda_q2, lambda_k2, attn_sub_norm, conv_w, w_out, rel_bias, norm_ffn, w_group, b_group, w_erouter, b_erouter, w_gate_up, w_down, w_ple_proj, w_ple_gate, ple_norm):
    depth = w_in.shape[0]
    layers = []
    for i in range(depth):
        L = _prepare_layer(i, norm_mix, w_in, q_norm, k_norm, lambda_q1, lambda_k1, lambda_q2,
                           lambda_k2, attn_sub_norm, conv_w, w_out, norm_ffn, w_group, b_group,
                           w_erouter, b_erouter, w_gate_up, w_down, w_ple_proj, w_ple_gate, ple_norm)
        bound_ok, shift = _score_bound(q_norm[i], k_norm[i], rel_bias)
        layers.append((L, _bias_tiles(rel_bias, shift, _ATTN_TILE), bound_ok,
                       _lambda(*L["lam_vecs"], L["lambda_init"])))

    def encode(x, p):
        for i, (L, bias_tiles, bound_ok, lam) in enumerate(layers):
            x = _layer(x, p[i], L, bias_tiles, bound_ok, lam)
        return x

    return (encode(x_prompt, p_prompt), encode(x_sample, p_sample))
```

```python
import functools
import math

import jax
import jax.numpy as jnp
from jax import lax
from jax.experimental import pallas as pl
from jax.experimental.pallas import tpu as pltpu

_F32 = jnp.float32
_BF16 = jnp.bfloat16

_EPS = 1e-6
_ATT_HEAD_DIM = 64
_MAX_DISTANCE = 128
_EXPERTS_PER_GROUP = 4
_LANES = 128
_F32_SUBLANES = 8
_BF16_SUBLANES = 16
_ATTN_TILE = 512
_ATTN_ROWS = 128
_ATTN_LANES = 256
_TOKEN_TILE = 512
_PLE_CHUNK = 128
_DMA_PRIORITIES = 2
_VMEM_LIMIT_BYTES = 56 * 1024 * 1024
_SAFE_EXPONENT_SPAN = 100.0
_NEG_BIG = -1e30
_LOG2E = math.log2(math.e)


def _cparams(*sem):
    return pltpu.CompilerParams(dimension_semantics=sem, vmem_limit_bytes=_VMEM_LIMIT_BYTES)


def _dot(a, b):
    return jnp.dot(a, b, preferred_element_type=_F32)


def _pack_bf16_pairs(x):
    n = x.shape[1] // 2
    bits = lax.bitcast_convert_type(x.astype(_BF16).astype(_F32), jnp.uint32)
    return (bits[:, :n] & jnp.uint32(0xFFFF0000)) | (bits[:, n:] >> 16)


def _unpack_bf16_pairs(u):
    hi = lax.bitcast_convert_type(u & jnp.uint32(0xFFFF0000), _F32)
    lo = lax.bitcast_convert_type(u << 16, _F32)
    return jnp.concatenate([hi, lo], axis=1)


def _bias_tile_kernel(tbl_ref, shift_ref, out_ref, *, tile, num_buckets):
    di = pl.program_id(0)
    m = pl.program_id(1)
    delta = (di - 2) * tile
    kk = lax.broadcasted_iota(jnp.int32, (tile, tile), 0)
    qq = lax.broadcasted_iota(jnp.int32, (tile, tile), 1)
    rel = kk - qq + delta
    half = num_buckets // 2
    max_exact = half // 2
    ret = jnp.where(rel > 0, half, 0)
    n = jnp.abs(rel)
    nf = jnp.maximum(n, 1).astype(_F32)
    large = max_exact + (jnp.log(nf / max_exact) / math.log(_MAX_DISTANCE / max_exact)
                         * (half - max_exact)).astype(jnp.int32)
    large = jnp.minimum(large, half - 1)
    bucket = ret + jnp.where(n < max_exact, n, large)
    acc = jnp.zeros((tile, tile), _F32)
    for b in range(num_buckets):
        acc = jnp.where(bucket == b, tbl_ref[b, m], acc)
    out_ref[0, 0] = acc * _LOG2E - shift_ref[m]


def _bias_tiles(rel_bias, shift, tile):
    num_buckets, n_maps = rel_bias.shape
    assert tile >= _MAX_DISTANCE
    return pl.pallas_call(
        functools.partial(_bias_tile_kernel, tile=tile, num_buckets=num_buckets),
        grid=(5, n_maps),
        in_specs=[pl.BlockSpec(memory_space=pltpu.SMEM), pl.BlockSpec(memory_space=pltpu.SMEM)],
        out_specs=pl.BlockSpec((1, 1, tile, tile), lambda di, m: (di, m // 2, 0, m % 2)),
        out_shape=jax.ShapeDtypeStruct((5, n_maps // 2, tile, 2 * tile), _F32),
        compiler_params=_cparams("arbitrary", "arbitrary"),
        name="bias_tiles",
    )(rel_bias.astype(_F32), shift.astype(_F32))


def _inproj_kernel(x_ref, gmix_ref, w_ref, gq_ref, gk_ref, bd_ref,
                   qT_ref, k_ref, vT_ref, b_ref, u_ref, *, d_att, d_conv, scale):
    x = x_ref[0]
    ms = jnp.mean(x * x, axis=-1, keepdims=True)
    h = (x * lax.rsqrt(ms + _EPS) * gmix_ref[...]).astype(_BF16)

    def proj(lo, n):
        return _dot(h, w_ref[:, lo:lo + n])

    tm = x.shape[0]
    n_maps = d_att // _ATT_HEAD_DIM
    zqT = proj(0, d_att).T.reshape(n_maps, _ATT_HEAD_DIM, tm)
    qms = jnp.mean(zqT * zqT, axis=1, keepdims=True)
    qn = zqT * lax.rsqrt(qms + _EPS) * gq_ref[...][None]
    qT_ref[0] = (qn * scale).reshape(d_att, tm).astype(_BF16)
    zk = proj(d_att, d_att)
    kms = _dot((zk * zk).astype(_BF16), bd_ref[...])
    k_ref[0] = (zk * lax.rsqrt(kms + _EPS) * gk_ref[...]).astype(_BF16)
    vT_ref[0] = proj(2 * d_att, d_att).T.astype(_BF16)
    b_ref[0] = proj(3 * d_att, d_conv).astype(_BF16)
    c = proj(3 * d_att + d_conv, d_conv)
    xc = proj(3 * d_att + 2 * d_conv, d_conv)
    u_ref[0] = (c * xc).astype(_BF16)


def _inproj(x, gmix, w_in, gq_col, gk_row, bd, *, d_att, d_conv, tm):
    B, S, D = x.shape
    d_in = w_in.shape[1]
    grid = (B, S // tm)
    tok = lambda width: pl.BlockSpec((1, tm, width), lambda b, i: (b, i, 0))
    tr = lambda rows: pl.BlockSpec((1, rows, tm), lambda b, i: (b, 0, i))
    full = lambda shape: pl.BlockSpec(shape, lambda b, i: (0,) * len(shape))
    return pl.pallas_call(
        functools.partial(_inproj_kernel, d_att=d_att, d_conv=d_conv,
                          scale=_LOG2E / math.sqrt(_ATT_HEAD_DIM)),
        grid=grid,
        in_specs=[tok(D), full((1, D)), full((D, d_in)), full((_ATT_HEAD_DIM, 1)), full((1, d_att)),
                  full((d_att, d_att))],
        out_specs=[tr(d_att), tok(d_att), tr(d_att), tok(d_conv), tok(d_conv)],
        out_shape=[jax.ShapeDtypeStruct((B, d_att, S), _BF16),
                   jax.ShapeDtypeStruct((B, S, d_att), _BF16),
                   jax.ShapeDtypeStruct((B, d_att, S), _BF16),
                   jax.ShapeDtypeStruct((B, S, d_conv), _BF16),
                   jax.ShapeDtypeStruct((B, S, d_conv), _BF16)],
        compiler_params=_cparams("parallel", "parallel"),
        name="inproj",
    )(x, gmix, w_in, gq_col, gk_row, bd)


def _attention_kernel(lam_ref, qT_ref, k_ref, vT_ref, bias_ref, gsub_ref, o_ref,
                      rhs_ref, s0_ref, s1_ref, p0_ref, p1_ref, mx0_ref, mx1_ref, al0_ref, al1_ref,
                      m_ref, l_ref, acc_ref, *, tile, out_scale):
    qi = pl.program_id(2)
    n_k = k_ref.shape[1] // tile
    hd = _ATT_HEAD_DIM
    s_refs, p_refs = (s0_ref, s1_ref), (p0_ref, p1_ref)
    mx_refs, al_refs = (mx0_ref, mx1_ref), (al0_ref, al1_ref)
    qT = qT_ref[0]
    row = lax.broadcasted_iota(jnp.int32, qT.shape, 0)
    zero = jnp.zeros_like(qT)
    rhs_ref[:, :tile] = jnp.where(row < hd, qT, zero)
    rhs_ref[:, tile:] = jnp.where(row >= hd, qT, zero)
    m_ref[...] = jnp.full(m_ref.shape, _NEG_BIG, _F32)
    l_ref[...] = jnp.zeros(l_ref.shape, _F32)
    acc_ref[...] = jnp.zeros(acc_ref.shape, _F32)

    row_blocks = [slice(r, r + _ATTN_ROWS) for r in range(0, tile, _ATTN_ROWS)]
    lane_blocks = [slice(c, c + _ATTN_LANES) for c in range(0, 2 * tile, _ATTN_LANES)]

    def fold8(x, op):
        return op(x.reshape(x.shape[0] // 8, 8, x.shape[1]), axis=0)

    def stage_a(kj, par):
        off = pl.multiple_of(kj * tile, tile)
        bidx = jnp.clip(kj - qi, -2, 2) + 2
        for cs in lane_blocks:
            rhs_c = rhs_ref[:, cs]
            mx = None
            for rs in row_blocks:
                k_blk = k_ref[0, pl.ds(off + rs.start, _ATTN_ROWS), :]
                s = _dot(k_blk, rhs_c) + bias_ref[bidx, 0, rs, cs]
                s_refs[par][rs, cs] = s
                part = fold8(s, jnp.max)
                mx = part if mx is None else jnp.maximum(mx, part)
            mx_refs[par][:, cs] = jnp.max(mx, axis=0, keepdims=True)

    def stage_b(par):
        m_old = m_ref[...]
        m_new = jnp.maximum(m_old, mx_refs[par][...])
        alpha = jnp.exp2(m_old - m_new)
        al_refs[par][...] = alpha
        m_ref[...] = m_new
        for cs in lane_blocks:
            m_c = m_new[:, cs]
            lsum = None
            for rs in row_blocks:
                p = jnp.exp2(s_refs[par][rs, cs] - m_c)
                p_refs[par][rs, cs] = p.astype(_BF16)
                part = fold8(p, jnp.sum)
                lsum = part if lsum is None else lsum + part
            l_ref[:, cs] = alpha[:, cs] * l_ref[:, cs] + jnp.sum(lsum, axis=0, keepdims=True)

    def stage_c(kj, par):
        off = pl.multiple_of(kj * tile, tile)
        for cs in lane_blocks:
            pv = _dot(vT_ref[0, :, pl.ds(off, tile)], p_refs[par][:, cs])
            acc_ref[:, cs] = al_refs[par][:, cs] * acc_ref[:, cs] + pv

    stage_a(0, 0)
    stage_a(1, 1)
    stage_b(0)

    def pair(t, carry):
        j = 2 * t + 1
        stage_a(j + 1, 0)
        stage_b(1)
        stage_c(j - 1, 0)
        stage_a(j + 2, 1)
        stage_b(0)
        stage_c(j, 1)
        return carry

    lax.fori_loop(0, (n_k - 2) // 2, pair, 0)
    stage_b(1)
    stage_c(n_k - 2, 0)
    stage_c(n_k - 1, 1)

    o_ref[0, 0] = _attention_output(lam_ref, l_ref[...], acc_ref[...], gsub_ref, tile=tile,
                                    out_scale=out_scale)


def _attention_output(lam_ref, l, acc, gsub_ref, *, tile, out_scale):
    o = acc * (1.0 / l)
    oT = o[:, :tile] - lam_ref[0] * o[:, tile:]
    ms = jnp.mean(oT * oT, axis=0, keepdims=True)
    oT = oT * lax.rsqrt(ms + _EPS) * gsub_ref[...] * out_scale
    return oT.T.astype(_BF16)


def _attention_bounded_kernel(lam_ref, qT_ref, k_ref, vT_ref, bias_ref, gsub_ref, o_ref,
                              rhs_ref, p0_ref, p1_ref, l_ref, acc_ref, *, tile, out_scale):
    seq = k_ref.shape[1]
    n_k = seq // tile
    hd = _ATT_HEAD_DIM
    p_refs = (p0_ref, p1_ref)
    row_blocks = [slice(r, r + _ATTN_ROWS) for r in range(0, tile, _ATTN_ROWS)]
    lane_blocks = [slice(c, c + _ATTN_LANES) for c in range(0, 2 * tile, _ATTN_LANES)]

    def setup_q(qi, q_load):
        qpar = qi % 2
        qT = qT_ref[0, :, pl.ds(pl.multiple_of(q_load * tile, tile), tile)]
        row = lax.broadcasted_iota(jnp.int32, qT.shape, 0)
        zero = jnp.zeros_like(qT)
        rhs_ref[qpar, :, :tile] = jnp.where(row < hd, qT, zero)
        rhs_ref[qpar, :, tile:] = jnp.where(row >= hd, qT, zero)
        l_ref[qpar] = jnp.zeros(l_ref.shape[1:], _F32)

    def stage_ab(qi, kj, par):
        qpar = qi % 2
        off = pl.multiple_of(kj * tile, tile)
        bidx = jnp.clip(kj - qi, -2, 2) + 2
        k_tile = k_ref[0, pl.ds(off, tile), :]
        for cs in lane_blocks:
            s = _dot(k_tile, rhs_ref[qpar, :, cs])
            lsum = None
            for rs in row_blocks:
                p = jnp.exp2(s[rs, :] + bias_ref[bidx, 0, rs, cs])
                p_refs[par][rs, cs] = p.astype(_BF16)
                part = jnp.sum(p.reshape(_ATTN_ROWS // 8, 8, _ATTN_LANES), axis=0)
                lsum = part if lsum is None else lsum + part
            l_ref[qpar, :, cs] += jnp.sum(lsum, axis=0, keepdims=True)

    def stage_c(kj, par):
        off = pl.multiple_of(kj * tile, tile)
        for cs in lane_blocks:
            acc_ref[:, cs] += _dot(vT_ref[0, :, pl.ds(off, tile)], p_refs[par][:, cs])

    n_q = seq // tile
    acc_ref[...] = jnp.zeros(acc_ref.shape, _F32)
    setup_q(0, 0)
    stage_ab(0, 0, 0)

    def q_tile_body(qi, carry):
        def pair(t, c):
            j = 2 * t
            stage_ab(qi, j + 1, 1)
            stage_c(j, 0)
            stage_ab(qi, j + 2, 0)
            stage_c(j + 1, 1)
            return c

        lax.fori_loop(0, n_k // 2 - 1, pair, 0)
        stage_ab(qi, n_k - 1, 1)
        stage_c(n_k - 2, 0)
        q_next = qi + 1
        setup_q(q_next, jnp.minimum(q_next, n_q - 1))
        stage_ab(q_next, 0, 0)
        stage_c(n_k - 1, 1)
        o_ref[0, 0, pl.ds(pl.multiple_of(qi * tile, tile), tile), :] = _attention_output(
            lam_ref, l_ref[qi % 2], acc_ref[...], gsub_ref, tile=tile, out_scale=out_scale)
        acc_ref[...] = jnp.zeros(acc_ref.shape, _F32)
        return carry

    lax.fori_loop(0, n_q, q_tile_body, 0)


def _lambda_kernel(q1_ref, k1_ref, q2_ref, k2_ref, out_ref, *, lambda_init):
    a = jnp.sum(q1_ref[...] * k1_ref[...], axis=-1, keepdims=True)
    b = jnp.sum(q2_ref[...] * k2_ref[...], axis=-1, keepdims=True)
    out_ref[...] = jnp.broadcast_to(jnp.exp(a) - jnp.exp(b) + lambda_init, out_ref.shape)


def _lambda(q1, k1, q2, k2, lambda_init):
    out = pl.pallas_call(
        functools.partial(_lambda_kernel, lambda_init=lambda_init),
        out_shape=jax.ShapeDtypeStruct((1, _LANES), _F32),
        name="lambda_scalar",
    )(q1, k1, q2, k2)
    return out[0, :1]


def _attention(bound_ok, lam, qT, k, vT, bias_tiles, gsub_col, *, tile, out_scale):
    B, d_att, S = qT.shape
    v_dim = 2 * _ATT_HEAD_DIM
    n_heads = d_att // v_dim
    assert S % (2 * tile) == 0
    row = lambda: pltpu.VMEM((1, 2 * tile), _F32)
    s_buf = lambda: pltpu.VMEM((tile, 2 * tile), _F32)
    p_buf = lambda: pltpu.VMEM((tile, 2 * tile), _BF16)
    rhs = pltpu.VMEM((v_dim, 2 * tile), _BF16)
    acc = pltpu.VMEM((v_dim, 2 * tile), _F32)

    out_shape = jax.ShapeDtypeStruct((B, n_heads, S, v_dim), _BF16)
    operands = (lam, qT, k, vT, bias_tiles, gsub_col)

    def bounded():
        return pl.pallas_call(
            functools.partial(_attention_bounded_kernel, tile=tile, out_scale=out_scale),
            grid=(n_heads, B),
            in_specs=[pl.BlockSpec(memory_space=pltpu.SMEM),
                      pl.BlockSpec((1, v_dim, S), lambda h, b: (b, h, 0)),
                      pl.BlockSpec((1, S, v_dim), lambda h, b: (b, 0, h)),
                      pl.BlockSpec((1, v_dim, S), lambda h, b: (b, h, 0)),
                      pl.BlockSpec((5, 1, tile, 2 * tile), lambda h, b: (0, h, 0, 0)),
                      pl.BlockSpec((v_dim, 1), lambda h, b: (0, 0))],
            out_specs=pl.BlockSpec((1, 1, S, v_dim), lambda h, b: (b, h, 0, 0)),
            out_shape=out_shape,
            scratch_shapes=[pltpu.VMEM((2, v_dim, 2 * tile), _BF16), p_buf(), p_buf(),
                            pltpu.VMEM((2, 1, 2 * tile), _F32), acc],
            compiler_params=_cparams("parallel", "parallel"),
            name="diff_attention_bounded",
        )(*operands)

    def exact():
        return pl.pallas_call(
            functools.partial(_attention_kernel, tile=tile, out_scale=out_scale),
            grid=(n_heads, B, S // tile),
            in_specs=[pl.BlockSpec(memory_space=pltpu.SMEM),
                      pl.BlockSpec((1, v_dim, tile), lambda h, b, i: (b, h, i)),
                      pl.BlockSpec((1, S, v_dim), lambda h, b, i: (b, 0, h)),
                      pl.BlockSpec((1, v_dim, S), lambda h, b, i: (b, h, 0)),
                      pl.BlockSpec((5, 1, tile, 2 * tile), lambda h, b, i: (0, h, 0, 0)),
                      pl.BlockSpec((v_dim, 1), lambda h, b, i: (0, 0))],
            out_specs=pl.BlockSpec((1, 1, tile, v_dim), lambda h, b, i: (b, h, i, 0)),
            out_shape=out_shape,
            scratch_shapes=[rhs, s_buf(), s_buf(), p_buf(), p_buf(), row(), row(), row(), row(), row(),
                            row(), acc],
            compiler_params=_cparams("parallel", "parallel", "parallel"),
            name="diff_attention",
        )(*operands)

    return lax.cond(bound_ok, bounded, exact)


def _outproj_kernel(oatt_ref, b_ref, u_ref, uprev_ref, unext_ref, x_ref, cw_ref, wout_ref, gffn_ref,
                    wr_ref, br_ref, x1_ref, tw_ref, route_ref, *, d_att, n_experts):
    i = pl.program_id(1)
    n_i = pl.num_programs(1)
    u = u_ref[0].astype(_F32)
    tm = u.shape[0]
    prev_row = jnp.where(i > 0, uprev_ref[0, _BF16_SUBLANES - 1:_BF16_SUBLANES, :].astype(_F32), 0.0)
    next_row = jnp.where(i < n_i - 1, unext_ref[0, 0:1, :].astype(_F32), 0.0)
    rows = lax.broadcasted_iota(jnp.int32, (tm, 1), 0)
    u_prev = jnp.where(rows == 0, prev_row, pltpu.roll(u, 1, axis=0))
    u_next = jnp.where(rows == tm - 1, next_row, pltpu.roll(u, tm - 1, axis=0))
    conv = cw_ref[0:1, :] * u_prev + cw_ref[1:2, :] * u + cw_ref[2:3, :] * u_next
    o_conv = (b_ref[0].astype(_F32) * conv).astype(_BF16)
    o_att = jnp.concatenate([oatt_ref[0, h] for h in range(oatt_ref.shape[1])], axis=1)
    mix = _dot(o_att, wout_ref[:d_att, :]) + _dot(o_conv, wout_ref[d_att:, :])
    x1 = x_ref[0] + mix
    x1_ref[0] = x1
    ms = jnp.mean(x1 * x1, axis=-1, keepdims=True)
    t = x1 * lax.rsqrt(ms + _EPS) * gffn_ref[...]
    t_hi = t.astype(_BF16)
    half = x1.shape[1] // 2
    tw_ref[0, :, :half] = _pack_bf16_pairs(t)
    t_lo = (t - t_hi.astype(_F32)).astype(_BF16)
    a_hi = _dot(t_hi, wr_ref[...])
    a_lo = _dot(t_lo, wr_ref[...])
    logits = a_hi[:, :_LANES] + a_hi[:, _LANES:] + a_lo[:, :_LANES] + br_ref[...]
    n_groups = n_experts // _EXPERTS_PER_GROUP
    lane = lax.broadcasted_iota(jnp.int32, (1, _LANES), 1)
    lane_f = lane.astype(_F32)
    big = float(_LANES)
    gmask = (lane >= n_experts) & (lane < n_experts + n_groups)
    gl = jnp.where(gmask, logits, _NEG_BIG)
    gmax = jnp.max(gl, axis=-1, keepdims=True)
    gsum = jnp.sum(jnp.where(gmask, jnp.exp(gl - gmax), 0.0), axis=-1, keepdims=True)
    g_w = 1.0 / gsum
    g_idx = jnp.min(jnp.where(gmask & (gl == gmax), lane_f - n_experts, big), axis=-1, keepdims=True)
    lo = g_idx * _EXPERTS_PER_GROUP
    emask = (lane_f >= lo) & (lane_f < lo + _EXPERTS_PER_GROUP)
    el = jnp.where(emask, logits, _NEG_BIG)
    emax = jnp.max(el, axis=-1, keepdims=True)
    ep = jnp.where(emask, jnp.exp(el - emax), 0.0)
    p_exp = ep / jnp.sum(ep, axis=-1, keepdims=True)
    top1 = jnp.max(p_exp, axis=-1, keepdims=True)
    i1 = jnp.min(jnp.where(emask & (p_exp == top1), lane_f, big), axis=-1, keepdims=True)
    rest = jnp.where(emask & (lane_f != i1), p_exp, -1.0)
    top2 = jnp.max(rest, axis=-1, keepdims=True)
    i2 = jnp.min(jnp.where(rest == top2, lane_f, big), axis=-1, keepdims=True)
    denom = top1 + top2
    gates = jnp.where(lane_f == i1 - lo, g_w * (top1 / denom),
                      jnp.where(lane_f == i2 - lo, g_w * (top2 / denom), 0.0))
    tw_ref[0, :, half:] = lax.bitcast_convert_type(gates, jnp.uint32)
    route_ref[0] = jnp.broadcast_to(g_idx, (tm, _LANES)).T[:_F32_SUBLANES, :]


def _outproj(oatt, b, u, x, conv_w, w_out, gffn, wr, br, *, n_experts, tm):
    B, S, D = x.shape
    n_heads, v_dim = oatt.shape[1], oatt.shape[3]
    d_att = n_heads * v_dim
    d_conv = b.shape[-1]
    hb = _BF16_SUBLANES
    per_tile = tm // hb
    n_halo = S // hb
    tok = lambda width: pl.BlockSpec((1, tm, width), lambda bb, i: (bb, i, 0))
    full = lambda shape: pl.BlockSpec(shape, lambda bb, i: (0,) * len(shape))
    return pl.pallas_call(
        functools.partial(_outproj_kernel, d_att=d_att, n_experts=n_experts),
        grid=(B, S // tm),
        in_specs=[pl.BlockSpec((1, n_heads, tm, v_dim), lambda bb, i: (bb, 0, i, 0)), tok(d_conv), tok(d_conv),
                  pl.BlockSpec((1, hb, d_conv), lambda bb, i: (bb, jnp.maximum(i * per_tile - 1, 0), 0)),
                  pl.BlockSpec((1, hb, d_conv),
                               lambda bb, i: (bb, jnp.minimum((i + 1) * per_tile, n_halo - 1), 0)),
                  tok(D), full(conv_w.shape), full(w_out.shape), full((1, D)), full(wr.shape),
                  full((1, _LANES))],
        out_specs=[tok(D), tok(D // 2 + _LANES),
                   pl.BlockSpec((1, _F32_SUBLANES, tm), lambda bb, i: (bb, 0, i))],
        out_shape=[jax.ShapeDtypeStruct((B, S, D), _F32),
                   jax.ShapeDtypeStruct((B, S, D // 2 + _LANES), jnp.uint32),
                   jax.ShapeDtypeStruct((B, _F32_SUBLANES, S), _F32)],
        compiler_params=_cparams("parallel", "parallel"),
        name="outproj_router",
    )(oatt, b, u, u, u, x, conv_w, w_out, gffn, wr, br)


def _row_copies_wait(src_rows, dst_rows, sem):
    pltpu.make_async_copy(src_rows, dst_rows, sem).wait()


def _stage_indices(idx_ref, idx_smem, slot, idx_sem):
    c = pltpu.make_async_copy(idx_ref.at[0, 0], idx_smem.at[slot], idx_sem)
    c.start()
    c.wait()


def _scatter_rows_kernel(pads_ref, pos_ref, x_ref, out_hbm, idx_smem, idx_sem, sem, *, n_groups):
    i = pl.program_id(0)
    tm = x_ref.shape[0]
    _stage_indices(pos_ref, idx_smem, 0, idx_sem)
    for r in range(tm):
        pltpu.make_async_copy(x_ref.at[pl.ds(r, 1), :], out_hbm.at[pl.ds(idx_smem[0, r], 1), :],
                              sem).start(priority=r % _DMA_PRIORITIES)
    _row_copies_wait(x_ref, out_hbm.at[pl.ds(0, tm), :], sem)

    @pl.when(i == pl.num_programs(0) - 1)
    def _():
        for g in range(n_groups):
            lo = pads_ref[g]
            count = pads_ref[n_groups + g] - lo

            def fill(k, carry, lo=lo):
                pltpu.make_async_copy(x_ref.at[pl.ds(0, 1), :], out_hbm.at[pl.ds(lo + k, 1), :], sem).start()
                return carry

            def drain(k, carry):
                _row_copies_wait(x_ref.at[pl.ds(0, 1), :], out_hbm.at[pl.ds(0, 1), :], sem)
                return carry

            lax.fori_loop(0, count, fill, 0)
            lax.fori_loop(0, count, drain, 0)


def _scatter_rows(pads, pos, x, n_out_rows, *, tm):
    T, width = x.shape
    n_steps = T // tm
    return pl.pallas_call(
        functools.partial(_scatter_rows_kernel, n_groups=pads.shape[0] // 2),
        grid_spec=pltpu.PrefetchScalarGridSpec(
            num_scalar_prefetch=1,
            grid=(n_steps,),
            in_specs=[pl.BlockSpec((1, 1, tm), lambda i, pads: (i, 0, 0)),
                      pl.BlockSpec((tm, width), lambda i, pads: (i, 0))],
            out_specs=pl.BlockSpec(memory_space=pl.ANY),
            scratch_shapes=[pltpu.SMEM((1, tm), jnp.int32), pltpu.SemaphoreType.DMA,
                            pltpu.SemaphoreType.DMA]),
        out_shape=jax.ShapeDtypeStruct((n_out_rows, width), x.dtype),
        compiler_params=_cparams("arbitrary"),
        name="dispatch_rows",
    )(pads, pos.reshape(n_steps, 1, tm), x)


def _moe_sorted_kernel(tile_group_ref, tw_ref, wgu_ref, wdn_ref, out_ref, *, d_expert):
    del tile_group_ref
    half = out_ref.shape[1]
    t = _unpack_bf16_pairs(tw_ref[:, :half]).astype(_BF16)
    gates = lax.bitcast_convert_type(tw_ref[:, half:], _F32)
    width = _EXPERTS_PER_GROUP * d_expert
    gu = _dot(t, wgu_ref[0])
    g_lin = gu[:, :width]
    act = g_lin * jax.nn.sigmoid(g_lin) * gu[:, width:]
    parts = [(act[:, e * d_expert:(e + 1) * d_expert] * gates[:, e:e + 1]).astype(_BF16)
             for e in range(_EXPERTS_PER_GROUP)]
    out_ref[...] = _pack_bf16_pairs(_dot(jnp.concatenate(parts, axis=1), wdn_ref[0]))


def _moe_sorted(tile_group, tw_sorted, wgu, wdn, *, d_expert, tm):
    rows, width = tw_sorted.shape
    half = width - _LANES
    grp = lambda shape: pl.BlockSpec((1,) + shape, lambda i, tg: (tg[i], 0, 0))
    return pl.pallas_call(
        functools.partial(_moe_sorted_kernel, d_expert=d_expert),
        grid_spec=pltpu.PrefetchScalarGridSpec(
            num_scalar_prefetch=1,
            grid=(rows // tm,),
            in_specs=[pl.BlockSpec((tm, width), lambda i, tg: (i, 0)),
                      grp(wgu.shape[1:]), grp(wdn.shape[1:])],
            out_specs=pl.BlockSpec((tm, half), lambda i, tg: (i, 0))),
        out_shape=jax.ShapeDtypeStruct((rows, half), jnp.uint32),
        compiler_params=_cparams("parallel"),
        name="moe_sorted",
    )(tile_group, tw_sorted, wgu, wdn)


def _ple_kernel(idx_first_ref, idx_b_ref, idx_a_ref, moe_hbm, x1_ref, p_ref, wproj_ref, wgate_ref,
                gple_ref, y_ref, idx_smem, idx_sem, buf, sem):
    j = pl.program_id(0)
    tm = buf.shape[1]

    def rows_start(idx_slot, slot, lo=0, hi=None):
        for r in range(lo, tm if hi is None else hi):
            pltpu.make_async_copy(moe_hbm.at[pl.ds(idx_smem[idx_slot, r], 1), :],
                                  buf.at[slot, pl.ds(r, 1), :], sem.at[slot]).start(
                                      priority=r % _DMA_PRIORITIES)

    def rows_wait(slot):
        _row_copies_wait(moe_hbm.at[pl.ds(0, tm), :], buf.at[slot], sem.at[slot])

    def compute_rows(base, slot, lo, hi):
        rows = slice(base + lo, base + hi)
        x2 = x1_ref[rows, :] + _unpack_bf16_pairs(buf[slot, lo:hi, :])
        e_raw = _dot(p_ref[rows, :].astype(_BF16), wproj_ref[...])
        ms = jnp.mean(e_raw * e_raw, axis=-1, keepdims=True)
        emb = e_raw * lax.rsqrt(ms + _EPS) * gple_ref[...]
        gate_p = jax.nn.sigmoid(_dot(x2.astype(_BF16), wgate_ref[...]))
        y_ref[rows, :] = x2 + gate_p * emb

    def compute_and_fetch(base, slot, idx_slot, other):
        for lo in range(0, tm, _PLE_CHUNK):
            rows_start(idx_slot, other, lo, lo + _PLE_CHUNK)
            compute_rows(base, slot, lo, lo + _PLE_CHUNK)

    @pl.when(j == 0)
    def _():
        _stage_indices(idx_first_ref, idx_smem, 0, idx_sem)
        rows_start(0, 0)

    _stage_indices(idx_b_ref, idx_smem, 0, idx_sem)
    _stage_indices(idx_a_ref, idx_smem, 1, idx_sem)
    rows_wait(0)
    compute_and_fetch(0, 0, 0, 1)
    rows_wait(1)
    compute_and_fetch(tm, 1, 1, 0)

    @pl.when(j == pl.num_programs(0) - 1)
    def _():
        rows_wait(0)


def _ple(pos, moe_sorted, x1, p, wproj, wgate, gple, *, tm):
    T, D = x1.shape
    half = moe_sorted.shape[1]
    n_tiles = T // tm
    assert n_tiles % 2 == 0
    idx = pos.reshape(n_tiles, 1, tm)
    tok = lambda width: pl.BlockSpec((2 * tm, width), lambda j: (j, 0))
    full = lambda shape: pl.BlockSpec(shape, lambda j: (0,) * len(shape))
    idx_spec = lambda tile_of_step: pl.BlockSpec((1, 1, tm), lambda j: (tile_of_step(j), 0, 0))
    return pl.pallas_call(
        _ple_kernel,
        grid=(n_tiles // 2,),
        in_specs=[idx_spec(lambda j: 0), idx_spec(lambda j: 2 * j + 1),
                  idx_spec(lambda j: jnp.minimum(2 * j + 2, n_tiles - 1)),
                  pl.BlockSpec(memory_space=pl.ANY), tok(D), tok(p.shape[-1]), full(wproj.shape),
                  full(wgate.shape), full((1, D))],
        out_specs=tok(D),
        out_shape=jax.ShapeDtypeStruct((T, D), _F32),
        scratch_shapes=[pltpu.SMEM((2, tm), jnp.int32), pltpu.SemaphoreType.DMA,
                        pltpu.VMEM((2, tm, half), jnp.uint32), pltpu.SemaphoreType.DMA((2,))],
        compiler_params=_cparams("arbitrary"),
        name="ple_gate",
    )(idx, idx, idx, moe_sorted, x1, p, wproj, wgate, gple)


def _dispatch_plan(group_of_token, n_groups, tm):
    T = group_of_token.shape[0]
    n_tiles = T // tm + n_groups - 1
    onehot = (group_of_token[:, None] == jnp.arange(n_groups, dtype=jnp.int32)[None, :]).astype(jnp.int32)
    rank = jnp.take_along_axis(jnp.cumsum(onehot, axis=0), group_of_token[:, None], axis=1)[:, 0] - 1
    count = jnp.sum(onehot, axis=0)
    tiles_per_group = (count + tm - 1) // tm
    tile_end = jnp.cumsum(tiles_per_group)
    tile_start = tile_end - tiles_per_group
    pos = tile_start[group_of_token] * tm + rank
    tile_group = jnp.searchsorted(tile_end, jnp.arange(n_tiles, dtype=jnp.int32), side="right")
    tile_group = jnp.minimum(tile_group, n_groups - 1).astype(jnp.int32)
    pad_lo = tile_start * tm + count
    pad_hi = (tile_end * tm).at[n_groups - 1].set(n_tiles * tm)
    pads = jnp.concatenate([pad_lo, pad_hi]).astype(jnp.int32)
    return pos.astype(jnp.int32), tile_group, pads, n_tiles * tm


def _prepare_layer(i, norm_mix, w_in, q_norm, k_norm, lambda_q1, lambda_k1, lambda_q2, lambda_k2,
                   attn_sub_norm, conv_w, w_out, norm_ffn, w_group, b_group, w_erouter, b_erouter,
                   w_gate_up, w_down, w_ple_proj, w_ple_gate, ple_norm):
    D = w_in.shape[1]
    d_mix = w_out.shape[1]
    d_att = d_mix // 2
    d_conv = d_mix - d_att
    n_maps = d_att // _ATT_HEAD_DIM
    n_experts = w_gate_up.shape[1]
    n_groups = w_group.shape[-1]
    d_expert = w_down.shape[2]
    assert n_experts == n_groups * _EXPERTS_PER_GROUP and n_experts + n_groups <= _LANES
    lambda_init = 0.8 - 0.6 * math.exp(-0.3 * i)
    head_of = jnp.arange(d_att) // _ATT_HEAD_DIM
    bd = jnp.where(head_of[:, None] == head_of[None, :], 1.0 / _ATT_HEAD_DIM, 0.0).astype(_BF16)
    wr = jnp.zeros((D, _LANES), _F32)
    wr = wr.at[:, :n_experts].set(w_erouter[i]).at[:, n_experts:n_experts + n_groups].set(w_group[i])
    wr_hi = wr.astype(_BF16)
    wr_lo = (wr - wr_hi.astype(_F32)).astype(_BF16)
    br = jnp.zeros((1, _LANES), _F32)
    br = br.at[0, :n_experts].set(b_erouter[i]).at[0, n_experts:n_experts + n_groups].set(b_group[i])
    wg = w_gate_up[i].reshape(n_groups, _EXPERTS_PER_GROUP, D, 2, d_expert)
    wgu = jnp.transpose(wg, (0, 2, 3, 1, 4)).reshape(n_groups, D, 2 * _EXPERTS_PER_GROUP * d_expert)
    wdn = w_down[i].reshape(n_groups, _EXPERTS_PER_GROUP * d_expert, D)
    return dict(
        d_att=d_att, d_conv=d_conv, n_experts=n_experts, d_expert=d_expert, lambda_init=lambda_init,
        gmix=norm_mix[i][None, :], w_in=w_in[i].astype(_BF16),
        gq_col=q_norm[i][:, None], gk_row=jnp.tile(k_norm[i], n_maps)[None, :], bd=bd,
        lam_vecs=tuple(v[i][None, :] for v in (lambda_q1, lambda_k1, lambda_q2, lambda_k2)),
        gsub_col=attn_sub_norm[i][:, None], conv_w=conv_w[i], w_out=w_out[i].astype(_BF16),
        gffn=norm_ffn[i][None, :], wr=jnp.concatenate([wr_hi, wr_lo], axis=1), br=br,
        wgu=wgu.astype(_BF16), wdn=wdn.astype(_BF16), wproj=w_ple_proj[i].astype(_BF16),
        wgate=w_ple_gate[i].astype(_BF16), gple=ple_norm[i][None, :])


def _score_bound(q_gain, k_gain, rel_bias):
    hd = _ATT_HEAD_DIM
    rounding_slack = 1.02
    q_norm_max = math.sqrt(hd) * jnp.max(jnp.abs(q_gain)) * (_LOG2E / math.sqrt(hd))
    k_norm_max = math.sqrt(hd) * jnp.max(jnp.abs(k_gain))
    qk = rounding_slack * q_norm_max * k_norm_max
    b2 = rel_bias.astype(_F32) * _LOG2E
    b_max, b_min = jnp.max(b2, axis=0), jnp.min(b2, axis=0)
    ok = jnp.all(2.0 * qk + (b_max - b_min) <= _SAFE_EXPONENT_SPAN)
    ok = ok & jnp.isfinite(qk) & jnp.all(jnp.isfinite(b2))
    return ok, jnp.where(ok, qk + b_max, jnp.zeros_like(b_max))


def _layer(x, p_i, L, bias_tiles, bound_ok, lam):
    B, S, D = x.shape
    tm = _TOKEN_TILE
    tile = _ATTN_TILE
    assert S % tm == 0 and S % tile == 0 and tm % _BF16_SUBLANES == 0
    qT, k, vT, b, u = _inproj(x, L["gmix"], L["w_in"], L["gq_col"], L["gk_row"], L["bd"],
                              d_att=L["d_att"], d_conv=L["d_conv"], tm=tm)
    oatt = _attention(bound_ok, lam, qT, k, vT, bias_tiles, L["gsub_col"], tile=tile,
                      out_scale=1.0 - L["lambda_init"])
    x1, tw, route = _outproj(oatt, b, u, x, L["conv_w"], L["w_out"], L["gffn"], L["wr"], L["br"],
                             n_experts=L["n_experts"], tm=tm)
    x1 = x1.reshape(B * S, D)
    tw = tw.reshape(B * S, D // 2 + _LANES)
    group_of_token = route[:, 0, :].reshape(B * S).astype(jnp.int32)
    pos, tile_group, pads, sorted_rows = _dispatch_plan(group_of_token, L["wgu"].shape[0], tm)
    tw_sorted = _scatter_rows(pads, pos, tw, sorted_rows, tm=tm)
    moe = _moe_sorted(tile_group, tw_sorted, L["wgu"], L["wdn"], d_expert=L["d_expert"], tm=tm)
    y = _ple(pos, moe, x1, p_i.reshape(B * S, -1), L["wproj"], L["wgate"], L["gple"], tm=tm)
    return y.reshape(B, S, D)


def kernel(x_prompt, x_sample, p_prompt, p_sample, norm_mix, w_in, q_norm, k_norm, lambda_q1, lambda_k1, lambda_q2, lambda_k2, attn_sub_norm, conv_w, w_out, rel_bias, norm_ffn, w_group, b_group, w_erouter, b_erouter, w_gate_up, w_down, w_ple_proj, w_ple_gate, ple_norm):
    depth = w_in.shape[0]
    layers = []
    for i in range(depth):
        L = _prepare_layer(i, norm_mix, w_in, q_norm, k_norm, lambda_q1, lambda_k1, lambda_q2,
                           lambda_k2, attn_sub_norm, conv_w, w_out, norm_ffn, w_group, b_group,
                           w_erouter, b_erouter, w_gate_up, w_down, w_ple_proj, w_ple_gate, ple_norm)
        bound_ok, shift = _score_bound(q_norm[i], k_norm[i], rel_bias)
        layers.append((L, _bias_tiles(rel_bias, shift, _ATTN_TILE), bound_ok,
                       _lambda(*L["lam_vecs"], L["lambda_init"])))

    def encode(x, p):
        for i, (L, bias_tiles, bound_ok, lam) in enumerate(layers):
            x = _layer(x, p[i], L, bias_tiles, bound_ok, lam)
        return x

    return (encode(x_prompt, p_prompt), encode(x_sample, p_sample))
```

```python
import functools
import math

import jax
import jax.numpy as jnp
from jax import lax
from jax.experimental import pallas as pl
from jax.experimental.pallas import tpu as pltpu

_F32 = jnp.float32
_BF16 = jnp.bfloat16

_EPS = 1e-6
_ATT_HEAD_DIM = 64
_MAX_DISTANCE = 128
_EXPERTS_PER_GROUP = 4
_LANES = 128
_F32_SUBLANES = 8
_BF16_SUBLANES = 16
_ATTN_TILE = 512
_ATTN_ROWS = 128
_ATTN_LANES = 256
_TOKEN_TILE = 512
_PLE_CHUNK = 128
_DMA_PRIORITIES = 2
_VMEM_LIMIT_BYTES = 56 * 1024 * 1024
_SAFE_EXPONENT_SPAN = 100.0
_NEG_BIG = -1e30
_LOG2E = math.log2(math.e)


def _cparams(*sem):
    return pltpu.CompilerParams(dimension_semantics=sem, vmem_limit_bytes=_VMEM_LIMIT_BYTES)


def _dot(a, b):
    return jnp.dot(a, b, preferred_element_type=_F32)


def _pack_bf16_pairs(x):
    n = x.shape[1] // 2
    bits = lax.bitcast_convert_type(x.astype(_BF16).astype(_F32), jnp.uint32)
    return (bits[:, :n] & jnp.uint32(0xFFFF0000)) | (bits[:, n:] >> 16)


def _unpack_bf16_pairs(u):
    hi = lax.bitcast_convert_type(u & jnp.uint32(0xFFFF0000), _F32)
    lo = lax.bitcast_convert_type(u << 16, _F32)
    return jnp.concatenate([hi, lo], axis=1)


def _bias_tile_kernel(tbl_ref, shift_ref, out_ref, *, tile, num_buckets):
    di = pl.program_id(0)
    m = pl.program_id(1)
    half = num_buckets // 2
    max_exact = half // 2
    is_far = (di == 0) | (di == 4)

    @pl.when(is_far)
    def _():
        b_far = jnp.where(di == 0, half - 1, num_buckets - 1)
        out_ref[0, 0] = jnp.full((tile, tile), tbl_ref[b_far, m] * _LOG2E - shift_ref[m], _F32)

    @pl.when(jnp.logical_not(is_far))
    def _():
        delta = (di - 2) * tile
        kk = lax.broadcasted_iota(jnp.int32, (tile, tile), 0)
        qq = lax.broadcasted_iota(jnp.int32, (tile, tile), 1)
        rel = kk - qq + delta
        ret = jnp.where(rel > 0, half, 0)
        n = jnp.abs(rel)
        nf = jnp.maximum(n, 1).astype(_F32)
        large = max_exact + (jnp.log(nf / max_exact) / math.log(_MAX_DISTANCE / max_exact)
                             * (half - max_exact)).astype(jnp.int32)
        large = jnp.minimum(large, half - 1)
        bucket = ret + jnp.where(n < max_exact, n, large)
        acc = jnp.zeros((tile, tile), _F32)
        for b in range(num_buckets):
            acc = jnp.where(bucket == b, tbl_ref[b, m], acc)
        out_ref[0, 0] = acc * _LOG2E - shift_ref[m]


def _bias_tiles(rel_bias, shift, tile):
    num_buckets, n_maps = rel_bias.shape
    assert tile >= _MAX_DISTANCE
    return pl.pallas_call(
        functools.partial(_bias_tile_kernel, tile=tile, num_buckets=num_buckets),
        grid=(5, n_maps),
        in_specs=[pl.BlockSpec(memory_space=pltpu.SMEM), pl.BlockSpec(memory_space=pltpu.SMEM)],
        out_specs=pl.BlockSpec((1, 1, tile, tile), lambda di, m: (di, m // 2, 0, m % 2)),
        out_shape=jax.ShapeDtypeStruct((5, n_maps // 2, tile, 2 * tile), _F32),
        compiler_params=_cparams("arbitrary", "arbitrary"),
        name="bias_tiles",
    )(rel_bias.astype(_F32), shift.astype(_F32))


def _inproj_kernel(x_ref, gmix_ref, w_ref, gq_ref, gk_ref, bd_ref,
                   qT_ref, k_ref, vT_ref, b_ref, u_ref, *, d_att, d_conv, scale):
    x = x_ref[0]
    ms = jnp.mean(x * x, axis=-1, keepdims=True)
    h = (x * lax.rsqrt(ms + _EPS) * gmix_ref[...]).astype(_BF16)

    def proj(lo, n):
        return _dot(h, w_ref[:, lo:lo + n])

    tm = x.shape[0]
    n_maps = d_att // _ATT_HEAD_DIM
    zqT = proj(0, d_att).T.reshape(n_maps, _ATT_HEAD_DIM, tm)
    qms = jnp.mean(zqT * zqT, axis=1, keepdims=True)
    qn = zqT * lax.rsqrt(qms + _EPS) * gq_ref[...][None]
    qT_ref[0] = (qn * scale).reshape(d_att, tm).astype(_BF16)
    zk = proj(d_att, d_att)
    kms = _dot((zk * zk).astype(_BF16), bd_ref[...])
    k_ref[0] = (zk * lax.rsqrt(kms + _EPS) * gk_ref[...]).astype(_BF16)
    vT_ref[0] = proj(2 * d_att, d_att).T.astype(_BF16)
    b_ref[0] = proj(3 * d_att, d_conv).astype(_BF16)
    c = proj(3 * d_att + d_conv, d_conv)
    xc = proj(3 * d_att + 2 * d_conv, d_conv)
    u_ref[0] = (c * xc).astype(_BF16)


def _inproj(x, gmix, w_in, gq_col, gk_row, bd, *, d_att, d_conv, tm):
    B, S, D = x.shape
    d_in = w_in.shape[1]
    grid = (B, S // tm)
    tok = lambda width: pl.BlockSpec((1, tm, width), lambda b, i: (b, i, 0))
    tr = lambda rows: pl.BlockSpec((1, rows, tm), lambda b, i: (b, 0, i))
    full = lambda shape: pl.BlockSpec(shape, lambda b, i: (0,) * len(shape))
    return pl.pallas_call(
        functools.partial(_inproj_kernel, d_att=d_att, d_conv=d_conv,
                          scale=_LOG2E / math.sqrt(_ATT_HEAD_DIM)),
        grid=grid,
        in_specs=[tok(D), full((1, D)), full((D, d_in)), full((_ATT_HEAD_DIM, 1)), full((1, d_att)),
                  full((d_att, d_att))],
        out_specs=[tr(d_att), tok(d_att), tr(d_att), tok(d_conv), tok(d_conv)],
        out_shape=[jax.ShapeDtypeStruct((B, d_att, S), _BF16),
                   jax.ShapeDtypeStruct((B, S, d_att), _BF16),
                   jax.ShapeDtypeStruct((B, d_att, S), _BF16),
                   jax.ShapeDtypeStruct((B, S, d_conv), _BF16),
                   jax.ShapeDtypeStruct((B, S, d_conv), _BF16)],
        compiler_params=_cparams("parallel", "parallel"),
        name="inproj",
    )(x, gmix, w_in, gq_col, gk_row, bd)


def _attention_kernel(lam_ref, qT_ref, k_ref, vT_ref, bias_ref, gsub_ref, o_ref,
                      rhs_ref, s0_ref, s1_ref, p0_ref, p1_ref, mx0_ref, mx1_ref, al0_ref, al1_ref,
                      m_ref, l_ref, acc_ref, *, tile, out_scale):
    qi = pl.program_id(2)
    n_k = k_ref.shape[1] // tile
    hd = _ATT_HEAD_DIM
    s_refs, p_refs = (s0_ref, s1_ref), (p0_ref, p1_ref)
    mx_refs, al_refs = (mx0_ref, mx1_ref), (al0_ref, al1_ref)
    qT = qT_ref[0]
    row = lax.broadcasted_iota(jnp.int32, qT.shape, 0)
    zero = jnp.zeros_like(qT)
    rhs_ref[:, :tile] = jnp.where(row < hd, qT, zero)
    rhs_ref[:, tile:] = jnp.where(row >= hd, qT, zero)
    m_ref[...] = jnp.full(m_ref.shape, _NEG_BIG, _F32)
    l_ref[...] = jnp.zeros(l_ref.shape, _F32)
    acc_ref[...] = jnp.zeros(acc_ref.shape, _F32)

    row_blocks = [slice(r, r + _ATTN_ROWS) for r in range(0, tile, _ATTN_ROWS)]
    lane_blocks = [slice(c, c + _ATTN_LANES) for c in range(0, 2 * tile, _ATTN_LANES)]

    def fold8(x, op):
        return op(x.reshape(x.shape[0] // 8, 8, x.shape[1]), axis=0)

    def stage_a(kj, par):
        off = pl.multiple_of(kj * tile, tile)
        bidx = jnp.clip(kj - qi, -2, 2) + 2
        for cs in lane_blocks:
            rhs_c = rhs_ref[:, cs]
            mx = None
            for rs in row_blocks:
                k_blk = k_ref[0, pl.ds(off + rs.start, _ATTN_ROWS), :]
                s = _dot(k_blk, rhs_c) + bias_ref[bidx, 0, rs, cs]
                s_refs[par][rs, cs] = s
                part = fold8(s, jnp.max)
                mx = part if mx is None else jnp.maximum(mx, part)
            mx_refs[par][:, cs] = jnp.max(mx, axis=0, keepdims=True)

    def stage_b(par):
        m_old = m_ref[...]
        m_new = jnp.maximum(m_old, mx_refs[par][...])
        alpha = jnp.exp2(m_old - m_new)
        al_refs[par][...] = alpha
        m_ref[...] = m_new
        for cs in lane_blocks:
            m_c = m_new[:, cs]
            lsum = None
            for rs in row_blocks:
                p = jnp.exp2(s_refs[par][rs, cs] - m_c)
                p_refs[par][rs, cs] = p.astype(_BF16)
                part = fold8(p, jnp.sum)
                lsum = part if lsum is None else lsum + part
            l_ref[:, cs] = alpha[:, cs] * l_ref[:, cs] + jnp.sum(lsum, axis=0, keepdims=True)

    def stage_c(kj, par):
        off = pl.multiple_of(kj * tile, tile)
        for cs in lane_blocks:
            pv = _dot(vT_ref[0, :, pl.ds(off, tile)], p_refs[par][:, cs])
            acc_ref[:, cs] = al_refs[par][:, cs] * acc_ref[:, cs] + pv

    stage_a(0, 0)
    stage_a(1, 1)
    stage_b(0)

    def pair(t, carry):
        j = 2 * t + 1
        stage_a(j + 1, 0)
        stage_b(1)
        stage_c(j - 1, 0)
        stage_a(j + 2, 1)
        stage_b(0)
        stage_c(j, 1)
        return carry

    lax.fori_loop(0, (n_k - 2) // 2, pair, 0)
    stage_b(1)
    stage_c(n_k - 2, 0)
    stage_c(n_k - 1, 1)

    o_ref[0, 0] = _attention_output(lam_ref, l_ref[...], acc_ref[...], gsub_ref, tile=tile,
                                    out_scale=out_scale)


def _attention_output(lam_ref, l, acc, gsub_ref, *, tile, out_scale):
    o = acc * (1.0 / l)
    oT = o[:, :tile] - lam_ref[0] * o[:, tile:]
    ms = jnp.mean(oT * oT, axis=0, keepdims=True)
    oT = oT * lax.rsqrt(ms + _EPS) * gsub_ref[...] * out_scale
    return oT.T.astype(_BF16)


def _attention_bounded_kernel(lam_ref, qT_ref, k_ref, vT_ref, bias_ref, gsub_ref, o_ref,
                              rhs_ref, p0_ref, p1_ref, l_ref, acc_ref, *, tile, out_scale):
    seq = k_ref.shape[1]
    n_k = seq // tile
    hd = _ATT_HEAD_DIM
    p_refs = (p0_ref, p1_ref)
    row_blocks = [slice(r, r + _ATTN_ROWS) for r in range(0, tile, _ATTN_ROWS)]
    lane_blocks = [slice(c, c + _ATTN_LANES) for c in range(0, 2 * tile, _ATTN_LANES)]

    def setup_q(qi, q_load):
        qpar = qi % 2
        qT = qT_ref[0, :, pl.ds(pl.multiple_of(q_load * tile, tile), tile)]
        row = lax.broadcasted_iota(jnp.int32, qT.shape, 0)
        zero = jnp.zeros_like(qT)
        rhs_ref[qpar, :, :tile] = jnp.where(row < hd, qT, zero)
        rhs_ref[qpar, :, tile:] = jnp.where(row >= hd, qT, zero)
        l_ref[qpar] = jnp.zeros(l_ref.shape[1:], _F32)

    def stage_ab(qi, kj, par):
        qpar = qi % 2
        off = pl.multiple_of(kj * tile, tile)
        bidx = jnp.clip(kj - qi, -2, 2) + 2
        k_tile = k_ref[0, pl.ds(off, tile), :]
        for cs in lane_blocks:
            s = _dot(k_tile, rhs_ref[qpar, :, cs])
            lsum = None
            for rs in row_blocks:
                p = jnp.exp2(s[rs, :] + bias_ref[bidx, 0, rs, cs])
                p_refs[par][rs, cs] = p.astype(_BF16)
                part = jnp.sum(p.reshape(_ATTN_ROWS // 8, 8, _ATTN_LANES), axis=0)
                lsum = part if lsum is None else lsum + part
            l_ref[qpar, :, cs] += jnp.sum(lsum, axis=0, keepdims=True)

    def stage_c(kj, par):
        off = pl.multiple_of(kj * tile, tile)
        for cs in lane_blocks:
            acc_ref[:, cs] += _dot(vT_ref[0, :, pl.ds(off, tile)], p_refs[par][:, cs])

    n_q = seq // tile
    acc_ref[...] = jnp.zeros(acc_ref.shape, _F32)
    setup_q(0, 0)
    stage_ab(0, 0, 0)

    def q_tile_body(qi, carry):
        def pair(t, c):
            j = 2 * t
            stage_ab(qi, j + 1, 1)
            stage_c(j, 0)
            stage_ab(qi, j + 2, 0)
            stage_c(j + 1, 1)
            return c

        lax.fori_loop(0, n_k // 2 - 1, pair, 0)
        stage_ab(qi, n_k - 1, 1)
        stage_c(n_k - 2, 0)
        q_next = qi + 1
        setup_q(q_next, jnp.minimum(q_next, n_q - 1))
        stage_ab(q_next, 0, 0)
        stage_c(n_k - 1, 1)
        o_ref[0, 0, pl.ds(pl.multiple_of(qi * tile, tile), tile), :] = _attention_output(
            lam_ref, l_ref[qi % 2], acc_ref[...], gsub_ref, tile=tile, out_scale=out_scale)
        acc_ref[...] = jnp.zeros(acc_ref.shape, _F32)
        return carry

    lax.fori_loop(0, n_q, q_tile_body, 0)


def _lambda_kernel(q1_ref, k1_ref, q2_ref, k2_ref, out_ref, *, lambda_init):
    a = jnp.sum(q1_ref[...] * k1_ref[...], axis=-1, keepdims=True)
    b = jnp.sum(q2_ref[...] * k2_ref[...], axis=-1, keepdims=True)
    out_ref[...] = jnp.broadcast_to(jnp.exp(a) - jnp.exp(b) + lambda_init, out_ref.shape)


def _lambda(q1, k1, q2, k2, lambda_init):
    out = pl.pallas_call(
        functools.partial(_lambda_kernel, lambda_init=lambda_init),
        out_shape=jax.ShapeDtypeStruct((1, _LANES), _F32),
        name="lambda_scalar",
    )(q1, k1, q2, k2)
    return out[0, :1]


def _attention(bound_ok, lam, qT, k, vT, bias_tiles, gsub_col, *, tile, out_scale):
    B, d_att, S = qT.shape
    v_dim = 2 * _ATT_HEAD_DIM
    n_heads = d_att // v_dim
    assert S % (2 * tile) == 0
    row = lambda: pltpu.VMEM((1, 2 * tile), _F32)
    s_buf = lambda: pltpu.VMEM((tile, 2 * tile), _F32)
    p_buf = lambda: pltpu.VMEM((tile, 2 * tile), _BF16)
    rhs = pltpu.VMEM((v_dim, 2 * tile), _BF16)
    acc = pltpu.VMEM((v_dim, 2 * tile), _F32)

    out_shape = jax.ShapeDtypeStruct((B, n_heads, S, v_dim), _BF16)
    operands = (lam, qT, k, vT, bias_tiles, gsub_col)

    def bounded():
        return pl.pallas_call(
            functools.partial(_attention_bounded_kernel, tile=tile, out_scale=out_scale),
            grid=(n_heads, B),
            in_specs=[pl.BlockSpec(memory_space=pltpu.SMEM),
                      pl.BlockSpec((1, v_dim, S), lambda h, b: (b, h, 0)),
                      pl.BlockSpec((1, S, v_dim), lambda h, b: (b, 0, h)),
                      pl.BlockSpec((1, v_dim, S), lambda h, b: (b, h, 0)),
                      pl.BlockSpec((5, 1, tile, 2 * tile), lambda h, b: (0, h, 0, 0)),
                      pl.BlockSpec((v_dim, 1), lambda h, b: (0, 0))],
            out_specs=pl.BlockSpec((1, 1, S, v_dim), lambda h, b: (b, h, 0, 0)),
            out_shape=out_shape,
            scratch_shapes=[pltpu.VMEM((2, v_dim, 2 * tile), _BF16), p_buf(), p_buf(),
                            pltpu.VMEM((2, 1, 2 * tile), _F32), acc],
            compiler_params=_cparams("parallel", "parallel"),
            name="diff_attention_bounded",
        )(*operands)

    def exact():
        return pl.pallas_call(
            functools.partial(_attention_kernel, tile=tile, out_scale=out_scale),
            grid=(n_heads, B, S // tile),
            in_specs=[pl.BlockSpec(memory_space=pltpu.SMEM),
                      pl.BlockSpec((1, v_dim, tile), lambda h, b, i: (b, h, i)),
                      pl.BlockSpec((1, S, v_dim), lambda h, b, i: (b, 0, h)),
                      pl.BlockSpec((1, v_dim, S), lambda h, b, i: (b, h, 0)),
                      pl.BlockSpec((5, 1, tile, 2 * tile), lambda h, b, i: (0, h, 0, 0)),
                      pl.BlockSpec((v_dim, 1), lambda h, b, i: (0, 0))],
            out_specs=pl.BlockSpec((1, 1, tile, v_dim), lambda h, b, i: (b, h, i, 0)),
            out_shape=out_shape,
            scratch_shapes=[rhs, s_buf(), s_buf(), p_buf(), p_buf(), row(), row(), row(), row(), row(),
                            row(), acc],
            compiler_params=_cparams("parallel", "parallel", "parallel"),
            name="diff_attention",
        )(*operands)

    return lax.cond(bound_ok, bounded, exact)


def _outproj_kernel(oatt_ref, b_ref, u_ref, uprev_ref, unext_ref, x_ref, cw_ref, wout_ref, gffn_ref,
                    wr_ref, br_ref, x1_ref, tw_ref, route_ref, *, d_att, n_experts):
    i = pl.program_id(1)
    n_i = pl.num_programs(1)
    u = u_ref[0].astype(_F32)
    tm = u.shape[0]
    prev_row = jnp.where(i > 0, uprev_ref[0, _BF16_SUBLANES - 1:_BF16_SUBLANES, :].astype(_F32), 0.0)
    next_row = jnp.where(i < n_i - 1, unext_ref[0, 0:1, :].astype(_F32), 0.0)
    rows = lax.broadcasted_iota(jnp.int32, (tm, 1), 0)
    u_prev = jnp.where(rows == 0, prev_row, pltpu.roll(u, 1, axis=0))
    u_next = jnp.where(rows == tm - 1, next_row, pltpu.roll(u, tm - 1, axis=0))
    conv = cw_ref[0:1, :] * u_prev + cw_ref[1:2, :] * u + cw_ref[2:3, :] * u_next
    o_conv = (b_ref[0].astype(_F32) * conv).astype(_BF16)
    o_att = jnp.concatenate([oatt_ref[0, h] for h in range(oatt_ref.shape[1])], axis=1)
    mix = _dot(o_att, wout_ref[:d_att, :]) + _dot(o_conv, wout_ref[d_att:, :])
    x1 = x_ref[0] + mix
    x1_ref[0] = x1
    ms = jnp.mean(x1 * x1, axis=-1, keepdims=True)
    t = x1 * lax.rsqrt(ms + _EPS) * gffn_ref[...]
    t_hi = t.astype(_BF16)
    half = x1.shape[1] // 2
    tw_ref[0, :, :half] = _pack_bf16_pairs(t)
    t_lo = (t - t_hi.astype(_F32)).astype(_BF16)
    a_hi = _dot(t_hi, wr_ref[...])
    a_lo = _dot(t_lo, wr_ref[...])
    logits = a_hi[:, :_LANES] + a_hi[:, _LANES:] + a_lo[:, :_LANES] + br_ref[...]
    n_groups = n_experts // _EXPERTS_PER_GROUP
    lane = lax.broadcasted_iota(jnp.int32, (1, _LANES), 1)
    lane_f = lane.astype(_F32)
    big = float(_LANES)
    gmask = (lane >= n_experts) & (lane < n_experts + n_groups)
    gl = jnp.where(gmask, logits, _NEG_BIG)
    gmax = jnp.max(gl, axis=-1, keepdims=True)
    gsum = jnp.sum(jnp.where(gmask, jnp.exp(gl - gmax), 0.0), axis=-1, keepdims=True)
    g_w = 1.0 / gsum
    g_idx = jnp.min(jnp.where(gmask & (gl == gmax), lane_f - n_experts, big), axis=-1, keepdims=True)
    lo = g_idx * _EXPERTS_PER_GROUP
    emask = (lane_f >= lo) & (lane_f < lo + _EXPERTS_PER_GROUP)
    el = jnp.where(emask, logits, _NEG_BIG)
    emax = jnp.max(el, axis=-1, keepdims=True)
    ep = jnp.where(emask, jnp.exp(el - emax), 0.0)
    p_exp = ep / jnp.sum(ep, axis=-1, keepdims=True)
    top1 = jnp.max(p_exp, axis=-1, keepdims=True)
    i1 = jnp.min(jnp.where(emask & (p_exp == top1), lane_f, big), axis=-1, keepdims=True)
    rest = jnp.where(emask & (lane_f != i1), p_exp, -1.0)
    top2 = jnp.max(rest, axis=-1, keepdims=True)
    i2 = jnp.min(jnp.where(rest == top2, lane_f, big), axis=-1, keepdims=True)
    denom = top1 + top2
    gates = jnp.where(lane_f == i1 - lo, g_w * (top1 / denom),
                      jnp.where(lane_f == i2 - lo, g_w * (top2 / denom), 0.0))
    tw_ref[0, :, half:] = lax.bitcast_convert_type(gates, jnp.uint32)
    route_ref[0] = jnp.broadcast_to(g_idx, (tm, _LANES)).T[:_F32_SUBLANES, :]


def _outproj(oatt, b, u, x, conv_w, w_out, gffn, wr, br, *, n_experts, tm):
    B, S, D = x.shape
    n_heads, v_dim = oatt.shape[1], oatt.shape[3]
    d_att = n_heads * v_dim
    d_conv = b.shape[-1]
    hb = _BF16_SUBLANES
    per_tile = tm // hb
    n_halo = S // hb
    tok = lambda width: pl.BlockSpec((1, tm, width), lambda bb, i: (bb, i, 0))
    full = lambda shape: pl.BlockSpec(shape, lambda bb, i: (0,) * len(shape))
    return pl.pallas_call(
        functools.partial(_outproj_kernel, d_att=d_att, n_experts=n_experts),
        grid=(B, S // tm),
        in_specs=[pl.BlockSpec((1, n_heads, tm, v_dim), lambda bb, i: (bb, 0, i, 0)), tok(d_conv), tok(d_conv),
                  pl.BlockSpec((1, hb, d_conv), lambda bb, i: (bb, jnp.maximum(i * per_tile - 1, 0), 0)),
                  pl.BlockSpec((1, hb, d_conv),
                               lambda bb, i: (bb, jnp.minimum((i + 1) * per_tile, n_halo - 1), 0)),
                  tok(D), full(conv_w.shape), full(w_out.shape), full((1, D)), full(wr.shape),
                  full((1, _LANES))],
        out_specs=[tok(D), tok(D // 2 + _LANES),
                   pl.BlockSpec((1, _F32_SUBLANES, tm), lambda bb, i: (bb, 0, i))],
        out_shape=[jax.ShapeDtypeStruct((B, S, D), _F32),
                   jax.ShapeDtypeStruct((B, S, D // 2 + _LANES), jnp.uint32),
                   jax.ShapeDtypeStruct((B, _F32_SUBLANES, S), _F32)],
        compiler_params=_cparams("parallel", "parallel"),
        name="outproj_router",
    )(oatt, b, u, u, u, x, conv_w, w_out, gffn, wr, br)


def _row_copies_wait(src_rows, dst_rows, sem):
    pltpu.make_async_copy(src_rows, dst_rows, sem).wait()


def _stage_indices(idx_ref, idx_smem, slot, idx_sem):
    c = pltpu.make_async_copy(idx_ref.at[0, 0], idx_smem.at[slot], idx_sem)
    c.start()
    c.wait()


def _scatter_rows_kernel(pads_ref, pos_ref, x_ref, out_hbm, idx_smem, idx_sem, sem, *, n_groups):
    i = pl.program_id(0)
    tm = x_ref.shape[0]
    _stage_indices(pos_ref, idx_smem, 0, idx_sem)
    for r in range(tm):
        pltpu.make_async_copy(x_ref.at[pl.ds(r, 1), :], out_hbm.at[pl.ds(idx_smem[0, r], 1), :],
                              sem).start(priority=r % _DMA_PRIORITIES)
    _row_copies_wait(x_ref, out_hbm.at[pl.ds(0, tm), :], sem)

    @pl.when(i == pl.num_programs(0) - 1)
    def _():
        for g in range(n_groups):
            lo = pads_ref[g]
            count = pads_ref[n_groups + g] - lo

            def fill(k, carry, lo=lo):
                pltpu.make_async_copy(x_ref.at[pl.ds(0, 1), :], out_hbm.at[pl.ds(lo + k, 1), :], sem).start()
                return carry

            def drain(k, carry):
                _row_copies_wait(x_ref.at[pl.ds(0, 1), :], out_hbm.at[pl.ds(0, 1), :], sem)
                return carry

            lax.fori_loop(0, count, fill, 0)
            lax.fori_loop(0, count, drain, 0)


def _scatter_rows(pads, pos, x, n_out_rows, *, tm):
    T, width = x.shape
    n_steps = T // tm
    return pl.pallas_call(
        functools.partial(_scatter_rows_kernel, n_groups=pads.shape[0] // 2),
        grid_spec=pltpu.PrefetchScalarGridSpec(
            num_scalar_prefetch=1,
            grid=(n_steps,),
            in_specs=[pl.BlockSpec((1, 1, tm), lambda i, pads: (i, 0, 0)),
                      pl.BlockSpec((tm, width), lambda i, pads: (i, 0))],
            out_specs=pl.BlockSpec(memory_space=pl.ANY),
            scratch_shapes=[pltpu.SMEM((1, tm), jnp.int32), pltpu.SemaphoreType.DMA,
                            pltpu.SemaphoreType.DMA]),
        out_shape=jax.ShapeDtypeStruct((n_out_rows, width), x.dtype),
        compiler_params=_cparams("arbitrary"),
        name="dispatch_rows",
    )(pads, pos.reshape(n_steps, 1, tm), x)


def _moe_sorted_kernel(tile_group_ref, tw_ref, wgu_ref, wdn_ref, out_ref, *, d_expert):
    del tile_group_ref
    half = out_ref.shape[1]
    t = _unpack_bf16_pairs(tw_ref[:, :half]).astype(_BF16)
    gates = lax.bitcast_convert_type(tw_ref[:, half:], _F32)
    parts = []
    for e in range(_EXPERTS_PER_GROUP):
        gu = _dot(t, wgu_ref[e])
        g_lin = gu[:, :d_expert]
        act = g_lin * jax.nn.sigmoid(g_lin) * gu[:, d_expert:]
        parts.append((act * gates[:, e:e + 1]).astype(_BF16))
    out_ref[...] = _pack_bf16_pairs(_dot(jnp.concatenate(parts, axis=1), wdn_ref[0]))


def _moe_sorted(tile_group, tw_sorted, wgu, wdn, *, d_expert, tm):
    rows, width = tw_sorted.shape
    half = width - _LANES
    grp = lambda shape: pl.BlockSpec((1,) + shape, lambda i, tg: (tg[i], 0, 0))
    return pl.pallas_call(
        functools.partial(_moe_sorted_kernel, d_expert=d_expert),
        grid_spec=pltpu.PrefetchScalarGridSpec(
            num_scalar_prefetch=1,
            grid=(rows // tm,),
            in_specs=[pl.BlockSpec((tm, width), lambda i, tg: (i, 0)),
                      pl.BlockSpec((_EXPERTS_PER_GROUP,) + wgu.shape[1:], lambda i, tg: (tg[i], 0, 0)),
                      grp(wdn.shape[1:])],
            out_specs=pl.BlockSpec((tm, half), lambda i, tg: (i, 0))),
        out_shape=jax.ShapeDtypeStruct((rows, half), jnp.uint32),
        compiler_params=_cparams("parallel"),
        name="moe_sorted",
    )(tile_group, tw_sorted, wgu, wdn)


def _ple_kernel(idx_first_ref, idx_b_ref, idx_a_ref, moe_hbm, x1_ref, p_ref, wproj_ref, wgate_ref,
                gple_ref, y_ref, idx_smem, idx_sem, buf, sem):
    j = pl.program_id(0)
    tm = buf.shape[1]

    def rows_start(idx_slot, slot, lo=0, hi=None):
        for r in range(lo, tm if hi is None else hi):
            pltpu.make_async_copy(moe_hbm.at[pl.ds(idx_smem[idx_slot, r], 1), :],
                                  buf.at[slot, pl.ds(r, 1), :], sem.at[slot]).start(
                                      priority=r % _DMA_PRIORITIES)

    def rows_wait(slot):
        _row_copies_wait(moe_hbm.at[pl.ds(0, tm), :], buf.at[slot], sem.at[slot])

    def compute_rows(base, slot, lo, hi):
        rows = slice(base + lo, base + hi)
        x2 = x1_ref[rows, :] + _unpack_bf16_pairs(buf[slot, lo:hi, :])
        e_raw = _dot(p_ref[rows, :].astype(_BF16), wproj_ref[...])
        ms = jnp.mean(e_raw * e_raw, axis=-1, keepdims=True)
        emb = e_raw * lax.rsqrt(ms + _EPS) * gple_ref[...]
        gate_p = jax.nn.sigmoid(_dot(x2.astype(_BF16), wgate_ref[...]))
        y_ref[rows, :] = x2 + gate_p * emb

    def compute_and_fetch(base, slot, idx_slot, other):
        for lo in range(0, tm, _PLE_CHUNK):
            rows_start(idx_slot, other, lo, lo + _PLE_CHUNK)
            compute_rows(base, slot, lo, lo + _PLE_CHUNK)

    @pl.when(j == 0)
    def _():
        _stage_indices(idx_first_ref, idx_smem, 0, idx_sem)
        rows_start(0, 0)

    _stage_indices(idx_b_ref, idx_smem, 0, idx_sem)
    _stage_indices(idx_a_ref, idx_smem, 1, idx_sem)
    rows_wait(0)
    compute_and_fetch(0, 0, 0, 1)
    rows_wait(1)
    compute_and_fetch(tm, 1, 1, 0)

    @pl.when(j == pl.num_programs(0) - 1)
    def _():
        rows_wait(0)


def _ple(pos, moe_sorted, x1, p, wproj, wgate, gple, *, tm):
    T, D = x1.shape
    half = moe_sorted.shape[1]
    n_tiles = T // tm
    assert n_tiles % 2 == 0
    idx = pos.reshape(n_tiles, 1, tm)
    tok = lambda width: pl.BlockSpec((2 * tm, width), lambda j: (j, 0))
    full = lambda shape: pl.BlockSpec(shape, lambda j: (0,) * len(shape))
    idx_spec = lambda tile_of_step: pl.BlockSpec((1, 1, tm), lambda j: (tile_of_step(j), 0, 0))
    return pl.pallas_call(
        _ple_kernel,
        grid=(n_tiles // 2,),
        in_specs=[idx_spec(lambda j: 0), idx_spec(lambda j: 2 * j + 1),
                  idx_spec(lambda j: jnp.minimum(2 * j + 2, n_tiles - 1)),
                  pl.BlockSpec(memory_space=pl.ANY), tok(D), tok(p.shape[-1]), full(wproj.shape),
                  full(wgate.shape), full((1, D))],
        out_specs=tok(D),
        out_shape=jax.ShapeDtypeStruct((T, D), _F32),
        scratch_shapes=[pltpu.SMEM((2, tm), jnp.int32), pltpu.SemaphoreType.DMA,
                        pltpu.VMEM((2, tm, half), jnp.uint32), pltpu.SemaphoreType.DMA((2,))],
        compiler_params=_cparams("arbitrary"),
        name="ple_gate",
    )(idx, idx, idx, moe_sorted, x1, p, wproj, wgate, gple)


def _dispatch_plan(group_of_token, n_groups, tm):
    T = group_of_token.shape[0]
    n_tiles = T // tm + n_groups - 1
    onehot = (group_of_token[:, None] == jnp.arange(n_groups, dtype=jnp.int32)[None, :]).astype(jnp.int32)
    rank = jnp.take_along_axis(jnp.cumsum(onehot, axis=0), group_of_token[:, None], axis=1)[:, 0] - 1
    count = jnp.sum(onehot, axis=0)
    tiles_per_group = (count + tm - 1) // tm
    tile_end = jnp.cumsum(tiles_per_group)
    tile_start = tile_end - tiles_per_group
    pos = tile_start[group_of_token] * tm + rank
    tile_group = jnp.searchsorted(tile_end, jnp.arange(n_tiles, dtype=jnp.int32), side="right")
    tile_group = jnp.minimum(tile_group, n_groups - 1).astype(jnp.int32)
    pad_lo = tile_start * tm + count
    pad_hi = (tile_end * tm).at[n_groups - 1].set(n_tiles * tm)
    pads = jnp.concatenate([pad_lo, pad_hi]).astype(jnp.int32)
    return pos.astype(jnp.int32), tile_group, pads, n_tiles * tm


def _prepare_layer(i, norm_mix, w_in, q_norm, k_norm, lambda_q1, lambda_k1, lambda_q2, lambda_k2,
                   attn_sub_norm, conv_w, w_out, norm_ffn, w_group, b_group, w_erouter, b_erouter,
                   w_gate_up, w_down, w_ple_proj, w_ple_gate, ple_norm):
    D = w_in.shape[1]
    d_mix = w_out.shape[1]
    d_att = d_mix // 2
    d_conv = d_mix - d_att
    n_maps = d_att // _ATT_HEAD_DIM
    n_experts = w_gate_up.shape[1]
    n_groups = w_group.shape[-1]
    d_expert = w_down.shape[2]
    assert n_experts == n_groups * _EXPERTS_PER_GROUP and n_experts + n_groups <= _LANES
    lambda_init = 0.8 - 0.6 * math.exp(-0.3 * i)
    head_of = jnp.arange(d_att) // _ATT_HEAD_DIM
    bd = jnp.where(head_of[:, None] == head_of[None, :], 1.0 / _ATT_HEAD_DIM, 0.0).astype(_BF16)
    wr = jnp.zeros((D, _LANES), _F32)
    wr = wr.at[:, :n_experts].set(w_erouter[i]).at[:, n_experts:n_experts + n_groups].set(w_group[i])
    wr_hi = wr.astype(_BF16)
    wr_lo = (wr - wr_hi.astype(_F32)).astype(_BF16)
    br = jnp.zeros((1, _LANES), _F32)
    br = br.at[0, :n_experts].set(b_erouter[i]).at[0, n_experts:n_experts + n_groups].set(b_group[i])
    wgu = w_gate_up[i]
    wdn = w_down[i].reshape(n_groups, _EXPERTS_PER_GROUP * d_expert, D)
    return dict(
        d_att=d_att, d_conv=d_conv, n_experts=n_experts, d_expert=d_expert, lambda_init=lambda_init,
        gmix=norm_mix[i][None, :], w_in=w_in[i].astype(_BF16),
        gq_col=q_norm[i][:, None], gk_row=jnp.tile(k_norm[i], n_maps)[None, :], bd=bd,
        lam_vecs=tuple(v[i][None, :] for v in (lambda_q1, lambda_k1, lambda_q2, lambda_k2)),
        gsub_col=attn_sub_norm[i][:, None], conv_w=conv_w[i], w_out=w_out[i].astype(_BF16),
        gffn=norm_ffn[i][None, :], wr=jnp.concatenate([wr_hi, wr_lo], axis=1), br=br,
        wgu=wgu.astype(_BF16), wdn=wdn.astype(_BF16), wproj=w_ple_proj[i].astype(_BF16),
        wgate=w_ple_gate[i].astype(_BF16), gple=ple_norm[i][None, :])


def _score_bound(q_gain, k_gain, rel_bias):
    hd = _ATT_HEAD_DIM
    rounding_slack = 1.02
    q_norm_max = math.sqrt(hd) * jnp.max(jnp.abs(q_gain)) * (_LOG2E / math.sqrt(hd))
    k_norm_max = math.sqrt(hd) * jnp.max(jnp.abs(k_gain))
    qk = rounding_slack * q_norm_max * k_norm_max
    b2 = rel_bias.astype(_F32) * _LOG2E
    b_max, b_min = jnp.max(b2, axis=0), jnp.min(b2, axis=0)
    ok = jnp.all(2.0 * qk + (b_max - b_min) <= _SAFE_EXPONENT_SPAN)
    ok = ok & jnp.isfinite(qk) & jnp.all(jnp.isfinite(b2))
    return ok, jnp.where(ok, qk + b_max, jnp.zeros_like(b_max))


def _layer(x, p_i, L, bias_tiles, bound_ok, lam):
    B, S, D = x.shape
    tm = _TOKEN_TILE
    tile = _ATTN_TILE
    assert S % tm == 0 and S % tile == 0 and tm % _BF16_SUBLANES == 0
    qT, k, vT, b, u = _inproj(x, L["gmix"], L["w_in"], L["gq_col"], L["gk_row"], L["bd"],
                              d_att=L["d_att"], d_conv=L["d_conv"], tm=tm)
    oatt = _attention(bound_ok, lam, qT, k, vT, bias_tiles, L["gsub_col"], tile=tile,
                      out_scale=1.0 - L["lambda_init"])
    x1, tw, route = _outproj(oatt, b, u, x, L["conv_w"], L["w_out"], L["gffn"], L["wr"], L["br"],
                             n_experts=L["n_experts"], tm=tm)
    x1 = x1.reshape(B * S, D)
    tw = tw.reshape(B * S, D // 2 + _LANES)
    group_of_token = route[:, 0, :].reshape(B * S).astype(jnp.int32)
    pos, tile_group, pads, sorted_rows = _dispatch_plan(group_of_token, L["wdn"].shape[0], tm)
    tw_sorted = _scatter_rows(pads, pos, tw, sorted_rows, tm=tm)
    moe = _moe_sorted(tile_group, tw_sorted, L["wgu"], L["wdn"], d_expert=L["d_expert"], tm=tm)
    y = _ple(pos, moe, x1, p_i.reshape(B * S, -1), L["wproj"], L["wgate"], L["gple"], tm=tm)
    return y.reshape(B, S, D)


def kernel(x_prompt, x_sample, p_prompt, p_sample, norm_mix, w_in, q_norm, k_norm, lambda_q1, lambda_k1, lambda_q2, lambda_k2, attn_sub_norm, conv_w, w_out, rel_bias, norm_ffn, w_group, b_group, w_erouter, b_erouter, w_gate_up, w_down, w_ple_proj, w_ple_gate, ple_norm):
    depth = w_in.shape[0]
    layers = []
    for i in range(depth):
        L = _prepare_layer(i, norm_mix, w_in, q_norm, k_norm, lambda_q1, lambda_k1, lambda_q2,
                           lambda_k2, attn_sub_norm, conv_w, w_out, norm_ffn, w_group, b_group,
                           w_erouter, b_erouter, w_gate_up, w_down, w_ple_proj, w_ple_gate, ple_norm)
        bound_ok, shift = _score_bound(q_norm[i], k_norm[i], rel_bias)
        layers.append((L, _bias_tiles(rel_bias, shift, _ATTN_TILE), bound_ok,
                       _lambda(*L["lam_vecs"], L["lambda_init"])))

    def encode(x, p):
        for i, (L, bias_tiles, bound_ok, lam) in enumerate(layers):
            x = _layer(x, p[i], L, bias_tiles, bound_ok, lam)
        return x

    return (encode(x_prompt, p_prompt), encode(x_sample, p_sample))
```

```python
import functools
import math

import jax
import jax.numpy as jnp
from jax import lax
from jax.experimental import pallas as pl
from jax.experimental.pallas import tpu as pltpu

_F32 = jnp.float32
_BF16 = jnp.bfloat16

_EPS = 1e-6
_ATT_HEAD_DIM = 64
_MAX_DISTANCE = 128
_EXPERTS_PER_GROUP = 4
_LANES = 128
_F32_SUBLANES = 8
_BF16_SUBLANES = 16
_ATTN_TILE = 512
_ATTN_ROWS = 128
_ATTN_LANES = 256
_TOKEN_TILE = 512
_PLE_CHUNK = 128
_DMA_PRIORITIES = 2
_VMEM_LIMIT_BYTES = 56 * 1024 * 1024
_SAFE_EXPONENT_SPAN = 100.0
_NEG_BIG = -1e30
_LOG2E = math.log2(math.e)


def _cparams(*sem):
    return pltpu.CompilerParams(dimension_semantics=sem, vmem_limit_bytes=_VMEM_LIMIT_BYTES)


def _dot(a, b):
    return jnp.dot(a, b, preferred_element_type=_F32)


def _pack_bf16_pairs(x):
    n = x.shape[1] // 2
    bits = lax.bitcast_convert_type(x.astype(_BF16).astype(_F32), jnp.uint32)
    return (bits[:, :n] & jnp.uint32(0xFFFF0000)) | (bits[:, n:] >> 16)


def _unpack_bf16_pairs(u):
    hi = lax.bitcast_convert_type(u & jnp.uint32(0xFFFF0000), _F32)
    lo = lax.bitcast_convert_type(u << 16, _F32)
    return jnp.concatenate([hi, lo], axis=1)


def _bias_tile_kernel(tbl_ref, shift_ref, out_ref, *, tile, num_buckets):
    di = pl.program_id(0)
    m = pl.program_id(1)
    half = num_buckets // 2
    max_exact = half // 2
    is_far = (di == 0) | (di == 4)

    @pl.when(is_far)
    def _():
        b_far = jnp.where(di == 0, half - 1, num_buckets - 1)
        out_ref[0, 0] = jnp.full((tile, tile), tbl_ref[b_far, m] * _LOG2E - shift_ref[m], _F32)

    @pl.when(jnp.logical_not(is_far))
    def _():
        delta = (di - 2) * tile
        kk = lax.broadcasted_iota(jnp.int32, (tile, tile), 0)
        qq = lax.broadcasted_iota(jnp.int32, (tile, tile), 1)
        rel = kk - qq + delta
        ret = jnp.where(rel > 0, half, 0)
        n = jnp.abs(rel)
        nf = jnp.maximum(n, 1).astype(_F32)
        large = max_exact + (jnp.log(nf / max_exact) / math.log(_MAX_DISTANCE / max_exact)
                             * (half - max_exact)).astype(jnp.int32)
        large = jnp.minimum(large, half - 1)
        bucket = ret + jnp.where(n < max_exact, n, large)
        acc = jnp.zeros((tile, tile), _F32)
        for b in range(num_buckets):
            acc = jnp.where(bucket == b, tbl_ref[b, m], acc)
        out_ref[0, 0] = acc * _LOG2E - shift_ref[m]


def _bias_tiles(rel_bias, shift, tile):
    num_buckets, n_maps = rel_bias.shape
    assert tile >= _MAX_DISTANCE
    return pl.pallas_call(
        functools.partial(_bias_tile_kernel, tile=tile, num_buckets=num_buckets),
        grid=(5, n_maps),
        in_specs=[pl.BlockSpec(memory_space=pltpu.SMEM), pl.BlockSpec(memory_space=pltpu.SMEM)],
        out_specs=pl.BlockSpec((1, 1, tile, tile), lambda di, m: (di, m // 2, 0, m % 2)),
        out_shape=jax.ShapeDtypeStruct((5, n_maps // 2, tile, 2 * tile), _F32),
        compiler_params=_cparams("arbitrary", "arbitrary"),
        name="bias_tiles",
    )(rel_bias.astype(_F32), shift.astype(_F32))


def _inproj_kernel(x_ref, gmix_ref, w_ref, gq_ref, gk_ref, bd_ref,
                   qT_ref, k_ref, vT_ref, b_ref, u_ref, *, d_att, d_conv, scale):
    x = x_ref[0]
    ms = jnp.mean(x * x, axis=-1, keepdims=True)
    h = (x * lax.rsqrt(ms + _EPS) * gmix_ref[...]).astype(_BF16)

    def proj(lo, n):
        return _dot(h, w_ref[:, lo:lo + n])

    tm = x.shape[0]
    n_maps = d_att // _ATT_HEAD_DIM
    zqT = proj(0, d_att).T.reshape(n_maps, _ATT_HEAD_DIM, tm)
    qms = jnp.mean(zqT * zqT, axis=1, keepdims=True)
    qn = zqT * lax.rsqrt(qms + _EPS) * gq_ref[...][None]
    qT_ref[0] = (qn * scale).reshape(d_att, tm).astype(_BF16)
    zk = proj(d_att, d_att)
    kms = _dot((zk * zk).astype(_BF16), bd_ref[...])
    k_ref[0] = (zk * lax.rsqrt(kms + _EPS) * gk_ref[...]).astype(_BF16)
    vT_ref[0] = proj(2 * d_att, d_att).T.astype(_BF16)
    b_ref[0] = proj(3 * d_att, d_conv).astype(_BF16)
    c = proj(3 * d_att + d_conv, d_conv)
    xc = proj(3 * d_att + 2 * d_conv, d_conv)
    u_ref[0] = (c * xc).astype(_BF16)


def _inproj(x, gmix, w_in, gq_col, gk_row, bd, *, d_att, d_conv, tm):
    B, S, D = x.shape
    d_in = w_in.shape[1]
    grid = (B, S // tm)
    tok = lambda width: pl.BlockSpec((1, tm, width), lambda b, i: (b, i, 0))
    tr = lambda rows: pl.BlockSpec((1, rows, tm), lambda b, i: (b, 0, i))
    full = lambda shape: pl.BlockSpec(shape, lambda b, i: (0,) * len(shape))
    return pl.pallas_call(
        functools.partial(_inproj_kernel, d_att=d_att, d_conv=d_conv,
                          scale=_LOG2E / math.sqrt(_ATT_HEAD_DIM)),
        grid=grid,
        in_specs=[tok(D), full((1, D)), full((D, d_in)), full((_ATT_HEAD_DIM, 1)), full((1, d_att)),
                  full((d_att, d_att))],
        out_specs=[tr(d_att), tok(d_att), tr(d_att), tok(d_conv), tok(d_conv)],
        out_shape=[jax.ShapeDtypeStruct((B, d_att, S), _BF16),
                   jax.ShapeDtypeStruct((B, S, d_att), _BF16),
                   jax.ShapeDtypeStruct((B, d_att, S), _BF16),
                   jax.ShapeDtypeStruct((B, S, d_conv), _BF16),
                   jax.ShapeDtypeStruct((B, S, d_conv), _BF16)],
        compiler_params=_cparams("parallel", "parallel"),
        name="inproj",
    )(x, gmix, w_in, gq_col, gk_row, bd)


def _attention_kernel(lam_ref, qT_ref, k_ref, vT_ref, bias_ref, gsub_ref, o_ref,
                      rhs_ref, s0_ref, s1_ref, p0_ref, p1_ref, mx0_ref, mx1_ref, al0_ref, al1_ref,
                      m_ref, l_ref, acc_ref, *, tile, out_scale):
    qi = pl.program_id(2)
    n_k = k_ref.shape[1] // tile
    hd = _ATT_HEAD_DIM
    s_refs, p_refs = (s0_ref, s1_ref), (p0_ref, p1_ref)
    mx_refs, al_refs = (mx0_ref, mx1_ref), (al0_ref, al1_ref)
    qT = qT_ref[0]
    row = lax.broadcasted_iota(jnp.int32, qT.shape, 0)
    zero = jnp.zeros_like(qT)
    rhs_ref[:, :tile] = jnp.where(row < hd, qT, zero)
    rhs_ref[:, tile:] = jnp.where(row >= hd, qT, zero)
    m_ref[...] = jnp.full(m_ref.shape, _NEG_BIG, _F32)
    l_ref[...] = jnp.zeros(l_ref.shape, _F32)
    acc_ref[...] = jnp.zeros(acc_ref.shape, _F32)

    row_blocks = [slice(r, r + _ATTN_ROWS) for r in range(0, tile, _ATTN_ROWS)]
    lane_blocks = [slice(c, c + _ATTN_LANES) for c in range(0, 2 * tile, _ATTN_LANES)]

    def fold8(x, op):
        return op(x.reshape(x.shape[0] // 8, 8, x.shape[1]), axis=0)

    def stage_a(kj, par):
        off = pl.multiple_of(kj * tile, tile)
        bidx = jnp.clip(kj - qi, -2, 2) + 2
        for cs in lane_blocks:
            rhs_c = rhs_ref[:, cs]
            mx = None
            for rs in row_blocks:
                k_blk = k_ref[0, pl.ds(off + rs.start, _ATTN_ROWS), :]
                s = _dot(k_blk, rhs_c) + bias_ref[bidx, 0, rs, cs]
                s_refs[par][rs, cs] = s
                part = fold8(s, jnp.max)
                mx = part if mx is None else jnp.maximum(mx, part)
            mx_refs[par][:, cs] = jnp.max(mx, axis=0, keepdims=True)

    def stage_b(par):
        m_old = m_ref[...]
        m_new = jnp.maximum(m_old, mx_refs[par][...])
        alpha = jnp.exp2(m_old - m_new)
        al_refs[par][...] = alpha
        m_ref[...] = m_new
        for cs in lane_blocks:
            m_c = m_new[:, cs]
            lsum = None
            for rs in row_blocks:
                p = jnp.exp2(s_refs[par][rs, cs] - m_c)
                p_refs[par][rs, cs] = p.astype(_BF16)
                part = fold8(p, jnp.sum)
                lsum = part if lsum is None else lsum + part
            l_ref[:, cs] = alpha[:, cs] * l_ref[:, cs] + jnp.sum(lsum, axis=0, keepdims=True)

    def stage_c(kj, par):
        off = pl.multiple_of(kj * tile, tile)
        for cs in lane_blocks:
            pv = _dot(vT_ref[0, :, pl.ds(off, tile)], p_refs[par][:, cs])
            acc_ref[:, cs] = al_refs[par][:, cs] * acc_ref[:, cs] + pv

    stage_a(0, 0)
    stage_a(1, 1)
    stage_b(0)

    def pair(t, carry):
        j = 2 * t + 1
        stage_a(j + 1, 0)
        stage_b(1)
        stage_c(j - 1, 0)
        stage_a(j + 2, 1)
        stage_b(0)
        stage_c(j, 1)
        return carry

    lax.fori_loop(0, (n_k - 2) // 2, pair, 0)
    stage_b(1)
    stage_c(n_k - 2, 0)
    stage_c(n_k - 1, 1)

    o_ref[0, 0] = _attention_output(lam_ref, l_ref[...], acc_ref[...], gsub_ref, tile=tile,
                                    out_scale=out_scale)


def _attention_output(lam_ref, l, acc, gsub_ref, *, tile, out_scale):
    o = acc * (1.0 / l)
    oT = o[:, :tile] - lam_ref[0] * o[:, tile:]
    ms = jnp.mean(oT * oT, axis=0, keepdims=True)
    oT = oT * lax.rsqrt(ms + _EPS) * gsub_ref[...] * out_scale
    return oT.T.astype(_BF16)


def _attention_bounded_kernel(lam_ref, qT_ref, k_ref, vT_ref, bias_ref, gsub_ref, o_ref,
                              rhs_ref, p0_ref, p1_ref, l_ref, acc_ref, *, tile, out_scale):
    seq = k_ref.shape[1]
    n_k = seq // tile
    hd = _ATT_HEAD_DIM
    p_refs = (p0_ref, p1_ref)
    row_blocks = [slice(r, r + _ATTN_ROWS) for r in range(0, tile, _ATTN_ROWS)]
    lane_blocks = [slice(c, c + _ATTN_LANES) for c in range(0, 2 * tile, _ATTN_LANES)]

    def setup_q(qi, q_load):
        qpar = qi % 2
        qT = qT_ref[0, :, pl.ds(pl.multiple_of(q_load * tile, tile), tile)]
        row = lax.broadcasted_iota(jnp.int32, qT.shape, 0)
        zero = jnp.zeros_like(qT)
        rhs_ref[qpar, :, :tile] = jnp.where(row < hd, qT, zero)
        rhs_ref[qpar, :, tile:] = jnp.where(row >= hd, qT, zero)
        l_ref[qpar] = jnp.zeros(l_ref.shape[1:], _F32)

    def stage_ab(qi, kj, par):
        qpar = qi % 2
        off = pl.multiple_of(kj * tile, tile)
        bidx = jnp.clip(kj - qi, -2, 2) + 2
        k_tile = k_ref[0, pl.ds(off, tile), :]
        for cs in lane_blocks:
            s = _dot(k_tile, rhs_ref[qpar, :, cs])
            lsum = None
            for rs in row_blocks:
                p = jnp.exp2(s[rs, :] + bias_ref[bidx, 0, rs, cs])
                p_refs[par][rs, cs] = p.astype(_BF16)
                part = jnp.sum(p.reshape(_ATTN_ROWS // 8, 8, _ATTN_LANES), axis=0)
                lsum = part if lsum is None else lsum + part
            l_ref[qpar, :, cs] += jnp.sum(lsum, axis=0, keepdims=True)

    def stage_c(kj, par):
        off = pl.multiple_of(kj * tile, tile)
        for cs in lane_blocks:
            acc_ref[:, cs] += _dot(vT_ref[0, :, pl.ds(off, tile)], p_refs[par][:, cs])

    n_q = seq // tile
    acc_ref[...] = jnp.zeros(acc_ref.shape, _F32)
    setup_q(0, 0)
    stage_ab(0, 0, 0)

    def q_tile_body(qi, carry):
        def pair(t, c):
            j = 2 * t
            stage_ab(qi, j + 1, 1)
            stage_c(j, 0)
            stage_ab(qi, j + 2, 0)
            stage_c(j + 1, 1)
            return c

        lax.fori_loop(0, n_k // 2 - 1, pair, 0)
        stage_ab(qi, n_k - 1, 1)
        stage_c(n_k - 2, 0)
        q_next = qi + 1
        setup_q(q_next, jnp.minimum(q_next, n_q - 1))
        stage_ab(q_next, 0, 0)
        stage_c(n_k - 1, 1)
        o_ref[0, 0, pl.ds(pl.multiple_of(qi * tile, tile), tile), :] = _attention_output(
            lam_ref, l_ref[qi % 2], acc_ref[...], gsub_ref, tile=tile, out_scale=out_scale)
        acc_ref[...] = jnp.zeros(acc_ref.shape, _F32)
        return carry

    lax.fori_loop(0, n_q, q_tile_body, 0)


def _lambda_kernel(q1_ref, k1_ref, q2_ref, k2_ref, out_ref, *, lambda_init):
    a = jnp.sum(q1_ref[...] * k1_ref[...], axis=-1, keepdims=True)
    b = jnp.sum(q2_ref[...] * k2_ref[...], axis=-1, keepdims=True)
    out_ref[...] = jnp.broadcast_to(jnp.exp(a) - jnp.exp(b) + lambda_init, out_ref.shape)


def _lambda(q1, k1, q2, k2, lambda_init):
    out = pl.pallas_call(
        functools.partial(_lambda_kernel, lambda_init=lambda_init),
        out_shape=jax.ShapeDtypeStruct((1, _LANES), _F32),
        name="lambda_scalar",
    )(q1, k1, q2, k2)
    return out[0, :1]


def _attention(bound_ok, lam, qT, k, vT, bias_tiles, gsub_col, *, tile, out_scale):
    B, d_att, S = qT.shape
    v_dim = 2 * _ATT_HEAD_DIM
    n_heads = d_att // v_dim
    assert S % (2 * tile) == 0
    row = lambda: pltpu.VMEM((1, 2 * tile), _F32)
    s_buf = lambda: pltpu.VMEM((tile, 2 * tile), _F32)
    p_buf = lambda: pltpu.VMEM((tile, 2 * tile), _BF16)
    rhs = pltpu.VMEM((v_dim, 2 * tile), _BF16)
    acc = pltpu.VMEM((v_dim, 2 * tile), _F32)

    out_shape = jax.ShapeDtypeStruct((B, n_heads, S, v_dim), _BF16)
    operands = (lam, qT, k, vT, bias_tiles, gsub_col)

    def bounded():
        return pl.pallas_call(
            functools.partial(_attention_bounded_kernel, tile=tile, out_scale=out_scale),
            grid=(n_heads, B),
            in_specs=[pl.BlockSpec(memory_space=pltpu.SMEM),
                      pl.BlockSpec((1, v_dim, S), lambda h, b: (b, h, 0)),
                      pl.BlockSpec((1, S, v_dim), lambda h, b: (b, 0, h)),
                      pl.BlockSpec((1, v_dim, S), lambda h, b: (b, h, 0)),
                      pl.BlockSpec((5, 1, tile, 2 * tile), lambda h, b: (0, h, 0, 0)),
                      pl.BlockSpec((v_dim, 1), lambda h, b: (0, 0))],
            out_specs=pl.BlockSpec((1, 1, S, v_dim), lambda h, b: (b, h, 0, 0)),
            out_shape=out_shape,
            scratch_shapes=[pltpu.VMEM((2, v_dim, 2 * tile), _BF16), p_buf(), p_buf(),
                            pltpu.VMEM((2, 1, 2 * tile), _F32), acc],
            compiler_params=_cparams("parallel", "parallel"),
            name="diff_attention_bounded",
        )(*operands)

    def exact():
        return pl.pallas_call(
            functools.partial(_attention_kernel, tile=tile, out_scale=out_scale),
            grid=(n_heads, B, S // tile),
            in_specs=[pl.BlockSpec(memory_space=pltpu.SMEM),
                      pl.BlockSpec((1, v_dim, tile), lambda h, b, i: (b, h, i)),
                      pl.BlockSpec((1, S, v_dim), lambda h, b, i: (b, 0, h)),
                      pl.BlockSpec((1, v_dim, S), lambda h, b, i: (b, h, 0)),
                      pl.BlockSpec((5, 1, tile, 2 * tile), lambda h, b, i: (0, h, 0, 0)),
                      pl.BlockSpec((v_dim, 1), lambda h, b, i: (0, 0))],
            out_specs=pl.BlockSpec((1, 1, tile, v_dim), lambda h, b, i: (b, h, i, 0)),
            out_shape=out_shape,
            scratch_shapes=[rhs, s_buf(), s_buf(), p_buf(), p_buf(), row(), row(), row(), row(), row(),
                            row(), acc],
            compiler_params=_cparams("parallel", "parallel", "parallel"),
            name="diff_attention",
        )(*operands)

    return lax.cond(bound_ok, bounded, exact)


def _outproj_kernel(oatt_ref, b_ref, u_ref, uprev_ref, unext_ref, x_ref, cw_ref, wout_ref, gffn_ref,
                    wr_ref, br_ref, x1_ref, tw_ref, route_ref, *, d_att, n_experts):
    i = pl.program_id(1)
    n_i = pl.num_programs(1)
    u = u_ref[0].astype(_F32)
    tm = u.shape[0]
    prev_row = jnp.where(i > 0, uprev_ref[0, _BF16_SUBLANES - 1:_BF16_SUBLANES, :].astype(_F32), 0.0)
    next_row = jnp.where(i < n_i - 1, unext_ref[0, 0:1, :].astype(_F32), 0.0)
    rows = lax.broadcasted_iota(jnp.int32, (tm, 1), 0)
    u_prev = jnp.where(rows == 0, prev_row, pltpu.roll(u, 1, axis=0))
    u_next = jnp.where(rows == tm - 1, next_row, pltpu.roll(u, tm - 1, axis=0))
    conv = cw_ref[0:1, :] * u_prev + cw_ref[1:2, :] * u + cw_ref[2:3, :] * u_next
    o_conv = (b_ref[0].astype(_F32) * conv).astype(_BF16)
    o_att = jnp.concatenate([oatt_ref[0, h] for h in range(oatt_ref.shape[1])], axis=1)
    mix = _dot(o_att, wout_ref[:d_att, :]) + _dot(o_conv, wout_ref[d_att:, :])
    x1 = x_ref[0] + mix
    x1_ref[0] = x1
    ms = jnp.mean(x1 * x1, axis=-1, keepdims=True)
    t = x1 * lax.rsqrt(ms + _EPS) * gffn_ref[...]
    t_hi = t.astype(_BF16)
    half = x1.shape[1] // 2
    tw_ref[0, :, :half] = _pack_bf16_pairs(t)
    t_lo = (t - t_hi.astype(_F32)).astype(_BF16)
    a_hi = _dot(t_hi, wr_ref[...])
    a_lo = _dot(t_lo, wr_ref[...])
    logits = a_hi[:, :_LANES] + a_hi[:, _LANES:] + a_lo[:, :_LANES] + br_ref[...]
    n_groups = n_experts // _EXPERTS_PER_GROUP
    lane = lax.broadcasted_iota(jnp.int32, (1, _LANES), 1)
    lane_f = lane.astype(_F32)
    big = float(_LANES)
    gmask = (lane >= n_experts) & (lane < n_experts + n_groups)
    gl = jnp.where(gmask, logits, _NEG_BIG)
    gmax = jnp.max(gl, axis=-1, keepdims=True)
    gsum = jnp.sum(jnp.where(gmask, jnp.exp(gl - gmax), 0.0), axis=-1, keepdims=True)
    g_w = 1.0 / gsum
    g_idx = jnp.min(jnp.where(gmask & (gl == gmax), lane_f - n_experts, big), axis=-1, keepdims=True)
    lo = g_idx * _EXPERTS_PER_GROUP
    emask = (lane_f >= lo) & (lane_f < lo + _EXPERTS_PER_GROUP)
    el = jnp.where(emask, logits, _NEG_BIG)
    emax = jnp.max(el, axis=-1, keepdims=True)
    ep = jnp.where(emask, jnp.exp(el - emax), 0.0)
    p_exp = ep / jnp.sum(ep, axis=-1, keepdims=True)
    top1 = jnp.max(p_exp, axis=-1, keepdims=True)
    i1 = jnp.min(jnp.where(emask & (p_exp == top1), lane_f, big), axis=-1, keepdims=True)
    rest = jnp.where(emask & (lane_f != i1), p_exp, -1.0)
    top2 = jnp.max(rest, axis=-1, keepdims=True)
    i2 = jnp.min(jnp.where(rest == top2, lane_f, big), axis=-1, keepdims=True)
    denom = top1 + top2
    gates = jnp.where(lane_f == i1 - lo, g_w * (top1 / denom),
                      jnp.where(lane_f == i2 - lo, g_w * (top2 / denom), 0.0))
    tw_ref[0, :, half:] = lax.bitcast_convert_type(gates, jnp.uint32)
    route_ref[0] = jnp.broadcast_to(g_idx, (tm, _LANES)).T[:_F32_SUBLANES, :]


def _outproj(oatt, b, u, x, conv_w, w_out, gffn, wr, br, *, n_experts, tm):
    B, S, D = x.shape
    n_heads, v_dim = oatt.shape[1], oatt.shape[3]
    d_att = n_heads * v_dim
    d_conv = b.shape[-1]
    hb = _BF16_SUBLANES
    per_tile = tm // hb
    n_halo = S // hb
    tok = lambda width: pl.BlockSpec((1, tm, width), lambda bb, i: (bb, i, 0))
    full = lambda shape: pl.BlockSpec(shape, lambda bb, i: (0,) * len(shape))
    return pl.pallas_call(
        functools.partial(_outproj_kernel, d_att=d_att, n_experts=n_experts),
        grid=(B, S // tm),
        in_specs=[pl.BlockSpec((1, n_heads, tm, v_dim), lambda bb, i: (bb, 0, i, 0)), tok(d_conv), tok(d_conv),
                  pl.BlockSpec((1, hb, d_conv), lambda bb, i: (bb, jnp.maximum(i * per_tile - 1, 0), 0)),
                  pl.BlockSpec((1, hb, d_conv),
                               lambda bb, i: (bb, jnp.minimum((i + 1) * per_tile, n_halo - 1), 0)),
                  tok(D), full(conv_w.shape), full(w_out.shape), full((1, D)), full(wr.shape),
                  full((1, _LANES))],
        out_specs=[tok(D), tok(D // 2 + _LANES),
                   pl.BlockSpec((1, _F32_SUBLANES, tm), lambda bb, i: (bb, 0, i))],
        out_shape=[jax.ShapeDtypeStruct((B, S, D), _F32),
                   jax.ShapeDtypeStruct((B, S, D // 2 + _LANES), jnp.uint32),
                   jax.ShapeDtypeStruct((B, _F32_SUBLANES, S), _F32)],
        compiler_params=_cparams("parallel", "parallel"),
        name="outproj_router",
    )(oatt, b, u, u, u, x, conv_w, w_out, gffn, wr, br)


def _row_copies_wait(src_rows, dst_rows, sem):
    pltpu.make_async_copy(src_rows, dst_rows, sem).wait()


def _stage_indices(idx_ref, idx_smem, slot, idx_sem):
    c = pltpu.make_async_copy(idx_ref.at[0, 0], idx_smem.at[slot], idx_sem)
    c.start()
    c.wait()


def _scatter_rows_kernel(pads_ref, pos_a_ref, pos_b_ref, x_ref, out_hbm, idx_smem, idx_sem, stage, sem, *,
                         n_groups):
    j = pl.program_id(0)
    tm = stage.shape[1]

    def burst(slot):
        stage[slot] = x_ref[slot * tm:(slot + 1) * tm, :]
        for r in range(tm):
            pltpu.make_async_copy(stage.at[slot, pl.ds(r, 1), :], out_hbm.at[pl.ds(idx_smem[slot, r], 1), :],
                                  sem.at[slot]).start(priority=r % _DMA_PRIORITIES)

    def burst_wait(slot):
        _row_copies_wait(stage.at[slot], out_hbm.at[pl.ds(0, tm), :], sem.at[slot])

    _stage_indices(pos_a_ref, idx_smem, 0, idx_sem)
    _stage_indices(pos_b_ref, idx_smem, 1, idx_sem)
    burst(0)

    @pl.when(j > 0)
    def _():
        burst_wait(1)

    burst(1)
    burst_wait(0)

    @pl.when(j == pl.num_programs(0) - 1)
    def _():
        burst_wait(1)
        for g in range(n_groups):
            lo = pads_ref[g]
            count = pads_ref[n_groups + g] - lo

            def fill(k, carry, lo=lo):
                pltpu.make_async_copy(stage.at[0, pl.ds(0, 1), :], out_hbm.at[pl.ds(lo + k, 1), :],
                                      sem.at[0]).start()
                return carry

            def drain(k, carry):
                _row_copies_wait(stage.at[0, pl.ds(0, 1), :], out_hbm.at[pl.ds(0, 1), :], sem.at[0])
                return carry

            lax.fori_loop(0, count, fill, 0)
            lax.fori_loop(0, count, drain, 0)


def _scatter_rows(pads, pos, x, n_out_rows, *, tm):
    T, width = x.shape
    n_tiles = T // tm
    assert n_tiles % 2 == 0
    idx = pos.reshape(n_tiles, 1, tm)
    return pl.pallas_call(
        functools.partial(_scatter_rows_kernel, n_groups=pads.shape[0] // 2),
        grid_spec=pltpu.PrefetchScalarGridSpec(
            num_scalar_prefetch=1,
            grid=(n_tiles // 2,),
            in_specs=[pl.BlockSpec((1, 1, tm), lambda j, pads: (2 * j, 0, 0)),
                      pl.BlockSpec((1, 1, tm), lambda j, pads: (2 * j + 1, 0, 0)),
                      pl.BlockSpec((2 * tm, width), lambda j, pads: (j, 0))],
            out_specs=pl.BlockSpec(memory_space=pl.ANY),
            scratch_shapes=[pltpu.SMEM((2, tm), jnp.int32), pltpu.SemaphoreType.DMA,
                            pltpu.VMEM((2, tm, width), x.dtype), pltpu.SemaphoreType.DMA((2,))]),
        out_shape=jax.ShapeDtypeStruct((n_out_rows, width), x.dtype),
        compiler_params=_cparams("arbitrary"),
        name="dispatch_rows",
    )(pads, idx, idx, x)


def _moe_sorted_kernel(tile_group_ref, tw_ref, wgu_ref, wdn_ref, out_ref, *, d_expert):
    del tile_group_ref
    half = out_ref.shape[1]
    t = _unpack_bf16_pairs(tw_ref[:, :half]).astype(_BF16)
    gates = lax.bitcast_convert_type(tw_ref[:, half:], _F32)
    parts = []
    for e in range(_EXPERTS_PER_GROUP):
        gu = _dot(t, wgu_ref[e])
        g_lin = gu[:, :d_expert]
        act = g_lin * jax.nn.sigmoid(g_lin) * gu[:, d_expert:]
        parts.append((act * gates[:, e:e + 1]).astype(_BF16))
    out_ref[...] = _pack_bf16_pairs(_dot(jnp.concatenate(parts, axis=1), wdn_ref[0]))


def _moe_sorted(tile_group, tw_sorted, wgu, wdn, *, d_expert, tm):
    rows, width = tw_sorted.shape
    half = width - _LANES
    grp = lambda shape: pl.BlockSpec((1,) + shape, lambda i, tg: (tg[i], 0, 0))
    return pl.pallas_call(
        functools.partial(_moe_sorted_kernel, d_expert=d_expert),
        grid_spec=pltpu.PrefetchScalarGridSpec(
            num_scalar_prefetch=1,
            grid=(rows // tm,),
            in_specs=[pl.BlockSpec((tm, width), lambda i, tg: (i, 0)),
                      pl.BlockSpec((_EXPERTS_PER_GROUP,) + wgu.shape[1:], lambda i, tg: (tg[i], 0, 0)),
                      grp(wdn.shape[1:])],
            out_specs=pl.BlockSpec((tm, half), lambda i, tg: (i, 0))),
        out_shape=jax.ShapeDtypeStruct((rows, half), jnp.uint32),
        compiler_params=_cparams("parallel"),
        name="moe_sorted",
    )(tile_group, tw_sorted, wgu, wdn)


def _ple_kernel(idx_first_ref, idx_b_ref, idx_a_ref, moe_hbm, x1_ref, p_ref, wproj_ref, wgate_ref,
                gple_ref, y_ref, idx_smem, idx_sem, buf, sem):
    j = pl.program_id(0)
    tm = buf.shape[1]

    def rows_start(idx_slot, slot, lo=0, hi=None):
        for r in range(lo, tm if hi is None else hi):
            pltpu.make_async_copy(moe_hbm.at[pl.ds(idx_smem[idx_slot, r], 1), :],
                                  buf.at[slot, pl.ds(r, 1), :], sem.at[slot]).start(
                                      priority=r % _DMA_PRIORITIES)

    def rows_wait(slot):
        _row_copies_wait(moe_hbm.at[pl.ds(0, tm), :], buf.at[slot], sem.at[slot])

    def compute_rows(base, slot, lo, hi):
        rows = slice(base + lo, base + hi)
        x2 = x1_ref[rows, :] + _unpack_bf16_pairs(buf[slot, lo:hi, :])
        e_raw = _dot(p_ref[rows, :].astype(_BF16), wproj_ref[...])
        ms = jnp.mean(e_raw * e_raw, axis=-1, keepdims=True)
        emb = e_raw * lax.rsqrt(ms + _EPS) * gple_ref[...]
        gate_p = jax.nn.sigmoid(_dot(x2.astype(_BF16), wgate_ref[...]))
        y_ref[rows, :] = x2 + gate_p * emb

    def compute_and_fetch(base, slot, idx_slot, other):
        for lo in range(0, tm, _PLE_CHUNK):
            rows_start(idx_slot, other, min(2 * lo, tm), min(2 * lo + 2 * _PLE_CHUNK, tm))
            compute_rows(base, slot, lo, lo + _PLE_CHUNK)

    @pl.when(j == 0)
    def _():
        _stage_indices(idx_first_ref, idx_smem, 0, idx_sem)
        rows_start(0, 0)

    _stage_indices(idx_b_ref, idx_smem, 0, idx_sem)
    _stage_indices(idx_a_ref, idx_smem, 1, idx_sem)
    rows_wait(0)
    compute_and_fetch(0, 0, 0, 1)
    rows_wait(1)
    compute_and_fetch(tm, 1, 1, 0)

    @pl.when(j == pl.num_programs(0) - 1)
    def _():
        rows_wait(0)


def _ple(pos, moe_sorted, x1, p, wproj, wgate, gple, *, tm):
    T, D = x1.shape
    half = moe_sorted.shape[1]
    n_tiles = T // tm
    assert n_tiles % 2 == 0
    idx = pos.reshape(n_tiles, 1, tm)
    tok = lambda width: pl.BlockSpec((2 * tm, width), lambda j: (j, 0))
    full = lambda shape: pl.BlockSpec(shape, lambda j: (0,) * len(shape))
    idx_spec = lambda tile_of_step: pl.BlockSpec((1, 1, tm), lambda j: (tile_of_step(j), 0, 0))
    return pl.pallas_call(
        _ple_kernel,
        grid=(n_tiles // 2,),
        in_specs=[idx_spec(lambda j: 0), idx_spec(lambda j: 2 * j + 1),
                  idx_spec(lambda j: jnp.minimum(2 * j + 2, n_tiles - 1)),
                  pl.BlockSpec(memory_space=pl.ANY), tok(D), tok(p.shape[-1]), full(wproj.shape),
                  full(wgate.shape), full((1, D))],
        out_specs=tok(D),
        out_shape=jax.ShapeDtypeStruct((T, D), _F32),
        scratch_shapes=[pltpu.SMEM((2, tm), jnp.int32), pltpu.SemaphoreType.DMA,
                        pltpu.VMEM((2, tm, half), jnp.uint32), pltpu.SemaphoreType.DMA((2,))],
        compiler_params=_cparams("arbitrary"),
        name="ple_gate",
    )(idx, idx, idx, moe_sorted, x1, p, wproj, wgate, gple)


def _dispatch_plan(group_of_token, n_groups, tm):
    T = group_of_token.shape[0]
    n_tiles = T // tm + n_groups - 1
    onehot = (group_of_token[:, None] == jnp.arange(n_groups, dtype=jnp.int32)[None, :]).astype(jnp.int32)
    rank = jnp.take_along_axis(jnp.cumsum(onehot, axis=0), group_of_token[:, None], axis=1)[:, 0] - 1
    count = jnp.sum(onehot, axis=0)
    tiles_per_group = (count + tm - 1) // tm
    tile_end = jnp.cumsum(tiles_per_group)
    tile_start = tile_end - tiles_per_group
    pos = tile_start[group_of_token] * tm + rank
    tile_group = jnp.searchsorted(tile_end, jnp.arange(n_tiles, dtype=jnp.int32), side="right")
    tile_group = jnp.minimum(tile_group, n_groups - 1).astype(jnp.int32)
    pad_lo = tile_start * tm + count
    pad_hi = (tile_end * tm).at[n_groups - 1].set(n_tiles * tm)
    pads = jnp.concatenate([pad_lo, pad_hi]).astype(jnp.int32)
    return pos.astype(jnp.int32), tile_group, pads, n_tiles * tm


def _prepare_layer(i, norm_mix, w_in, q_norm, k_norm, lambda_q1, lambda_k1, lambda_q2, lambda_k2,
                   attn_sub_norm, conv_w, w_out, norm_ffn, w_group, b_group, w_erouter, b_erouter,
                   w_gate_up, w_down, w_ple_proj, w_ple_gate, ple_norm):
    D = w_in.shape[1]
    d_mix = w_out.shape[1]
    d_att = d_mix // 2
    d_conv = d_mix - d_att
    n_maps = d_att // _ATT_HEAD_DIM
    n_experts = w_gate_up.shape[1]
    n_groups = w_group.shape[-1]
    d_expert = w_down.shape[2]
    assert n_experts == n_groups * _EXPERTS_PER_GROUP and n_experts + n_groups <= _LANES
    lambda_init = 0.8 - 0.6 * math.exp(-0.3 * i)
    head_of = jnp.arange(d_att) // _ATT_HEAD_DIM
    bd = jnp.where(head_of[:, None] == head_of[None, :], 1.0 / _ATT_HEAD_DIM, 0.0).astype(_BF16)
    wr = jnp.zeros((D, _LANES), _F32)
    wr = wr.at[:, :n_experts].set(w_erouter[i]).at[:, n_experts:n_experts + n_groups].set(w_group[i])
    wr_hi = wr.astype(_BF16)
    wr_lo = (wr - wr_hi.astype(_F32)).astype(_BF16)
    br = jnp.zeros((1, _LANES), _F32)
    br = br.at[0, :n_experts].set(b_erouter[i]).at[0, n_experts:n_experts + n_groups].set(b_group[i])
    wgu = w_gate_up[i]
    wdn = w_down[i].reshape(n_groups, _EXPERTS_PER_GROUP * d_expert, D)
    return dict(
        d_att=d_att, d_conv=d_conv, n_experts=n_experts, d_expert=d_expert, lambda_init=lambda_init,
        gmix=norm_mix[i][None, :], w_in=w_in[i].astype(_BF16),
        gq_col=q_norm[i][:, None], gk_row=jnp.tile(k_norm[i], n_maps)[None, :], bd=bd,
        lam_vecs=tuple(v[i][None, :] for v in (lambda_q1, lambda_k1, lambda_q2, lambda_k2)),
        gsub_col=attn_sub_norm[i][:, None], conv_w=conv_w[i], w_out=w_out[i].astype(_BF16),
        gffn=norm_ffn[i][None, :], wr=jnp.concatenate([wr_hi, wr_lo], axis=1), br=br,
        wgu=wgu.astype(_BF16), wdn=wdn.astype(_BF16), wproj=w_ple_proj[i].astype(_BF16),
        wgate=w_ple_gate[i].astype(_BF16), gple=ple_norm[i][None, :])


def _score_bound(q_gain, k_gain, rel_bias):
    hd = _ATT_HEAD_DIM
    rounding_slack = 1.02
    q_norm_max = math.sqrt(hd) * jnp.max(jnp.abs(q_gain)) * (_LOG2E / math.sqrt(hd))
    k_norm_max = math.sqrt(hd) * jnp.max(jnp.abs(k_gain))
    qk = rounding_slack * q_norm_max * k_norm_max
    b2 = rel_bias.astype(_F32) * _LOG2E
    b_max, b_min = jnp.max(b2, axis=0), jnp.min(b2, axis=0)
    ok = jnp.all(2.0 * qk + (b_max - b_min) <= _SAFE_EXPONENT_SPAN)
    ok = ok & jnp.isfinite(qk) & jnp.all(jnp.isfinite(b2))
    return ok, jnp.where(ok, qk + b_max, jnp.zeros_like(b_max))


def _layer(x, p_i, L, bias_tiles, bound_ok, lam):
    B, S, D = x.shape
    tm = _TOKEN_TILE
    tile = _ATTN_TILE
    assert S % tm == 0 and S % tile == 0 and tm % _BF16_SUBLANES == 0
    qT, k, vT, b, u = _inproj(x, L["gmix"], L["w_in"], L["gq_col"], L["gk_row"], L["bd"],
                              d_att=L["d_att"], d_conv=L["d_conv"], tm=tm)
    oatt = _attention(bound_ok, lam, qT, k, vT, bias_tiles, L["gsub_col"], tile=tile,
                      out_scale=1.0 - L["lambda_init"])
    x1, tw, route = _outproj(oatt, b, u, x, L["conv_w"], L["w_out"], L["gffn"], L["wr"], L["br"],
                             n_experts=L["n_experts"], tm=tm)
    x1 = x1.reshape(B * S, D)
    tw = tw.reshape(B * S, D // 2 + _LANES)
    group_of_token = route[:, 0, :].reshape(B * S).astype(jnp.int32)
    pos, tile_group, pads, sorted_rows = _dispatch_plan(group_of_token, L["wdn"].shape[0], tm)
    tw_sorted = _scatter_rows(pads, pos, tw, sorted_rows, tm=tm)
    moe = _moe_sorted(tile_group, tw_sorted, L["wgu"], L["wdn"], d_expert=L["d_expert"], tm=tm)
    y = _ple(pos, moe, x1, p_i.reshape(B * S, -1), L["wproj"], L["wgate"], L["gple"], tm=tm)
    return y.reshape(B, S, D)


def kernel(x_prompt, x_sample, p_prompt, p_sample, norm_mix, w_in, q_norm, k_norm, lambda_q1, lambda_k1, lambda_q2, lambda_k2, attn_sub_norm, conv_w, w_out, rel_bias, norm_ffn, w_group, b_group, w_erouter, b_erouter, w_gate_up, w_down, w_ple_proj, w_ple_gate, ple_norm):
    depth = w_in.shape[0]
    layers = []
    for i in range(depth):
        L = _prepare_layer(i, norm_mix, w_in, q_norm, k_norm, lambda_q1, lambda_k1, lambda_q2,
                           lambda_k2, attn_sub_norm, conv_w, w_out, norm_ffn, w_group, b_group,
                           w_erouter, b_erouter, w_gate_up, w_down, w_ple_proj, w_ple_gate, ple_norm)
        bound_ok, shift = _score_bound(q_norm[i], k_norm[i], rel_bias)
        layers.append((L, _bias_tiles(rel_bias, shift, _ATTN_TILE), bound_ok,
                       _lambda(*L["lam_vecs"], L["lambda_init"])))

    def encode(x, p):
        for i, (L, bias_tiles, bound_ok, lam) in enumerate(layers):
            x = _layer(x, p[i], L, bias_tiles, bound_ok, lam)
        return x

    return (encode(x_prompt, p_prompt), encode(x_sample, p_sample))
```

```python
import functools
import math

import jax
import jax.numpy as jnp
from jax import lax
from jax.experimental import pallas as pl
from jax.experimental.pallas import tpu as pltpu

_F32 = jnp.float32
_BF16 = jnp.bfloat16

_EPS = 1e-6
_ATT_HEAD_DIM = 64
_MAX_DISTANCE = 128
_EXPERTS_PER_GROUP = 4
_LANES = 128
_F32_SUBLANES = 8
_BF16_SUBLANES = 16
_ATTN_TILE = 512
_ATTN_ROWS = 128
_ATTN_LANES = 256
_TOKEN_TILE = 512
_PLE_CHUNK = 128
_DMA_PRIORITIES = 2
_VMEM_LIMIT_BYTES = 56 * 1024 * 1024
_SAFE_EXPONENT_SPAN = 100.0
_NEG_BIG = -1e30
_LOG2E = math.log2(math.e)


def _cparams(*sem):
    return pltpu.CompilerParams(dimension_semantics=sem, vmem_limit_bytes=_VMEM_LIMIT_BYTES)


def _dot(a, b):
    return jnp.dot(a, b, preferred_element_type=_F32)


def _pack_bf16_pairs(x):
    n = x.shape[1] // 2
    bits = lax.bitcast_convert_type(x.astype(_BF16).astype(_F32), jnp.uint32)
    return (bits[:, :n] & jnp.uint32(0xFFFF0000)) | (bits[:, n:] >> 16)


def _unpack_bf16_pairs(u):
    hi = lax.bitcast_convert_type(u & jnp.uint32(0xFFFF0000), _F32)
    lo = lax.bitcast_convert_type(u << 16, _F32)
    return jnp.concatenate([hi, lo], axis=1)


def _bias_tile_kernel(tbl_ref, shift_ref, out_ref, *, tile, num_buckets):
    di = pl.program_id(0)
    m = pl.program_id(1)
    half = num_buckets // 2
    max_exact = half // 2
    is_far = (di == 0) | (di == 4)

    @pl.when(is_far)
    def _():
        b_far = jnp.where(di == 0, half - 1, num_buckets - 1)
        out_ref[0, 0] = jnp.full((tile, tile), tbl_ref[b_far, m] * _LOG2E - shift_ref[m], _F32)

    @pl.when(jnp.logical_not(is_far))
    def _():
        delta = (di - 2) * tile
        kk = lax.broadcasted_iota(jnp.int32, (tile, tile), 0)
        qq = lax.broadcasted_iota(jnp.int32, (tile, tile), 1)
        rel = kk - qq + delta
        ret = jnp.where(rel > 0, half, 0)
        n = jnp.abs(rel)
        nf = jnp.maximum(n, 1).astype(_F32)
        large = max_exact + (jnp.log(nf / max_exact) / math.log(_MAX_DISTANCE / max_exact)
                             * (half - max_exact)).astype(jnp.int32)
        large = jnp.minimum(large, half - 1)
        bucket = ret + jnp.where(n < max_exact, n, large)
        acc = jnp.zeros((tile, tile), _F32)
        for b in range(num_buckets):
            acc = jnp.where(bucket == b, tbl_ref[b, m], acc)
        out_ref[0, 0] = acc * _LOG2E - shift_ref[m]


def _bias_tiles(rel_bias, shift, tile):
    num_buckets, n_maps = rel_bias.shape
    assert tile >= _MAX_DISTANCE
    return pl.pallas_call(
        functools.partial(_bias_tile_kernel, tile=tile, num_buckets=num_buckets),
        grid=(5, n_maps),
        in_specs=[pl.BlockSpec(memory_space=pltpu.SMEM), pl.BlockSpec(memory_space=pltpu.SMEM)],
        out_specs=pl.BlockSpec((1, 1, tile, tile), lambda di, m: (di, m // 2, 0, m % 2)),
        out_shape=jax.ShapeDtypeStruct((5, n_maps // 2, tile, 2 * tile), _F32),
        compiler_params=_cparams("arbitrary", "arbitrary"),
        name="bias_tiles",
    )(rel_bias.astype(_F32), shift.astype(_F32))


def _inproj_kernel(x_ref, gmix_ref, w_ref, gq_ref, gk_ref, bd_ref,
                   qT_ref, k_ref, vT_ref, b_ref, u_ref, *, d_att, d_conv, scale):
    x = x_ref[0]
    ms = jnp.mean(x * x, axis=-1, keepdims=True)
    h = (x * lax.rsqrt(ms + _EPS) * gmix_ref[...]).astype(_BF16)

    def proj(lo, n):
        return _dot(h, w_ref[:, lo:lo + n])

    tm = x.shape[0]
    n_maps = d_att // _ATT_HEAD_DIM
    zqT = proj(0, d_att).T.reshape(n_maps, _ATT_HEAD_DIM, tm)
    qms = jnp.mean(zqT * zqT, axis=1, keepdims=True)
    qn = zqT * lax.rsqrt(qms + _EPS) * gq_ref[...][None]
    qT_ref[0] = (qn * scale).reshape(d_att, tm).astype(_BF16)
    zk = proj(d_att, d_att)
    kms = _dot((zk * zk).astype(_BF16), bd_ref[...])
    k_ref[0] = (zk * lax.rsqrt(kms + _EPS) * gk_ref[...]).astype(_BF16)
    vT_ref[0] = proj(2 * d_att, d_att).T.astype(_BF16)
    b_ref[0] = proj(3 * d_att, d_conv).astype(_BF16)
    c = proj(3 * d_att + d_conv, d_conv)
    xc = proj(3 * d_att + 2 * d_conv, d_conv)
    u_ref[0] = (c * xc).astype(_BF16)


def _inproj(x, gmix, w_in, gq_col, gk_row, bd, *, d_att, d_conv, tm):
    B, S, D = x.shape
    d_in = w_in.shape[1]
    grid = (B, S // tm)
    tok = lambda width: pl.BlockSpec((1, tm, width), lambda b, i: (b, i, 0))
    tr = lambda rows: pl.BlockSpec((1, rows, tm), lambda b, i: (b, 0, i))
    full = lambda shape: pl.BlockSpec(shape, lambda b, i: (0,) * len(shape))
    return pl.pallas_call(
        functools.partial(_inproj_kernel, d_att=d_att, d_conv=d_conv,
                          scale=_LOG2E / math.sqrt(_ATT_HEAD_DIM)),
        grid=grid,
        in_specs=[tok(D), full((1, D)), full((D, d_in)), full((_ATT_HEAD_DIM, 1)), full((1, d_att)),
                  full((d_att, d_att))],
        out_specs=[tr(d_att), tok(d_att), tr(d_att), tok(d_conv), tok(d_conv)],
        out_shape=[jax.ShapeDtypeStruct((B, d_att, S), _BF16),
                   jax.ShapeDtypeStruct((B, S, d_att), _BF16),
                   jax.ShapeDtypeStruct((B, d_att, S), _BF16),
                   jax.ShapeDtypeStruct((B, S, d_conv), _BF16),
                   jax.ShapeDtypeStruct((B, S, d_conv), _BF16)],
        compiler_params=_cparams("parallel", "parallel"),
        name="inproj",
    )(x, gmix, w_in, gq_col, gk_row, bd)


def _attention_kernel(lam_ref, qT_ref, k_ref, vT_ref, bias_ref, gsub_ref, o_ref,
                      rhs_ref, s0_ref, s1_ref, p0_ref, p1_ref, mx0_ref, mx1_ref, al0_ref, al1_ref,
                      m_ref, l_ref, acc_ref, *, tile, out_scale):
    qi = pl.program_id(2)
    n_k = k_ref.shape[1] // tile
    hd = _ATT_HEAD_DIM
    s_refs, p_refs = (s0_ref, s1_ref), (p0_ref, p1_ref)
    mx_refs, al_refs = (mx0_ref, mx1_ref), (al0_ref, al1_ref)
    qT = qT_ref[0]
    row = lax.broadcasted_iota(jnp.int32, qT.shape, 0)
    zero = jnp.zeros_like(qT)
    rhs_ref[:, :tile] = jnp.where(row < hd, qT, zero)
    rhs_ref[:, tile:] = jnp.where(row >= hd, qT, zero)
    m_ref[...] = jnp.full(m_ref.shape, _NEG_BIG, _F32)
    l_ref[...] = jnp.zeros(l_ref.shape, _F32)
    acc_ref[...] = jnp.zeros(acc_ref.shape, _F32)

    row_blocks = [slice(r, r + _ATTN_ROWS) for r in range(0, tile, _ATTN_ROWS)]
    lane_blocks = [slice(c, c + _ATTN_LANES) for c in range(0, 2 * tile, _ATTN_LANES)]

    def fold8(x, op):
        return op(x.reshape(x.shape[0] // 8, 8, x.shape[1]), axis=0)

    def stage_a(kj, par):
        off = pl.multiple_of(kj * tile, tile)
        bidx = jnp.clip(kj - qi, -2, 2) + 2
        for cs in lane_blocks:
            rhs_c = rhs_ref[:, cs]
            mx = None
            for rs in row_blocks:
                k_blk = k_ref[0, pl.ds(off + rs.start, _ATTN_ROWS), :]
                s = _dot(k_blk, rhs_c) + bias_ref[bidx, 0, rs, cs]
                s_refs[par][rs, cs] = s
                part = fold8(s, jnp.max)
                mx = part if mx is None else jnp.maximum(mx, part)
            mx_refs[par][:, cs] = jnp.max(mx, axis=0, keepdims=True)

    def stage_b(par):
        m_old = m_ref[...]
        m_new = jnp.maximum(m_old, mx_refs[par][...])
        alpha = jnp.exp2(m_old - m_new)
        al_refs[par][...] = alpha
        m_ref[...] = m_new
        for cs in lane_blocks:
            m_c = m_new[:, cs]
            lsum = None
            for rs in row_blocks:
                p = jnp.exp2(s_refs[par][rs, cs] - m_c)
                p_refs[par][rs, cs] = p.astype(_BF16)
                part = fold8(p, jnp.sum)
                lsum = part if lsum is None else lsum + part
            l_ref[:, cs] = alpha[:, cs] * l_ref[:, cs] + jnp.sum(lsum, axis=0, keepdims=True)

    def stage_c(kj, par):
        off = pl.multiple_of(kj * tile, tile)
        for cs in lane_blocks:
            pv = _dot(vT_ref[0, :, pl.ds(off, tile)], p_refs[par][:, cs])
            acc_ref[:, cs] = al_refs[par][:, cs] * acc_ref[:, cs] + pv

    stage_a(0, 0)
    stage_a(1, 1)
    stage_b(0)

    def pair(t, carry):
        j = 2 * t + 1
        stage_a(j + 1, 0)
        stage_b(1)
        stage_c(j - 1, 0)
        stage_a(j + 2, 1)
        stage_b(0)
        stage_c(j, 1)
        return carry

    lax.fori_loop(0, (n_k - 2) // 2, pair, 0)
    stage_b(1)
    stage_c(n_k - 2, 0)
    stage_c(n_k - 1, 1)

    o_ref[0, 0] = _attention_output(lam_ref, l_ref[...], acc_ref[...], gsub_ref, tile=tile,
                                    out_scale=out_scale)


def _attention_output(lam_ref, l, acc, gsub_ref, *, tile, out_scale):
    o = acc * (1.0 / l)
    oT = o[:, :tile] - lam_ref[0] * o[:, tile:]
    ms = jnp.mean(oT * oT, axis=0, keepdims=True)
    oT = oT * lax.rsqrt(ms + _EPS) * gsub_ref[...] * out_scale
    return oT.T.astype(_BF16)


def _attention_bounded_kernel(lam_ref, qT_ref, k_ref, vT_ref, bias_ref, gsub_ref, o_ref,
                              rhs_ref, p0_ref, p1_ref, l_ref, acc_ref, *, tile, out_scale):
    seq = k_ref.shape[1]
    n_k = seq // tile
    hd = _ATT_HEAD_DIM
    p_refs = (p0_ref, p1_ref)
    row_blocks = [slice(r, r + _ATTN_ROWS) for r in range(0, tile, _ATTN_ROWS)]
    lane_blocks = [slice(c, c + _ATTN_LANES) for c in range(0, 2 * tile, _ATTN_LANES)]

    def setup_q(qi, q_load):
        qpar = qi % 2
        qT = qT_ref[0, :, pl.ds(pl.multiple_of(q_load * tile, tile), tile)]
        row = lax.broadcasted_iota(jnp.int32, qT.shape, 0)
        zero = jnp.zeros_like(qT)
        rhs_ref[qpar, :, :tile] = jnp.where(row < hd, qT, zero)
        rhs_ref[qpar, :, tile:] = jnp.where(row >= hd, qT, zero)
        l_ref[qpar] = jnp.zeros(l_ref.shape[1:], _F32)

    def stage_ab(qi, kj, par):
        qpar = qi % 2
        off = pl.multiple_of(kj * tile, tile)
        bidx = jnp.clip(kj - qi, -2, 2) + 2
        k_tile = k_ref[0, pl.ds(off, tile), :]
        for cs in lane_blocks:
            s = _dot(k_tile, rhs_ref[qpar, :, cs])
            lsum = None
            for rs in row_blocks:
                p = jnp.exp2(s[rs, :] + bias_ref[bidx, 0, rs, cs])
                p_refs[par][rs, cs] = p.astype(_BF16)
                part = jnp.sum(p.reshape(_ATTN_ROWS // 8, 8, _ATTN_LANES), axis=0)
                lsum = part if lsum is None else lsum + part
            l_ref[qpar, :, cs] += jnp.sum(lsum, axis=0, keepdims=True)

    def stage_c(kj, par):
        off = pl.multiple_of(kj * tile, tile)
        for cs in lane_blocks:
            acc_ref[:, cs] += _dot(vT_ref[0, :, pl.ds(off, tile)], p_refs[par][:, cs])

    n_q = seq // tile
    acc_ref[...] = jnp.zeros(acc_ref.shape, _F32)
    setup_q(0, 0)
    stage_ab(0, 0, 0)

    def q_tile_body(qi, carry):
        def pair(t, c):
            j = 2 * t
            stage_ab(qi, j + 1, 1)
            stage_c(j, 0)
            stage_ab(qi, j + 2, 0)
            stage_c(j + 1, 1)
            return c

        lax.fori_loop(0, n_k // 2 - 1, pair, 0)
        stage_ab(qi, n_k - 1, 1)
        stage_c(n_k - 2, 0)
        q_next = qi + 1
        setup_q(q_next, jnp.minimum(q_next, n_q - 1))
        stage_ab(q_next, 0, 0)
        stage_c(n_k - 1, 1)
        o_ref[0, 0, pl.ds(pl.multiple_of(qi * tile, tile), tile), :] = _attention_output(
            lam_ref, l_ref[qi % 2], acc_ref[...], gsub_ref, tile=tile, out_scale=out_scale)
        acc_ref[...] = jnp.zeros(acc_ref.shape, _F32)
        return carry

    lax.fori_loop(0, n_q, q_tile_body, 0)


def _lambda_kernel(q1_ref, k1_ref, q2_ref, k2_ref, out_ref, *, lambda_init):
    a = jnp.sum(q1_ref[...] * k1_ref[...], axis=-1, keepdims=True)
    b = jnp.sum(q2_ref[...] * k2_ref[...], axis=-1, keepdims=True)
    out_ref[...] = jnp.broadcast_to(jnp.exp(a) - jnp.exp(b) + lambda_init, out_ref.shape)


def _lambda(q1, k1, q2, k2, lambda_init):
    out = pl.pallas_call(
        functools.partial(_lambda_kernel, lambda_init=lambda_init),
        out_shape=jax.ShapeDtypeStruct((1, _LANES), _F32),
        name="lambda_scalar",
    )(q1, k1, q2, k2)
    return out[0, :1]


def _attention(bound_ok, lam, qT, k, vT, bias_tiles, gsub_col, *, tile, out_scale):
    B, d_att, S = qT.shape
    v_dim = 2 * _ATT_HEAD_DIM
    n_heads = d_att // v_dim
    assert S % (2 * tile) == 0
    row = lambda: pltpu.VMEM((1, 2 * tile), _F32)
    s_buf = lambda: pltpu.VMEM((tile, 2 * tile), _F32)
    p_buf = lambda: pltpu.VMEM((tile, 2 * tile), _BF16)
    rhs = pltpu.VMEM((v_dim, 2 * tile), _BF16)
    acc = pltpu.VMEM((v_dim, 2 * tile), _F32)

    out_shape = jax.ShapeDtypeStruct((B, n_heads, S, v_dim), _BF16)
    operands = (lam, qT, k, vT, bias_tiles, gsub_col)

    def bounded():
        return pl.pallas_call(
            functools.partial(_attention_bounded_kernel, tile=tile, out_scale=out_scale),
            grid=(n_heads, B),
            in_specs=[pl.BlockSpec(memory_space=pltpu.SMEM),
                      pl.BlockSpec((1, v_dim, S), lambda h, b: (b, h, 0)),
                      pl.BlockSpec((1, S, v_dim), lambda h, b: (b, 0, h)),
                      pl.BlockSpec((1, v_dim, S), lambda h, b: (b, h, 0)),
                      pl.BlockSpec((5, 1, tile, 2 * tile), lambda h, b: (0, h, 0, 0)),
                      pl.BlockSpec((v_dim, 1), lambda h, b: (0, 0))],
            out_specs=pl.BlockSpec((1, 1, S, v_dim), lambda h, b: (b, h, 0, 0)),
            out_shape=out_shape,
            scratch_shapes=[pltpu.VMEM((2, v_dim, 2 * tile), _BF16), p_buf(), p_buf(),
                            pltpu.VMEM((2, 1, 2 * tile), _F32), acc],
            compiler_params=_cparams("parallel", "parallel"),
            name="diff_attention_bounded",
        )(*operands)

    def exact():
        return pl.pallas_call(
            functools.partial(_attention_kernel, tile=tile, out_scale=out_scale),
            grid=(n_heads, B, S // tile),
            in_specs=[pl.BlockSpec(memory_space=pltpu.SMEM),
                      pl.BlockSpec((1, v_dim, tile), lambda h, b, i: (b, h, i)),
                      pl.BlockSpec((1, S, v_dim), lambda h, b, i: (b, 0, h)),
                      pl.BlockSpec((1, v_dim, S), lambda h, b, i: (b, h, 0)),
                      pl.BlockSpec((5, 1, tile, 2 * tile), lambda h, b, i: (0, h, 0, 0)),
                      pl.BlockSpec((v_dim, 1), lambda h, b, i: (0, 0))],
            out_specs=pl.BlockSpec((1, 1, tile, v_dim), lambda h, b, i: (b, h, i, 0)),
            out_shape=out_shape,
            scratch_shapes=[rhs, s_buf(), s_buf(), p_buf(), p_buf(), row(), row(), row(), row(), row(),
                            row(), acc],
            compiler_params=_cparams("parallel", "parallel", "parallel"),
            name="diff_attention",
        )(*operands)

    return lax.cond(bound_ok, bounded, exact)


def _outproj_kernel(oatt_ref, b_ref, u_ref, uprev_ref, unext_ref, x_ref, cw_ref, wout_ref, gffn_ref,
                    wr_ref, br_ref, x1_ref, tw_ref, route_ref, *, d_att, n_experts):
    i = pl.program_id(1)
    n_i = pl.num_programs(1)
    u = u_ref[0].astype(_F32)
    tm = u.shape[0]
    prev_row = jnp.where(i > 0, uprev_ref[0, _BF16_SUBLANES - 1:_BF16_SUBLANES, :].astype(_F32), 0.0)
    next_row = jnp.where(i < n_i - 1, unext_ref[0, 0:1, :].astype(_F32), 0.0)
    rows = lax.broadcasted_iota(jnp.int32, (tm, 1), 0)
    u_prev = jnp.where(rows == 0, prev_row, pltpu.roll(u, 1, axis=0))
    u_next = jnp.where(rows == tm - 1, next_row, pltpu.roll(u, tm - 1, axis=0))
    conv = cw_ref[0:1, :] * u_prev + cw_ref[1:2, :] * u + cw_ref[2:3, :] * u_next
    o_conv = (b_ref[0].astype(_F32) * conv).astype(_BF16)
    o_att = jnp.concatenate([oatt_ref[0, h] for h in range(oatt_ref.shape[1])], axis=1)
    mix = _dot(o_att, wout_ref[:d_att, :]) + _dot(o_conv, wout_ref[d_att:, :])
    x1 = x_ref[0] + mix
    x1_ref[0] = x1
    ms = jnp.mean(x1 * x1, axis=-1, keepdims=True)
    t = x1 * lax.rsqrt(ms + _EPS) * gffn_ref[...]
    t_hi = t.astype(_BF16)
    half = x1.shape[1] // 2
    tw_ref[0, :, :half] = _pack_bf16_pairs(t)
    t_lo = (t - t_hi.astype(_F32)).astype(_BF16)
    a_hi = _dot(t_hi, wr_ref[...])
    a_lo = _dot(t_lo, wr_ref[...])
    logits = a_hi[:, :_LANES] + a_hi[:, _LANES:] + a_lo[:, :_LANES] + br_ref[...]
    n_groups = n_experts // _EXPERTS_PER_GROUP
    lane = lax.broadcasted_iota(jnp.int32, (1, _LANES), 1)
    lane_f = lane.astype(_F32)
    big = float(_LANES)
    gmask = (lane >= n_experts) & (lane < n_experts + n_groups)
    gl = jnp.where(gmask, logits, _NEG_BIG)
    gmax = jnp.max(gl, axis=-1, keepdims=True)
    gsum = jnp.sum(jnp.where(gmask, jnp.exp(gl - gmax), 0.0), axis=-1, keepdims=True)
    g_w = 1.0 / gsum
    g_idx = jnp.min(jnp.where(gmask & (gl == gmax), lane_f - n_experts, big), axis=-1, keepdims=True)
    lo = g_idx * _EXPERTS_PER_GROUP
    emask = (lane_f >= lo) & (lane_f < lo + _EXPERTS_PER_GROUP)
    el = jnp.where(emask, logits, _NEG_BIG)
    emax = jnp.max(el, axis=-1, keepdims=True)
    ep = jnp.where(emask, jnp.exp(el - emax), 0.0)
    p_exp = ep / jnp.sum(ep, axis=-1, keepdims=True)
    top1 = jnp.max(p_exp, axis=-1, keepdims=True)
    i1 = jnp.min(jnp.where(emask & (p_exp == top1), lane_f, big), axis=-1, keepdims=True)
    rest = jnp.where(emask & (lane_f != i1), p_exp, -1.0)
    top2 = jnp.max(rest, axis=-1, keepdims=True)
    i2 = jnp.min(jnp.where(rest == top2, lane_f, big), axis=-1, keepdims=True)
    denom = top1 + top2
    gates = jnp.where(lane_f == i1 - lo, g_w * (top1 / denom),
                      jnp.where(lane_f == i2 - lo, g_w * (top2 / denom), 0.0))
    tw_ref[0, :, half:] = lax.bitcast_convert_type(gates, jnp.uint32)
    route_ref[0] = jnp.broadcast_to(g_idx, (tm, _LANES)).T[:_F32_SUBLANES, :]


def _outproj(oatt, b, u, x, conv_w, w_out, gffn, wr, br, *, n_experts, tm):
    B, S, D = x.shape
    n_heads, v_dim = oatt.shape[1], oatt.shape[3]
    d_att = n_heads * v_dim
    d_conv = b.shape[-1]
    hb = _BF16_SUBLANES
    per_tile = tm // hb
    n_halo = S // hb
    tok = lambda width: pl.BlockSpec((1, tm, width), lambda bb, i: (bb, i, 0))
    full = lambda shape: pl.BlockSpec(shape, lambda bb, i: (0,) * len(shape))
    return pl.pallas_call(
        functools.partial(_outproj_kernel, d_att=d_att, n_experts=n_experts),
        grid=(B, S // tm),
        in_specs=[pl.BlockSpec((1, n_heads, tm, v_dim), lambda bb, i: (bb, 0, i, 0)), tok(d_conv), tok(d_conv),
                  pl.BlockSpec((1, hb, d_conv), lambda bb, i: (bb, jnp.maximum(i * per_tile - 1, 0), 0)),
                  pl.BlockSpec((1, hb, d_conv),
                               lambda bb, i: (bb, jnp.minimum((i + 1) * per_tile, n_halo - 1), 0)),
                  tok(D), full(conv_w.shape), full(w_out.shape), full((1, D)), full(wr.shape),
                  full((1, _LANES))],
        out_specs=[tok(D), tok(D // 2 + _LANES),
                   pl.BlockSpec((1, _F32_SUBLANES, tm), lambda bb, i: (bb, 0, i))],
        out_shape=[jax.ShapeDtypeStruct((B, S, D), _F32),
                   jax.ShapeDtypeStruct((B, S, D // 2 + _LANES), jnp.uint32),
                   jax.ShapeDtypeStruct((B, _F32_SUBLANES, S), _F32)],
        compiler_params=_cparams("parallel", "parallel"),
        name="outproj_router",
    )(oatt, b, u, u, u, x, conv_w, w_out, gffn, wr, br)


def _row_copies_wait(src_rows, dst_rows, sem):
    pltpu.make_async_copy(src_rows, dst_rows, sem).wait()


def _stage_indices(idx_ref, idx_smem, slot, idx_sem):
    c = pltpu.make_async_copy(idx_ref.at[0, 0], idx_smem.at[slot], idx_sem)
    c.start()
    c.wait()


def _scatter_rows_kernel(pads_ref, pos_a_ref, pos_b_ref, x_ref, out_hbm, idx_smem, idx_sem, stage, sem, *,
                         n_groups):
    j = pl.program_id(0)
    tm = stage.shape[1]

    def burst(slot):
        stage[slot] = x_ref[slot * tm:(slot + 1) * tm, :]
        for r in range(tm):
            pltpu.make_async_copy(stage.at[slot, pl.ds(r, 1), :], out_hbm.at[pl.ds(idx_smem[slot, r], 1), :],
                                  sem.at[slot]).start(priority=r % _DMA_PRIORITIES)

    def burst_wait(slot):
        _row_copies_wait(stage.at[slot], out_hbm.at[pl.ds(0, tm), :], sem.at[slot])

    _stage_indices(pos_a_ref, idx_smem, 0, idx_sem)
    _stage_indices(pos_b_ref, idx_smem, 1, idx_sem)
    burst(0)

    @pl.when(j > 0)
    def _():
        burst_wait(1)

    burst(1)
    burst_wait(0)

    @pl.when(j == pl.num_programs(0) - 1)
    def _():
        burst_wait(1)
        for g in range(n_groups):
            lo = pads_ref[g]
            count = pads_ref[n_groups + g] - lo

            def fill(k, carry, lo=lo):
                pltpu.make_async_copy(stage.at[0, pl.ds(0, 1), :], out_hbm.at[pl.ds(lo + k, 1), :],
                                      sem.at[0]).start()
                return carry

            def drain(k, carry):
                _row_copies_wait(stage.at[0, pl.ds(0, 1), :], out_hbm.at[pl.ds(0, 1), :], sem.at[0])
                return carry

            lax.fori_loop(0, count, fill, 0)
            lax.fori_loop(0, count, drain, 0)


def _scatter_rows(pads, pos, x, n_out_rows, *, tm):
    T, width = x.shape
    n_tiles = T // tm
    assert n_tiles % 2 == 0
    idx = pos.reshape(n_tiles, 1, tm)
    return pl.pallas_call(
        functools.partial(_scatter_rows_kernel, n_groups=pads.shape[0] // 2),
        grid_spec=pltpu.PrefetchScalarGridSpec(
            num_scalar_prefetch=1,
            grid=(n_tiles // 2,),
            in_specs=[pl.BlockSpec((1, 1, tm), lambda j, pads: (2 * j, 0, 0)),
                      pl.BlockSpec((1, 1, tm), lambda j, pads: (2 * j + 1, 0, 0)),
                      pl.BlockSpec((2 * tm, width), lambda j, pads: (j, 0))],
            out_specs=pl.BlockSpec(memory_space=pl.ANY),
            scratch_shapes=[pltpu.SMEM((2, tm), jnp.int32), pltpu.SemaphoreType.DMA,
                            pltpu.VMEM((2, tm, width), x.dtype), pltpu.SemaphoreType.DMA((2,))]),
        out_shape=jax.ShapeDtypeStruct((n_out_rows, width), x.dtype),
        compiler_params=_cparams("arbitrary"),
        name="dispatch_rows",
    )(pads, idx, idx, x)


def _moe_sorted_kernel(tile_group_ref, tw_ref, wgu_ref, wdn_ref, out_ref, *, d_expert):
    del tile_group_ref
    half = out_ref.shape[1]
    t = _unpack_bf16_pairs(tw_ref[:, :half]).astype(_BF16)
    gates = lax.bitcast_convert_type(tw_ref[:, half:], _F32)
    parts = []
    for e in range(_EXPERTS_PER_GROUP):
        gu = _dot(t, wgu_ref[e])
        g_lin = gu[:, :d_expert]
        act = g_lin * jax.nn.sigmoid(g_lin) * gu[:, d_expert:]
        parts.append((act * gates[:, e:e + 1]).astype(_BF16))
    out_ref[...] = _pack_bf16_pairs(_dot(jnp.concatenate(parts, axis=1), wdn_ref[0]))


def _moe_sorted(tile_group, tw_sorted, wgu, wdn, *, d_expert, tm):
    rows, width = tw_sorted.shape
    half = width - _LANES
    grp = lambda shape: pl.BlockSpec((1,) + shape, lambda i, tg: (tg[i], 0, 0))
    return pl.pallas_call(
        functools.partial(_moe_sorted_kernel, d_expert=d_expert),
        grid_spec=pltpu.PrefetchScalarGridSpec(
            num_scalar_prefetch=1,
            grid=(rows // tm,),
            in_specs=[pl.BlockSpec((tm, width), lambda i, tg: (i, 0)),
                      pl.BlockSpec((_EXPERTS_PER_GROUP,) + wgu.shape[1:], lambda i, tg: (tg[i], 0, 0)),
                      grp(wdn.shape[1:])],
            out_specs=pl.BlockSpec((tm, half), lambda i, tg: (i, 0))),
        out_shape=jax.ShapeDtypeStruct((rows, half), jnp.uint32),
        compiler_params=_cparams("parallel"),
        name="moe_sorted",
    )(tile_group, tw_sorted, wgu, wdn)


def _ple_kernel(idx_first_ref, idx_b_ref, idx_a_ref, moe_hbm, x1_ref, p_ref, wproj_ref, wgate_ref,
                gple_ref, y_ref, idx_smem, idx_sem, buf, sem):
    j = pl.program_id(0)
    tm = buf.shape[1]

    def rows_start(idx_slot, slot, lo=0, hi=None):
        for r in range(lo, tm if hi is None else hi):
            pltpu.make_async_copy(moe_hbm.at[pl.ds(idx_smem[idx_slot, r], 1), :],
                                  buf.at[slot, pl.ds(r, 1), :], sem.at[slot]).start(
                                      priority=r % _DMA_PRIORITIES)

    def rows_wait(slot):
        _row_copies_wait(moe_hbm.at[pl.ds(0, tm), :], buf.at[slot], sem.at[slot])

    def compute_rows(base, slot, lo, hi):
        rows = slice(base + lo, base + hi)
        x2 = x1_ref[rows, :] + _unpack_bf16_pairs(buf[slot, lo:hi, :])
        e_raw = _dot(p_ref[rows, :].astype(_BF16), wproj_ref[...])
        ms = jnp.mean(e_raw * e_raw, axis=-1, keepdims=True)
        emb = e_raw * lax.rsqrt(ms + _EPS) * gple_ref[...]
        gate_p = jax.nn.sigmoid(_dot(x2.astype(_BF16), wgate_ref[...]))
        y_ref[rows, :] = x2 + gate_p * emb

    def compute_and_fetch(base, slot, idx_slot, other):
        for lo in range(0, tm, _PLE_CHUNK):
            rows_start(idx_slot, other, lo, lo + _PLE_CHUNK)
            compute_rows(base, slot, lo, lo + _PLE_CHUNK)

    @pl.when(j == 0)
    def _():
        _stage_indices(idx_first_ref, idx_smem, 0, idx_sem)
        rows_start(0, 0)

    _stage_indices(idx_b_ref, idx_smem, 0, idx_sem)
    _stage_indices(idx_a_ref, idx_smem, 1, idx_sem)
    rows_wait(0)
    compute_and_fetch(0, 0, 0, 1)
    rows_wait(1)
    compute_and_fetch(tm, 1, 1, 0)

    @pl.when(j == pl.num_programs(0) - 1)
    def _():
        rows_wait(0)


def _ple(pos, moe_sorted, x1, p, wproj, wgate, gple, *, tm):
    T, D = x1.shape
    half = moe_sorted.shape[1]
    n_tiles = T // tm
    assert n_tiles % 2 == 0
    idx = pos.reshape(n_tiles, 1, tm)
    tok = lambda width: pl.BlockSpec((2 * tm, width), lambda j: (j, 0))
    full = lambda shape: pl.BlockSpec(shape, lambda j: (0,) * len(shape))
    idx_spec = lambda tile_of_step: pl.BlockSpec((1, 1, tm), lambda j: (tile_of_step(j), 0, 0))
    return pl.pallas_call(
        _ple_kernel,
        grid=(n_tiles // 2,),
        in_specs=[idx_spec(lambda j: 0), idx_spec(lambda j: 2 * j + 1),
                  idx_spec(lambda j: jnp.minimum(2 * j + 2, n_tiles - 1)),
                  pl.BlockSpec(memory_space=pl.ANY), tok(D), tok(p.shape[-1]), full(wproj.shape),
                  full(wgate.shape), full((1, D))],
        out_specs=tok(D),
        out_shape=jax.ShapeDtypeStruct((T, D), _F32),
        scratch_shapes=[pltpu.SMEM((2, tm), jnp.int32), pltpu.SemaphoreType.DMA,
                        pltpu.VMEM((2, tm, half), jnp.uint32), pltpu.SemaphoreType.DMA((2,))],
        compiler_params=_cparams("arbitrary"),
        name="ple_gate",
    )(idx, idx, idx, moe_sorted, x1, p, wproj, wgate, gple)


def _dispatch_plan(group_of_token, n_groups, tm):
    T = group_of_token.shape[0]
    n_tiles = T // tm + n_groups - 1
    onehot = (group_of_token[:, None] == jnp.arange(n_groups, dtype=jnp.int32)[None, :]).astype(jnp.int32)
    rank = jnp.take_along_axis(jnp.cumsum(onehot, axis=0), group_of_token[:, None], axis=1)[:, 0] - 1
    count = jnp.sum(onehot, axis=0)
    tiles_per_group = (count + tm - 1) // tm
    tile_end = jnp.cumsum(tiles_per_group)
    tile_start = tile_end - tiles_per_group
    pos = tile_start[group_of_token] * tm + rank
    tile_group = jnp.searchsorted(tile_end, jnp.arange(n_tiles, dtype=jnp.int32), side="right")
    tile_group = jnp.minimum(tile_group, n_groups - 1).astype(jnp.int32)
    pad_lo = tile_start * tm + count
    pad_hi = (tile_end * tm).at[n_groups - 1].set(n_tiles * tm)
    pads = jnp.concatenate([pad_lo, pad_hi]).astype(jnp.int32)
    return pos.astype(jnp.int32), tile_group, pads, n_tiles * tm


def _prepare_layer(i, norm_mix, w_in, q_norm, k_norm, lambda_q1, lambda_k1, lambda_q2, lambda_k2,
                   attn_sub_norm, conv_w, w_out, norm_ffn, w_group, b_group, w_erouter, b_erouter,
                   w_gate_up, w_down, w_ple_proj, w_ple_gate, ple_norm):
    D = w_in.shape[1]
    d_mix = w_out.shape[1]
    d_att = d_mix // 2
    d_conv = d_mix - d_att
    n_maps = d_att // _ATT_HEAD_DIM
    n_experts = w_gate_up.shape[1]
    n_groups = w_group.shape[-1]
    d_expert = w_down.shape[2]
    assert n_experts == n_groups * _EXPERTS_PER_GROUP and n_experts + n_groups <= _LANES
    lambda_init = 0.8 - 0.6 * math.exp(-0.3 * i)
    head_of = jnp.arange(d_att) // _ATT_HEAD_DIM
    bd = jnp.where(head_of[:, None] == head_of[None, :], 1.0 / _ATT_HEAD_DIM, 0.0).astype(_BF16)
    wr = jnp.zeros((D, _LANES), _F32)
    wr = wr.at[:, :n_experts].set(w_erouter[i]).at[:, n_experts:n_experts + n_groups].set(w_group[i])
    wr_hi = wr.astype(_BF16)
    wr_lo = (wr - wr_hi.astype(_F32)).astype(_BF16)
    br = jnp.zeros((1, _LANES), _F32)
    br = br.at[0, :n_experts].set(b_erouter[i]).at[0, n_experts:n_experts + n_groups].set(b_group[i])
    wgu = w_gate_up[i]
    wdn = w_down[i].reshape(n_groups, _EXPERTS_PER_GROUP * d_expert, D)
    return dict(
        d_att=d_att, d_conv=d_conv, n_experts=n_experts, d_expert=d_expert, lambda_init=lambda_init,
        gmix=norm_mix[i][None, :], w_in=w_in[i].astype(_BF16),
        gq_col=q_norm[i][:, None], gk_row=jnp.tile(k_norm[i], n_maps)[None, :], bd=bd,
        lam_vecs=tuple(v[i][None, :] for v in (lambda_q1, lambda_k1, lambda_q2, lambda_k2)),
        gsub_col=attn_sub_norm[i][:, None], conv_w=conv_w[i], w_out=w_out[i].astype(_BF16),
        gffn=norm_ffn[i][None, :], wr=jnp.concatenate([wr_hi, wr_lo], axis=1), br=br,
        wgu=wgu.astype(_BF16), wdn=wdn.astype(_BF16), wproj=w_ple_proj[i].astype(_BF16),
        wgate=w_ple_gate[i].astype(_BF16), gple=ple_norm[i][None, :])


def _score_bound(q_gain, k_gain, rel_bias):
    hd = _ATT_HEAD_DIM
    rounding_slack = 1.02
    q_norm_max = math.sqrt(hd) * jnp.max(jnp.abs(q_gain)) * (_LOG2E / math.sqrt(hd))
    k_norm_max = math.sqrt(hd) * jnp.max(jnp.abs(k_gain))
    qk = rounding_slack * q_norm_max * k_norm_max
    b2 = rel_bias.astype(_F32) * _LOG2E
    b_max, b_min = jnp.max(b2, axis=0), jnp.min(b2, axis=0)
    ok = jnp.all(2.0 * qk + (b_max - b_min) <= _SAFE_EXPONENT_SPAN)
    ok = ok & jnp.isfinite(qk) & jnp.all(jnp.isfinite(b2))
    return ok, jnp.where(ok, qk + b_max, jnp.zeros_like(b_max))


def _layer(x, p_i, L, bias_tiles, bound_ok, lam):
    B, S, D = x.shape
    tm = _TOKEN_TILE
    tile = _ATTN_TILE
    assert S % tm == 0 and S % tile == 0 and tm % _BF16_SUBLANES == 0
    qT, k, vT, b, u = _inproj(x, L["gmix"], L["w_in"], L["gq_col"], L["gk_row"], L["bd"],
                              d_att=L["d_att"], d_conv=L["d_conv"], tm=tm)
    oatt = _attention(bound_ok, lam, qT, k, vT, bias_tiles, L["gsub_col"], tile=tile,
                      out_scale=1.0 - L["lambda_init"])
    x1, tw, route = _outproj(oatt, b, u, x, L["conv_w"], L["w_out"], L["gffn"], L["wr"], L["br"],
                             n_experts=L["n_experts"], tm=tm)
    x1 = x1.reshape(B * S, D)
    tw = tw.reshape(B * S, D // 2 + _LANES)
    group_of_token = route[:, 0, :].reshape(B * S).astype(jnp.int32)
    pos, tile_group, pads, sorted_rows = _dispatch_plan(group_of_token, L["wdn"].shape[0], tm)
    tw_sorted = _scatter_rows(pads, pos, tw, sorted_rows, tm=tm)
    moe = _moe_sorted(tile_group, tw_sorted, L["wgu"], L["wdn"], d_expert=L["d_expert"], tm=tm)
    y = _ple(pos, moe, x1, p_i.reshape(B * S, -1), L["wproj"], L["wgate"], L["gple"], tm=tm)
    return y.reshape(B, S, D)


def kernel(x_prompt, x_sample, p_prompt, p_sample, norm_mix, w_in, q_norm, k_norm, lambda_q1, lambda_k1, lambda_q2, lambda_k2, attn_sub_norm, conv_w, w_out, rel_bias, norm_ffn, w_group, b_group, w_erouter, b_erouter, w_gate_up, w_down, w_ple_proj, w_ple_gate, ple_norm):
    depth = w_in.shape[0]
    layers = []
    for i in range(depth):
        L = _prepare_layer(i, norm_mix, w_in, q_norm, k_norm, lambda_q1, lambda_k1, lambda_q2,
                           lambda_k2, attn_sub_norm, conv_w, w_out, norm_ffn, w_group, b_group,
                           w_erouter, b_erouter, w_gate_up, w_down, w_ple_proj, w_ple_gate, ple_norm)
        bound_ok, shift = _score_bound(q_norm[i], k_norm[i], rel_bias)
        layers.append((L, _bias_tiles(rel_bias, shift, _ATTN_TILE), bound_ok,
                       _lambda(*L["lam_vecs"], L["lambda_init"])))

    def encode(x, p):
        for i, (L, bias_tiles, bound_ok, lam) in enumerate(layers):
            x = _layer(x, p[i], L, bias_tiles, bound_ok, lam)
        return x

    return (encode(x_prompt, p_prompt), encode(x_sample, p_sample))
```

```python
import functools
import math

import jax
import jax.numpy as jnp
from jax import lax
from jax.experimental import pallas as pl
from jax.experimental.pallas import tpu as pltpu

_F32 = jnp.float32
_BF16 = jnp.bfloat16

_EPS = 1e-6
_ATT_HEAD_DIM = 64
_MAX_DISTANCE = 128
_EXPERTS_PER_GROUP = 4
_LANES = 128
_F32_SUBLANES = 8
_BF16_SUBLANES = 16
_ATTN_TILE = 512
_ATTN_ROWS = 128
_ATTN_LANES = 256
_TOKEN_TILE = 512
_PLE_CHUNK = 128
_DMA_PRIORITIES = 2
_VMEM_LIMIT_BYTES = 56 * 1024 * 1024
_SAFE_EXPONENT_SPAN = 100.0
_NEG_BIG = -1e30
_LOG2E = math.log2(math.e)


def _cparams(*sem):
    return pltpu.CompilerParams(dimension_semantics=sem, vmem_limit_bytes=_VMEM_LIMIT_BYTES)


def _dot(a, b):
    return jnp.dot(a, b, preferred_element_type=_F32)


def _pack_bf16_pairs(x):
    n = x.shape[1] // 2
    bits = lax.bitcast_convert_type(x.astype(_BF16).astype(_F32), jnp.uint32)
    return (bits[:, :n] & jnp.uint32(0xFFFF0000)) | (bits[:, n:] >> 16)


def _unpack_bf16_pairs(u):
    hi = lax.bitcast_convert_type(u & jnp.uint32(0xFFFF0000), _F32)
    lo = lax.bitcast_convert_type(u << 16, _F32)
    return jnp.concatenate([hi, lo], axis=1)


def _bias_tile_kernel(tbl_ref, shift_ref, out_ref, *, tile, num_buckets):
    di = pl.program_id(0)
    m = pl.program_id(1)
    half = num_buckets // 2
    max_exact = half // 2
    is_far = (di == 0) | (di == 4)

    @pl.when(is_far)
    def _():
        b_far = jnp.where(di == 0, half - 1, num_buckets - 1)
        out_ref[0, 0] = jnp.full((tile, tile), tbl_ref[b_far, m] * _LOG2E - shift_ref[m], _F32)

    @pl.when(jnp.logical_not(is_far))
    def _():
        delta = (di - 2) * tile
        kk = lax.broadcasted_iota(jnp.int32, (tile, tile), 0)
        qq = lax.broadcasted_iota(jnp.int32, (tile, tile), 1)
        rel = kk - qq + delta
        ret = jnp.where(rel > 0, half, 0)
        n = jnp.abs(rel)
        nf = jnp.maximum(n, 1).astype(_F32)
        large = max_exact + (jnp.log(nf / max_exact) / math.log(_MAX_DISTANCE / max_exact)
                             * (half - max_exact)).astype(jnp.int32)
        large = jnp.minimum(large, half - 1)
        bucket = ret + jnp.where(n < max_exact, n, large)
        acc = jnp.zeros((tile, tile), _F32)
        for b in range(num_buckets):
            acc = jnp.where(bucket == b, tbl_ref[b, m], acc)
        out_ref[0, 0] = acc * _LOG2E - shift_ref[m]


def _bias_tiles(rel_bias, shift, tile):
    num_buckets, n_maps = rel_bias.shape
    assert tile >= _MAX_DISTANCE
    return pl.pallas_call(
        functools.partial(_bias_tile_kernel, tile=tile, num_buckets=num_buckets),
        grid=(5, n_maps),
        in_specs=[pl.BlockSpec(memory_space=pltpu.SMEM), pl.BlockSpec(memory_space=pltpu.SMEM)],
        out_specs=pl.BlockSpec((1, 1, tile, tile), lambda di, m: (di, m // 2, 0, m % 2)),
        out_shape=jax.ShapeDtypeStruct((5, n_maps // 2, tile, 2 * tile), _F32),
        compiler_params=_cparams("arbitrary", "arbitrary"),
        name="bias_tiles",
    )(rel_bias.astype(_F32), shift.astype(_F32))


def _inproj_kernel(x_ref, gmix_ref, w_ref, gq_ref, gk_ref, bd_ref,
                   qT_ref, k_ref, vT_ref, b_ref, u_ref, *, d_att, d_conv, scale):
    x = x_ref[0]
    ms = jnp.mean(x * x, axis=-1, keepdims=True)
    h = (x * lax.rsqrt(ms + _EPS) * gmix_ref[...]).astype(_BF16)

    def proj(lo, n):
        return _dot(h, w_ref[:, lo:lo + n])

    tm = x.shape[0]
    n_maps = d_att // _ATT_HEAD_DIM
    zqT = proj(0, d_att).T.reshape(n_maps, _ATT_HEAD_DIM, tm)
    qms = jnp.mean(zqT * zqT, axis=1, keepdims=True)
    qn = zqT * lax.rsqrt(qms + _EPS) * gq_ref[...][None]
    qT_ref[0] = (qn * scale).reshape(d_att, tm).astype(_BF16)
    zk = proj(d_att, d_att)
    kms = _dot((zk * zk).astype(_BF16), bd_ref[...])
    k_ref[0] = (zk * lax.rsqrt(kms + _EPS) * gk_ref[...]).astype(_BF16)
    vT_ref[0] = proj(2 * d_att, d_att).T.astype(_BF16)
    b_ref[0] = proj(3 * d_att, d_conv).astype(_BF16)
    c = proj(3 * d_att + d_conv, d_conv)
    xc = proj(3 * d_att + 2 * d_conv, d_conv)
    u_ref[0] = (c * xc).astype(_BF16)


def _inproj(x, gmix, w_in, gq_col, gk_row, bd, *, d_att, d_conv, tm):
    B, S, D = x.shape
    d_in = w_in.shape[1]
    grid = (B, S // tm)
    tok = lambda width: pl.BlockSpec((1, tm, width), lambda b, i: (b, i, 0))
    tr = lambda rows: pl.BlockSpec((1, rows, tm), lambda b, i: (b, 0, i))
    full = lambda shape: pl.BlockSpec(shape, lambda b, i: (0,) * len(shape))
    return pl.pallas_call(
        functools.partial(_inproj_kernel, d_att=d_att, d_conv=d_conv,
                          scale=_LOG2E / math.sqrt(_ATT_HEAD_DIM)),
        grid=grid,
        in_specs=[tok(D), full((1, D)), full((D, d_in)), full((_ATT_HEAD_DIM, 1)), full((1, d_att)),
                  full((d_att, d_att))],
        out_specs=[tr(d_att), tok(d_att), tr(d_att), tok(d_conv), tok(d_conv)],
        out_shape=[jax.ShapeDtypeStruct((B, d_att, S), _BF16),
                   jax.ShapeDtypeStruct((B, S, d_att), _BF16),
                   jax.ShapeDtypeStruct((B, d_att, S), _BF16),
                   jax.ShapeDtypeStruct((B, S, d_conv), _BF16),
                   jax.ShapeDtypeStruct((B, S, d_conv), _BF16)],
        compiler_params=_cparams("parallel", "parallel"),
        name="inproj",
    )(x, gmix, w_in, gq_col, gk_row, bd)


def _attention_kernel(lam_ref, qT_ref, k_ref, vT_ref, bias_ref, gsub_ref, o_ref,
                      rhs_ref, s0_ref, s1_ref, p0_ref, p1_ref, mx0_ref, mx1_ref, al0_ref, al1_ref,
                      m_ref, l_ref, acc_ref, *, tile, out_scale):
    qi = pl.program_id(2)
    n_k = k_ref.shape[1] // tile
    hd = _ATT_HEAD_DIM
    s_refs, p_refs = (s0_ref, s1_ref), (p0_ref, p1_ref)
    mx_refs, al_refs = (mx0_ref, mx1_ref), (al0_ref, al1_ref)
    qT = qT_ref[0]
    row = lax.broadcasted_iota(jnp.int32, qT.shape, 0)
    zero = jnp.zeros_like(qT)
    rhs_ref[:, :tile] = jnp.where(row < hd, qT, zero)
    rhs_ref[:, tile:] = jnp.where(row >= hd, qT, zero)
    m_ref[...] = jnp.full(m_ref.shape, _NEG_BIG, _F32)
    l_ref[...] = jnp.zeros(l_ref.shape, _F32)
    acc_ref[...] = jnp.zeros(acc_ref.shape, _F32)

    row_blocks = [slice(r, r + _ATTN_ROWS) for r in range(0, tile, _ATTN_ROWS)]
    lane_blocks = [slice(c, c + _ATTN_LANES) for c in range(0, 2 * tile, _ATTN_LANES)]

    def fold8(x, op):
        return op(x.reshape(x.shape[0] // 8, 8, x.shape[1]), axis=0)

    def stage_a(kj, par):
        off = pl.multiple_of(kj * tile, tile)
        bidx = jnp.clip(kj - qi, -2, 2) + 2
        for cs in lane_blocks:
            rhs_c = rhs_ref[:, cs]
            mx = None
            for rs in row_blocks:
                k_blk = k_ref[0, pl.ds(off + rs.start, _ATTN_ROWS), :]
                s = _dot(k_blk, rhs_c) + bias_ref[bidx, 0, rs, cs]
                s_refs[par][rs, cs] = s
                part = fold8(s, jnp.max)
                mx = part if mx is None else jnp.maximum(mx, part)
            mx_refs[par][:, cs] = jnp.max(mx, axis=0, keepdims=True)

    def stage_b(par):
        m_old = m_ref[...]
        m_new = jnp.maximum(m_old, mx_refs[par][...])
        alpha = jnp.exp2(m_old - m_new)
        al_refs[par][...] = alpha
        m_ref[...] = m_new
        for cs in lane_blocks:
            m_c = m_new[:, cs]
            lsum = None
            for rs in row_blocks:
                p = jnp.exp2(s_refs[par][rs, cs] - m_c)
                p_refs[par][rs, cs] = p.astype(_BF16)
                part = fold8(p, jnp.sum)
                lsum = part if lsum is None else lsum + part
            l_ref[:, cs] = alpha[:, cs] * l_ref[:, cs] + jnp.sum(lsum, axis=0, keepdims=True)

    def stage_c(kj, par):
        off = pl.multiple_of(kj * tile, tile)
        for cs in lane_blocks:
            pv = _dot(vT_ref[0, :, pl.ds(off, tile)], p_refs[par][:, cs])
            acc_ref[:, cs] = al_refs[par][:, cs] * acc_ref[:, cs] + pv

    stage_a(0, 0)
    stage_a(1, 1)
    stage_b(0)

    def pair(t, carry):
        j = 2 * t + 1
        stage_a(j + 1, 0)
        stage_b(1)
        stage_c(j - 1, 0)
        stage_a(j + 2, 1)
        stage_b(0)
        stage_c(j, 1)
        return carry

    lax.fori_loop(0, (n_k - 2) // 2, pair, 0)
    stage_b(1)
    stage_c(n_k - 2, 0)
    stage_c(n_k - 1, 1)

    o_ref[0, 0] = _attention_output(lam_ref, l_ref[...], acc_ref[...], gsub_ref, tile=tile,
                                    out_scale=out_scale)


def _attention_output(lam_ref, l, acc, gsub_ref, *, tile, out_scale):
    o = acc * (1.0 / l)
    oT = o[:, :tile] - lam_ref[0] * o[:, tile:]
    ms = jnp.mean(oT * oT, axis=0, keepdims=True)
    oT = oT * lax.rsqrt(ms + _EPS) * gsub_ref[...] * out_scale
    return oT.T.astype(_BF16)


def _attention_bounded_kernel(lam_ref, qT_ref, k_ref, vT_ref, bias_ref, gsub_ref, o_ref,
                              rhs_ref, p0_ref, p1_ref, l_ref, acc_ref, *, tile, out_scale):
    seq = k_ref.shape[1]
    n_k = seq // tile
    hd = _ATT_HEAD_DIM
    p_refs = (p0_ref, p1_ref)
    row_blocks = [slice(r, r + _ATTN_ROWS) for r in range(0, tile, _ATTN_ROWS)]
    lane_blocks = [slice(c, c + _ATTN_LANES) for c in range(0, 2 * tile, _ATTN_LANES)]

    def setup_q(qi, q_load):
        qpar = qi % 2
        qT = qT_ref[0, :, pl.ds(pl.multiple_of(q_load * tile, tile), tile)]
        row = lax.broadcasted_iota(jnp.int32, qT.shape, 0)
        zero = jnp.zeros_like(qT)
        rhs_ref[qpar, :, :tile] = jnp.where(row < hd, qT, zero)
        rhs_ref[qpar, :, tile:] = jnp.where(row >= hd, qT, zero)
        l_ref[qpar] = jnp.zeros(l_ref.shape[1:], _F32)

    def stage_ab(qi, kj, par):
        qpar = qi % 2
        off = pl.multiple_of(kj * tile, tile)
        bidx = jnp.clip(kj - qi, -2, 2) + 2
        k_tile = k_ref[0, pl.ds(off, tile), :]
        for cs in lane_blocks:
            s = _dot(k_tile, rhs_ref[qpar, :, cs])
            lsum = None
            for rs in row_blocks:
                p = jnp.exp2(s[rs, :] + bias_ref[bidx, 0, rs, cs])
                p_refs[par][rs, cs] = p.astype(_BF16)
                part = jnp.sum(p.reshape(_ATTN_ROWS // 8, 8, _ATTN_LANES), axis=0)
                lsum = part if lsum is None else lsum + part
            l_ref[qpar, :, cs] += jnp.sum(lsum, axis=0, keepdims=True)

    def stage_c(kj, par):
        off = pl.multiple_of(kj * tile, tile)
        for cs in lane_blocks:
            acc_ref[:, cs] += _dot(vT_ref[0, :, pl.ds(off, tile)], p_refs[par][:, cs])

    n_q = seq // tile
    acc_ref[...] = jnp.zeros(acc_ref.shape, _F32)
    setup_q(0, 0)
    stage_ab(0, 0, 0)

    def q_tile_body(qi, carry):
        def pair(t, c):
            j = 2 * t
            stage_ab(qi, j + 1, 1)
            stage_c(j, 0)
            stage_ab(qi, j + 2, 0)
            stage_c(j + 1, 1)
            return c

        lax.fori_loop(0, n_k // 2 - 1, pair, 0)
        stage_ab(qi, n_k - 1, 1)
        stage_c(n_k - 2, 0)
        q_next = qi + 1
        setup_q(q_next, jnp.minimum(q_next, n_q - 1))
        stage_ab(q_next, 0, 0)
        stage_c(n_k - 1, 1)
        o_ref[0, 0, pl.ds(pl.multiple_of(qi * tile, tile), tile), :] = _attention_output(
            lam_ref, l_ref[qi % 2], acc_ref[...], gsub_ref, tile=tile, out_scale=out_scale)
        acc_ref[...] = jnp.zeros(acc_ref.shape, _F32)
        return carry

    lax.fori_loop(0, n_q, q_tile_body, 0)


def _lambda_kernel(q1_ref, k1_ref, q2_ref, k2_ref, out_ref, *, lambda_init):
    a = jnp.sum(q1_ref[...] * k1_ref[...], axis=-1, keepdims=True)
    b = jnp.sum(q2_ref[...] * k2_ref[...], axis=-1, keepdims=True)
    out_ref[...] = jnp.broadcast_to(jnp.exp(a) - jnp.exp(b) + lambda_init, out_ref.shape)


def _lambda(q1, k1, q2, k2, lambda_init):
    out = pl.pallas_call(
        functools.partial(_lambda_kernel, lambda_init=lambda_init),
        out_shape=jax.ShapeDtypeStruct((1, _LANES), _F32),
        name="lambda_scalar",
    )(q1, k1, q2, k2)
    return out[0, :1]


def _attention(bound_ok, lam, qT, k, vT, bias_tiles, gsub_col, *, tile, out_scale):
    B, d_att, S = qT.shape
    v_dim = 2 * _ATT_HEAD_DIM
    n_heads = d_att // v_dim
    assert S % (2 * tile) == 0
    row = lambda: pltpu.VMEM((1, 2 * tile), _F32)
    s_buf = lambda: pltpu.VMEM((tile, 2 * tile), _F32)
    p_buf = lambda: pltpu.VMEM((tile, 2 * tile), _BF16)
    rhs = pltpu.VMEM((v_dim, 2 * tile), _BF16)
    acc = pltpu.VMEM((v_dim, 2 * tile), _F32)

    out_shape = jax.ShapeDtypeStruct((B, n_heads, S, v_dim), _BF16)
    operands = (lam, qT, k, vT, bias_tiles, gsub_col)

    def bounded():
        return pl.pallas_call(
            functools.partial(_attention_bounded_kernel, tile=tile, out_scale=out_scale),
            grid=(n_heads, B),
            in_specs=[pl.BlockSpec(memory_space=pltpu.SMEM),
                      pl.BlockSpec((1, v_dim, S), lambda h, b: (b, h, 0)),
                      pl.BlockSpec((1, S, v_dim), lambda h, b: (b, 0, h)),
                      pl.BlockSpec((1, v_dim, S), lambda h, b: (b, h, 0)),
                      pl.BlockSpec((5, 1, tile, 2 * tile), lambda h, b: (0, h, 0, 0)),
                      pl.BlockSpec((v_dim, 1), lambda h, b: (0, 0))],
            out_specs=pl.BlockSpec((1, 1, S, v_dim), lambda h, b: (b, h, 0, 0)),
            out_shape=out_shape,
            scratch_shapes=[pltpu.VMEM((2, v_dim, 2 * tile), _BF16), p_buf(), p_buf(),
                            pltpu.VMEM((2, 1, 2 * tile), _F32), acc],
            compiler_params=_cparams("parallel", "parallel"),
            name="diff_attention_bounded",
        )(*operands)

    def exact():
        return pl.pallas_call(
            functools.partial(_attention_kernel, tile=tile, out_scale=out_scale),
            grid=(n_heads, B, S // tile),
            in_specs=[pl.BlockSpec(memory_space=pltpu.SMEM),
                      pl.BlockSpec((1, v_dim, tile), lambda h, b, i: (b, h, i)),
                      pl.BlockSpec((1, S, v_dim), lambda h, b, i: (b, 0, h)),
                      pl.BlockSpec((1, v_dim, S), lambda h, b, i: (b, h, 0)),
                      pl.BlockSpec((5, 1, tile, 2 * tile), lambda h, b, i: (0, h, 0, 0)),
                      pl.BlockSpec((v_dim, 1), lambda h, b, i: (0, 0))],
            out_specs=pl.BlockSpec((1, 1, tile, v_dim), lambda h, b, i: (b, h, i, 0)),
            out_shape=out_shape,
            scratch_shapes=[rhs, s_buf(), s_buf(), p_buf(), p_buf(), row(), row(), row(), row(), row(),
                            row(), acc],
            compiler_params=_cparams("parallel", "parallel", "parallel"),
            name="diff_attention",
        )(*operands)

    return lax.cond(bound_ok, bounded, exact)


def _outproj_kernel(oatt_ref, b_ref, u_ref, uprev_ref, unext_ref, x_ref, cw_ref, wout_ref, gffn_ref,
                    wr_ref, br_ref, x1_ref, tw_ref, route_ref, *, d_att, n_experts):
    i = pl.program_id(1)
    n_i = pl.num_programs(1)
    u = u_ref[0].astype(_F32)
    tm = u.shape[0]
    prev_row = jnp.where(i > 0, uprev_ref[0, _BF16_SUBLANES - 1:_BF16_SUBLANES, :].astype(_F32), 0.0)
    next_row = jnp.where(i < n_i - 1, unext_ref[0, 0:1, :].astype(_F32), 0.0)
    rows = lax.broadcasted_iota(jnp.int32, (tm, 1), 0)
    u_prev = jnp.where(rows == 0, prev_row, pltpu.roll(u, 1, axis=0))
    u_next = jnp.where(rows == tm - 1, next_row, pltpu.roll(u, tm - 1, axis=0))
    conv = cw_ref[0:1, :] * u_prev + cw_ref[1:2, :] * u + cw_ref[2:3, :] * u_next
    o_conv = (b_ref[0].astype(_F32) * conv).astype(_BF16)
    for rs in (slice(0, tm // 2), slice(tm // 2, tm)):
        o_att = jnp.concatenate([oatt_ref[0, h, rs, :] for h in range(oatt_ref.shape[1])], axis=1)
        mix = _dot(o_att, wout_ref[:d_att, :]) + _dot(o_conv[rs], wout_ref[d_att:, :])
        x1 = x_ref[0, rs, :] + mix
        x1_ref[0, rs, :] = x1
        ms = jnp.mean(x1 * x1, axis=-1, keepdims=True)
        t = x1 * lax.rsqrt(ms + _EPS) * gffn_ref[...]
        t_hi = t.astype(_BF16)
        half = x1.shape[1] // 2
        tw_ref[0, rs, :half] = _pack_bf16_pairs(t)
        t_lo = (t - t_hi.astype(_F32)).astype(_BF16)
        a_hi = _dot(t_hi, wr_ref[...])
        a_lo = _dot(t_lo, wr_ref[...])
        logits = a_hi[:, :_LANES] + a_hi[:, _LANES:] + a_lo[:, :_LANES] + br_ref[...]
        n_groups = n_experts // _EXPERTS_PER_GROUP
        lane = lax.broadcasted_iota(jnp.int32, (1, _LANES), 1)
        lane_f = lane.astype(_F32)
        big = float(_LANES)
        gmask = (lane >= n_experts) & (lane < n_experts + n_groups)
        gl = jnp.where(gmask, logits, _NEG_BIG)
        gmax = jnp.max(gl, axis=-1, keepdims=True)
        gsum = jnp.sum(jnp.where(gmask, jnp.exp(gl - gmax), 0.0), axis=-1, keepdims=True)
        g_w = 1.0 / gsum
        g_idx = jnp.min(jnp.where(gmask & (gl == gmax), lane_f - n_experts, big), axis=-1, keepdims=True)
        lo = g_idx * _EXPERTS_PER_GROUP
        emask = (lane_f >= lo) & (lane_f < lo + _EXPERTS_PER_GROUP)
        el = jnp.where(emask, logits, _NEG_BIG)
        emax = jnp.max(el, axis=-1, keepdims=True)
        ep = jnp.where(emask, jnp.exp(el - emax), 0.0)
        p_exp = ep / jnp.sum(ep, axis=-1, keepdims=True)
        top1 = jnp.max(p_exp, axis=-1, keepdims=True)
        i1 = jnp.min(jnp.where(emask & (p_exp == top1), lane_f, big), axis=-1, keepdims=True)
        rest = jnp.where(emask & (lane_f != i1), p_exp, -1.0)
        top2 = jnp.max(rest, axis=-1, keepdims=True)
        i2 = jnp.min(jnp.where(rest == top2, lane_f, big), axis=-1, keepdims=True)
        denom = top1 + top2
        gates = jnp.where(lane_f == i1 - lo, g_w * (top1 / denom),
                          jnp.where(lane_f == i2 - lo, g_w * (top2 / denom), 0.0))
        tw_ref[0, rs, half:] = lax.bitcast_convert_type(gates, jnp.uint32)
        route_ref[0, :, rs] = jnp.broadcast_to(g_idx, (tm // 2, _LANES)).T[:_F32_SUBLANES, :]


def _outproj(oatt, b, u, x, conv_w, w_out, gffn, wr, br, *, n_experts, tm):
    B, S, D = x.shape
    n_heads, v_dim = oatt.shape[1], oatt.shape[3]
    d_att = n_heads * v_dim
    d_conv = b.shape[-1]
    hb = _BF16_SUBLANES
    per_tile = tm // hb
    n_halo = S // hb
    tok = lambda width: pl.BlockSpec((1, tm, width), lambda bb, i: (bb, i, 0))
    full = lambda shape: pl.BlockSpec(shape, lambda bb, i: (0,) * len(shape))
    return pl.pallas_call(
        functools.partial(_outproj_kernel, d_att=d_att, n_experts=n_experts),
        grid=(B, S // tm),
        in_specs=[pl.BlockSpec((1, n_heads, tm, v_dim), lambda bb, i: (bb, 0, i, 0)), tok(d_conv), tok(d_conv),
                  pl.BlockSpec((1, hb, d_conv), lambda bb, i: (bb, jnp.maximum(i * per_tile - 1, 0), 0)),
                  pl.BlockSpec((1, hb, d_conv),
                               lambda bb, i: (bb, jnp.minimum((i + 1) * per_tile, n_halo - 1), 0)),
                  tok(D), full(conv_w.shape), full(w_out.shape), full((1, D)), full(wr.shape),
                  full((1, _LANES))],
        out_specs=[tok(D), tok(D // 2 + _LANES),
                   pl.BlockSpec((1, _F32_SUBLANES, tm), lambda bb, i: (bb, 0, i))],
        out_shape=[jax.ShapeDtypeStruct((B, S, D), _F32),
                   jax.ShapeDtypeStruct((B, S, D // 2 + _LANES), jnp.uint32),
                   jax.ShapeDtypeStruct((B, _F32_SUBLANES, S), _F32)],
        compiler_params=_cparams("parallel", "parallel"),
        name="outproj_router",
    )(oatt, b, u, u, u, x, conv_w, w_out, gffn, wr, br)


def _row_copies_wait(src_rows, dst_rows, sem):
    pltpu.make_async_copy(src_rows, dst_rows, sem).wait()


def _stage_indices(idx_ref, idx_smem, slot, idx_sem):
    c = pltpu.make_async_copy(idx_ref.at[0, 0], idx_smem.at[slot], idx_sem)
    c.start()
    c.wait()


def _scatter_rows_kernel(pads_ref, pos_a_ref, pos_b_ref, x_ref, out_hbm, idx_smem, idx_sem, stage, sem, *,
                         n_groups):
    j = pl.program_id(0)
    tm = stage.shape[1]

    def burst(slot):
        stage[slot] = x_ref[slot * tm:(slot + 1) * tm, :]
        for r in range(tm):
            pltpu.make_async_copy(stage.at[slot, pl.ds(r, 1), :], out_hbm.at[pl.ds(idx_smem[slot, r], 1), :],
                                  sem.at[slot]).start(priority=r % _DMA_PRIORITIES)

    def burst_wait(slot):
        _row_copies_wait(stage.at[slot], out_hbm.at[pl.ds(0, tm), :], sem.at[slot])

    _stage_indices(pos_a_ref, idx_smem, 0, idx_sem)
    _stage_indices(pos_b_ref, idx_smem, 1, idx_sem)
    burst(0)

    @pl.when(j > 0)
    def _():
        burst_wait(1)

    burst(1)
    burst_wait(0)

    @pl.when(j == pl.num_programs(0) - 1)
    def _():
        burst_wait(1)
        for g in range(n_groups):
            lo = pads_ref[g]
            count = pads_ref[n_groups + g] - lo

            def fill(k, carry, lo=lo):
                pltpu.make_async_copy(stage.at[0, pl.ds(0, 1), :], out_hbm.at[pl.ds(lo + k, 1), :],
                                      sem.at[0]).start()
                return carry

            def drain(k, carry):
                _row_copies_wait(stage.at[0, pl.ds(0, 1), :], out_hbm.at[pl.ds(0, 1), :], sem.at[0])
                return carry

            lax.fori_loop(0, count, fill, 0)
            lax.fori_loop(0, count, drain, 0)


def _scatter_rows(pads, pos, x, n_out_rows, *, tm):
    T, width = x.shape
    n_tiles = T // tm
    assert n_tiles % 2 == 0
    idx = pos.reshape(n_tiles, 1, tm)
    return pl.pallas_call(
        functools.partial(_scatter_rows_kernel, n_groups=pads.shape[0] // 2),
        grid_spec=pltpu.PrefetchScalarGridSpec(
            num_scalar_prefetch=1,
            grid=(n_tiles // 2,),
            in_specs=[pl.BlockSpec((1, 1, tm), lambda j, pads: (2 * j, 0, 0)),
                      pl.BlockSpec((1, 1, tm), lambda j, pads: (2 * j + 1, 0, 0)),
                      pl.BlockSpec((2 * tm, width), lambda j, pads: (j, 0))],
            out_specs=pl.BlockSpec(memory_space=pl.ANY),
            scratch_shapes=[pltpu.SMEM((2, tm), jnp.int32), pltpu.SemaphoreType.DMA,
                            pltpu.VMEM((2, tm, width), x.dtype), pltpu.SemaphoreType.DMA((2,))]),
        out_shape=jax.ShapeDtypeStruct((n_out_rows, width), x.dtype),
        compiler_params=_cparams("arbitrary"),
        name="dispatch_rows",
    )(pads, idx, idx, x)


def _moe_sorted_kernel(tile_group_ref, tw_ref, wgu_ref, wdn_ref, out_ref, *, d_expert):
    del tile_group_ref
    half = out_ref.shape[1]
    t = _unpack_bf16_pairs(tw_ref[:, :half]).astype(_BF16)
    gates = lax.bitcast_convert_type(tw_ref[:, half:], _F32)
    parts = []
    for e in range(_EXPERTS_PER_GROUP):
        gu = _dot(t, wgu_ref[e])
        g_lin = gu[:, :d_expert]
        act = g_lin * jax.nn.sigmoid(g_lin) * gu[:, d_expert:]
        parts.append((act * gates[:, e:e + 1]).astype(_BF16))
    out_ref[...] = _pack_bf16_pairs(_dot(jnp.concatenate(parts, axis=1), wdn_ref[0]))


def _moe_sorted(tile_group, tw_sorted, wgu, wdn, *, d_expert, tm):
    rows, width = tw_sorted.shape
    half = width - _LANES
    grp = lambda shape: pl.BlockSpec((1,) + shape, lambda i, tg: (tg[i], 0, 0))
    return pl.pallas_call(
        functools.partial(_moe_sorted_kernel, d_expert=d_expert),
        grid_spec=pltpu.PrefetchScalarGridSpec(
            num_scalar_prefetch=1,
            grid=(rows // tm,),
            in_specs=[pl.BlockSpec((tm, width), lambda i, tg: (i, 0)),
                      pl.BlockSpec((_EXPERTS_PER_GROUP,) + wgu.shape[1:], lambda i, tg: (tg[i], 0, 0)),
                      grp(wdn.shape[1:])],
            out_specs=pl.BlockSpec((tm, half), lambda i, tg: (i, 0))),
        out_shape=jax.ShapeDtypeStruct((rows, half), jnp.uint32),
        compiler_params=_cparams("parallel"),
        name="moe_sorted",
    )(tile_group, tw_sorted, wgu, wdn)


def _ple_kernel(idx_first_ref, idx_b_ref, idx_a_ref, moe_hbm, x1_ref, p_ref, wproj_ref, wgate_ref,
                gple_ref, y_ref, idx_smem, idx_sem, buf, sem):
    j = pl.program_id(0)
    tm = buf.shape[1]

    def rows_start(idx_slot, slot, lo=0, hi=None):
        for r in range(lo, tm if hi is None else hi):
            pltpu.make_async_copy(moe_hbm.at[pl.ds(idx_smem[idx_slot, r], 1), :],
                                  buf.at[slot, pl.ds(r, 1), :], sem.at[slot]).start(
                                      priority=r % _DMA_PRIORITIES)

    def rows_wait(slot):
        _row_copies_wait(moe_hbm.at[pl.ds(0, tm), :], buf.at[slot], sem.at[slot])

    def compute_rows(base, slot, lo, hi):
        rows = slice(base + lo, base + hi)
        x2 = x1_ref[rows, :] + _unpack_bf16_pairs(buf[slot, lo:hi, :])
        e_raw = _dot(p_ref[rows, :].astype(_BF16), wproj_ref[...])
        ms = jnp.mean(e_raw * e_raw, axis=-1, keepdims=True)
        emb = e_raw * lax.rsqrt(ms + _EPS) * gple_ref[...]
        gate_p = jax.nn.sigmoid(_dot(x2.astype(_BF16), wgate_ref[...]))
        y_ref[rows, :] = x2 + gate_p * emb

    def compute_and_fetch(base, slot, idx_slot, other):
        for lo in range(0, tm, _PLE_CHUNK):
            rows_start(idx_slot, other, lo, lo + _PLE_CHUNK)
            compute_rows(base, slot, lo, lo + _PLE_CHUNK)

    @pl.when(j == 0)
    def _():
        _stage_indices(idx_first_ref, idx_smem, 0, idx_sem)
        rows_start(0, 0)

    _stage_indices(idx_b_ref, idx_smem, 0, idx_sem)
    _stage_indices(idx_a_ref, idx_smem, 1, idx_sem)
    rows_wait(0)
    compute_and_fetch(0, 0, 0, 1)
    rows_wait(1)
    compute_and_fetch(tm, 1, 1, 0)

    @pl.when(j == pl.num_programs(0) - 1)
    def _():
        rows_wait(0)


def _ple(pos, moe_sorted, x1, p, wproj, wgate, gple, *, tm):
    T, D = x1.shape
    half = moe_sorted.shape[1]
    n_tiles = T // tm
    assert n_tiles % 2 == 0
    idx = pos.reshape(n_tiles, 1, tm)
    tok = lambda width: pl.BlockSpec((2 * tm, width), lambda j: (j, 0))
    full = lambda shape: pl.BlockSpec(shape, lambda j: (0,) * len(shape))
    idx_spec = lambda tile_of_step: pl.BlockSpec((1, 1, tm), lambda j: (tile_of_step(j), 0, 0))
    return pl.pallas_call(
        _ple_kernel,
        grid=(n_tiles // 2,),
        in_specs=[idx_spec(lambda j: 0), idx_spec(lambda j: 2 * j + 1),
                  idx_spec(lambda j: jnp.minimum(2 * j + 2, n_tiles - 1)),
                  pl.BlockSpec(memory_space=pl.ANY), tok(D), tok(p.shape[-1]), full(wproj.shape),
                  full(wgate.shape), full((1, D))],
        out_specs=tok(D),
        out_shape=jax.ShapeDtypeStruct((T, D), _F32),
        scratch_shapes=[pltpu.SMEM((2, tm), jnp.int32), pltpu.SemaphoreType.DMA,
                        pltpu.VMEM((2, tm, half), jnp.uint32), pltpu.SemaphoreType.DMA((2,))],
        compiler_params=_cparams("arbitrary"),
        name="ple_gate",
    )(idx, idx, idx, moe_sorted, x1, p, wproj, wgate, gple)


def _dispatch_plan(group_of_token, n_groups, tm):
    T = group_of_token.shape[0]
    n_tiles = T // tm + n_groups - 1
    onehot = (group_of_token[:, None] == jnp.arange(n_groups, dtype=jnp.int32)[None, :]).astype(jnp.int32)
    rank = jnp.take_along_axis(jnp.cumsum(onehot, axis=0), group_of_token[:, None], axis=1)[:, 0] - 1
    count = jnp.sum(onehot, axis=0)
    tiles_per_group = (count + tm - 1) // tm
    tile_end = jnp.cumsum(tiles_per_group)
    tile_start = tile_end - tiles_per_group
    pos = tile_start[group_of_token] * tm + rank
    tile_group = jnp.searchsorted(tile_end, jnp.arange(n_tiles, dtype=jnp.int32), side="right")
    tile_group = jnp.minimum(tile_group, n_groups - 1).astype(jnp.int32)
    pad_lo = tile_start * tm + count
    pad_hi = (tile_end * tm).at[n_groups - 1].set(n_tiles * tm)
    pads = jnp.concatenate([pad_lo, pad_hi]).astype(jnp.int32)
    return pos.astype(jnp.int32), tile_group, pads, n_tiles * tm


def _prepare_layer(i, norm_mix, w_in, q_norm, k_norm, lambda_q1, lambda_k1, lambda_q2, lambda_k2,
                   attn_sub_norm, conv_w, w_out, norm_ffn, w_group, b_group, w_erouter, b_erouter,
                   w_gate_up, w_down, w_ple_proj, w_ple_gate, ple_norm):
    D = w_in.shape[1]
    d_mix = w_out.shape[1]
    d_att = d_mix // 2
    d_conv = d_mix - d_att
    n_maps = d_att // _ATT_HEAD_DIM
    n_experts = w_gate_up.shape[1]
    n_groups = w_group.shape[-1]
    d_expert = w_down.shape[2]
    assert n_experts == n_groups * _EXPERTS_PER_GROUP and n_experts + n_groups <= _LANES
    lambda_init = 0.8 - 0.6 * math.exp(-0.3 * i)
    head_of = jnp.arange(d_att) // _ATT_HEAD_DIM
    bd = jnp.where(head_of[:, None] == head_of[None, :], 1.0 / _ATT_HEAD_DIM, 0.0).astype(_BF16)
    wr = jnp.zeros((D, _LANES), _F32)
    wr = wr.at[:, :n_experts].set(w_erouter[i]).at[:, n_experts:n_experts + n_groups].set(w_group[i])
    wr_hi = wr.astype(_BF16)
    wr_lo = (wr - wr_hi.astype(_F32)).astype(_BF16)
    br = jnp.zeros((1, _LANES), _F32)
    br = br.at[0, :n_experts].set(b_erouter[i]).at[0, n_experts:n_experts + n_groups].set(b_group[i])
    wgu = w_gate_up[i]
    wdn = w_down[i].reshape(n_groups, _EXPERTS_PER_GROUP * d_expert, D)
    return dict(
        d_att=d_att, d_conv=d_conv, n_experts=n_experts, d_expert=d_expert, lambda_init=lambda_init,
        gmix=norm_mix[i][None, :], w_in=w_in[i].astype(_BF16),
        gq_col=q_norm[i][:, None], gk_row=jnp.tile(k_norm[i], n_maps)[None, :], bd=bd,
        lam_vecs=tuple(v[i][None, :] for v in (lambda_q1, lambda_k1, lambda_q2, lambda_k2)),
        gsub_col=attn_sub_norm[i][:, None], conv_w=conv_w[i], w_out=w_out[i].astype(_BF16),
        gffn=norm_ffn[i][None, :], wr=jnp.concatenate([wr_hi, wr_lo], axis=1), br=br,
        wgu=wgu.astype(_BF16), wdn=wdn.astype(_BF16), wproj=w_ple_proj[i].astype(_BF16),
        wgate=w_ple_gate[i].astype(_BF16), gple=ple_norm[i][None, :])


def _score_bound(q_gain, k_gain, rel_bias):
    hd = _ATT_HEAD_DIM
    rounding_slack = 1.02
    q_norm_max = math.sqrt(hd) * jnp.max(jnp.abs(q_gain)) * (_LOG2E / math.sqrt(hd))
    k_norm_max = math.sqrt(hd) * jnp.max(jnp.abs(k_gain))
    qk = rounding_slack * q_norm_max * k_norm_max
    b2 = rel_bias.astype(_F32) * _LOG2E
    b_max, b_min = jnp.max(b2, axis=0), jnp.min(b2, axis=0)
    ok = jnp.all(2.0 * qk + (b_max - b_min) <= _SAFE_EXPONENT_SPAN)
    ok = ok & jnp.isfinite(qk) & jnp.all(jnp.isfinite(b2))
    return ok, jnp.where(ok, qk + b_max, jnp.zeros_like(b_max))


def _layer(x, p_i, L, bias_tiles, bound_ok, lam):
    B, S, D = x.shape
    tm = _TOKEN_TILE
    tile = _ATTN_TILE
    assert S % tm == 0 and S % tile == 0 and tm % _BF16_SUBLANES == 0
    qT, k, vT, b, u = _inproj(x, L["gmix"], L["w_in"], L["gq_col"], L["gk_row"], L["bd"],
                              d_att=L["d_att"], d_conv=L["d_conv"], tm=tm)
    oatt = _attention(bound_ok, lam, qT, k, vT, bias_tiles, L["gsub_col"], tile=tile,
                      out_scale=1.0 - L["lambda_init"])
    x1, tw, route = _outproj(oatt, b, u, x, L["conv_w"], L["w_out"], L["gffn"], L["wr"], L["br"],
                             n_experts=L["n_experts"], tm=2 * tm)
    x1 = x1.reshape(B * S, D)
    tw = tw.reshape(B * S, D // 2 + _LANES)
    group_of_token = route[:, 0, :].reshape(B * S).astype(jnp.int32)
    pos, tile_group, pads, sorted_rows = _dispatch_plan(group_of_token, L["wdn"].shape[0], tm)
    tw_sorted = _scatter_rows(pads, pos, tw, sorted_rows, tm=tm)
    moe = _moe_sorted(tile_group, tw_sorted, L["wgu"], L["wdn"], d_expert=L["d_expert"], tm=tm)
    y = _ple(pos, moe, x1, p_i.reshape(B * S, -1), L["wproj"], L["wgate"], L["gple"], tm=tm)
    return y.reshape(B, S, D)


def kernel(x_prompt, x_sample, p_prompt, p_sample, norm_mix, w_in, q_norm, k_norm, lambda_q1, lambda_k1, lambda_q2, lambda_k2, attn_sub_norm, conv_w, w_out, rel_bias, norm_ffn, w_group, b_group, w_erouter, b_erouter, w_gate_up, w_down, w_ple_proj, w_ple_gate, ple_norm):
    depth = w_in.shape[0]
    layers = []
    for i in range(depth):
        L = _prepare_layer(i, norm_mix, w_in, q_norm, k_norm, lambda_q1, lambda_k1, lambda_q2,
                           lambda_k2, attn_sub_norm, conv_w, w_out, norm_ffn, w_group, b_group,
                           w_erouter, b_erouter, w_gate_up, w_down, w_ple_proj, w_ple_gate, ple_norm)
        bound_ok, shift = _score_bound(q_norm[i], k_norm[i], rel_bias)
        layers.append((L, _bias_tiles(rel_bias, shift, _ATTN_TILE), bound_ok,
                       _lambda(*L["lam_vecs"], L["lambda_init"])))

    def encode(x, p):
        for i, (L, bias_tiles, bound_ok, lam) in enumerate(layers):
            x = _layer(x, p[i], L, bias_tiles, bound_ok, lam)
        return x

    return (encode(x_prompt, p_prompt), encode(x_sample, p_sample))
```
